```python
import jax
import jax.numpy as jnp
from jax import lax
import numpy as np

D_MODEL = 1024
BATCH = 32
SEQ = 2048
DEPTH = 2
DEC_BATCH = 4
DEC_SEQ = 8192
PAST_LEN = 128

RW_HEADS = 4
RW_HEAD_DIM = 64
RW_WIDTH = RW_HEADS * RW_HEAD_DIM
DECAY_LORA = 64
AAA_LORA = 64
GATE_LORA = 128
N_DIR = 2
GN_EPS = 64e-5

ATT_HEAD_DIM = 64
ATT_SLOTS = 4
DILATION_PAIRS = ((128, 1), (512, 4), (2048, 16))
N_DIL = len(DILATION_PAIRS)
ATT_WIDTH = N_DIL * ATT_SLOTS * ATT_HEAD_DIM
ATT_OUT = ATT_SLOTS * ATT_HEAD_DIM
ROPE_THETA = 500000.0
ROPE_DIMS = ATT_HEAD_DIM // 4

MIX_WIDTH = ATT_WIDTH + RW_WIDTH
OUT_IN = ATT_OUT + RW_WIDTH

ATT_COLS = 3 * ATT_WIDTH
SHIFT_SIZES = (RW_WIDTH, RW_WIDTH, RW_WIDTH, N_DIR * DECAY_LORA, N_DIR * AAA_LORA, GATE_LORA)
SHIFT_COLS = sum(SHIFT_SIZES)
SHIFT_SPLITS = [int(c) for c in np.cumsum(SHIFT_SIZES)[:-1]]
IN_COLS = ATT_COLS + SHIFT_COLS

N_EXPERTS = 16
EC_CAPACITY_FACTOR = 2
EXPERT_FF = 2048

NORM_EPS = 1e-6
NEG_INF = -1e30

kernel_name = 'hybrid_rwkv7_dilated_attn_ec_moe_encoder'


def rms_norm(x, g):
    x32 = x.astype(jnp.float32)
    y = x32 * lax.rsqrt(jnp.mean(x32 * x32, -1, keepdims=True) + NORM_EPS) * g.astype(jnp.float32)
    return y.astype(x.dtype)


def partial_rope(x, pos):
    half = ROPE_DIMS // 2
    inv = jnp.power(jnp.float32(ROPE_THETA), -jnp.arange(half, dtype=jnp.float32) * 2.0 / ROPE_DIMS)
    ang = pos.astype(jnp.float32)[:, None] * inv[None, :]
    cos = jnp.cos(ang)[None, :, None, None, :]
    sin = jnp.sin(ang)[None, :, None, None, :]
    x32 = x.astype(jnp.float32)
    x1 = x32[..., :half]
    x2 = x32[..., half:ROPE_DIMS]
    out = jnp.concatenate([x1 * cos - x2 * sin, x2 * cos + x1 * sin, x32[..., ROPE_DIMS:]], -1)
    return out.astype(x.dtype)


def band_attention(q, k, v, half):
    n, L, h, dh = q.shape
    blk = half
    nb = -(-L // blk)
    pad = nb * blk - L
    qb = jnp.pad(q, ((0, 0), (0, pad), (0, 0), (0, 0))).reshape(n, nb, blk, h, dh)
    kp = jnp.pad(k, ((0, 0), (blk, pad + blk), (0, 0), (0, 0))).reshape(n, nb + 2, blk, h, dh)
    vp = jnp.pad(v, ((0, 0), (blk, pad + blk), (0, 0), (0, 0))).reshape(n, nb + 2, blk, h, dh)
    kb = jnp.concatenate([kp[:, :-2], kp[:, 1:-1], kp[:, 2:]], axis=2)
    vb = jnp.concatenate([vp[:, :-2], vp[:, 1:-1], vp[:, 2:]], axis=2)
    s = jnp.einsum('nbqhd,nbkhd->nbhqk', qb, kb, preferred_element_type=jnp.float32) * (dh ** -0.5)
    qpos = jnp.arange(nb)[:, None] * blk + jnp.arange(blk)[None, :]
    kpos = jnp.arange(nb)[:, None] * blk - blk + jnp.arange(3 * blk)[None, :]
    dist = kpos[:, None, :] - qpos[:, :, None]
    valid = (jnp.abs(dist) <= half) & (kpos[:, None, :] >= 0) & (kpos[:, None, :] < L)
    s = jnp.where(valid[None, :, None, :, :], s, NEG_INF)
    m = jnp.max(s, -1, keepdims=True)
    p = jnp.exp(s - m)
    den = jnp.sum(p, -1, keepdims=True)
    o = jnp.einsum('nbhqk,nbkhd->nbqhd', p / den, vb.astype(jnp.float32))
    lse = jnp.transpose((m + jnp.log(den))[..., 0], (0, 1, 3, 2))
    o = o.reshape(n, nb * blk, h, dh)[:, :L]
    lse = lse.reshape(n, nb * blk, h)[:, :L]
    return o, lse


def dilated_attention(q, k, v):
    b, t = q.shape[:2]
    outs, lses = [], []
    for g, (window, dil) in enumerate(DILATION_PAIRS):
        half = window // (2 * dil)
        sub = t // dil

        def gather_stride(z):
            z = z[:, :, g].reshape(b, sub, dil, ATT_SLOTS, ATT_HEAD_DIM)
            return jnp.transpose(z, (0, 2, 1, 3, 4)).reshape(b * dil, sub, ATT_SLOTS, ATT_HEAD_DIM)

        o, lse = band_attention(gather_stride(q), gather_stride(k), gather_stride(v), half)
        o = jnp.transpose(o.reshape(b, dil, sub, ATT_SLOTS, ATT_HEAD_DIM), (0, 2, 1, 3, 4))
        lse = jnp.transpose(lse.reshape(b, dil, sub, ATT_SLOTS), (0, 2, 1, 3))
        outs.append(o.reshape(b, t, ATT_SLOTS, ATT_HEAD_DIM))
        lses.append(lse.reshape(b, t, ATT_SLOTS))
    wts = jax.nn.softmax(jnp.stack(lses, 0), axis=0)
    return jnp.sum(wts[..., None] * jnp.stack(outs, 0), axis=0)


def centred_token_shift(p, mu_prev, mu_next):
    p_prev = jnp.pad(p[:, :-1], ((0, 0), (1, 0), (0, 0)))
    p_next = jnp.pad(p[:, 1:], ((0, 0), (0, 1), (0, 0)))
    return p + mu_prev * (p_prev - p) + mu_next * (p_next - p)


def rwkv7_scan(r, w, k, v, kk, a, reverse):
    b, t, h, n = r.shape

    def step(S, inp):
        r_t, w_t, k_t, v_t, kk_t, a_t = inp
        sa = jnp.einsum('bhij,bhj->bhi', S, -kk_t)
        S = (S * w_t[:, :, None, :] + sa[..., None] * (kk_t * a_t)[:, :, None, :]
             + v_t[..., None] * k_t[:, :, None, :])
        y = jnp.einsum('bhij,bhj->bhi', S, r_t)
        return S, y

    xs = tuple(jnp.moveaxis(z, 1, 0) for z in (r, w, k, v, kk, a))
    S0 = jnp.zeros((b, h, n, n), jnp.float32)
    _, y = lax.scan(step, S0, xs, reverse=reverse)
    return jnp.moveaxis(y, 0, 1)


def rwkv7_mixer(u, w0, w2, a0, a2, g2, k_k, k_a, r_k, lnx_g, lnx_b):
    b, t, _ = u.shape
    u = u.astype(jnp.float32)
    r, k, v, lw, la, lg = jnp.split(u, SHIFT_SPLITS, axis=-1)
    lw = lw.reshape(b, t, N_DIR, DECAY_LORA)
    la = la.reshape(b, t, N_DIR, AAA_LORA)
    heads = lambda z: z.reshape(b, t, RW_HEADS, RW_HEAD_DIM)
    kk = heads(k * k_k)
    kk = kk * lax.rsqrt(jnp.sum(kk * kk, -1, keepdims=True) + 1e-12)
    w_log = -jax.nn.softplus(-(w0 + jnp.einsum('btnr,nrc->btnc', jnp.tanh(lw), w2))) - 0.5
    decay = jnp.exp(-jnp.exp(w_log))
    a = jax.nn.sigmoid(a0 + jnp.einsum('btnr,nrc->btnc', la, a2))
    rh = heads(r)
    vh = heads(v)
    y = jnp.zeros_like(rh)
    bonus = jnp.zeros_like(rh)
    for d in range(N_DIR):
        kd = heads(k * (1.0 + (a[:, :, d] - 1.0) * k_a))
        y = y + rwkv7_scan(rh, heads(decay[:, :, d]), kd, vh, kk, heads(a[:, :, d]), reverse=(d == 1))
        bonus = bonus + jnp.sum(rh * kd * r_k, -1, keepdims=True) * vh
    mu = jnp.mean(y, -1, keepdims=True)
    var = jnp.mean(jnp.square(y - mu), -1, keepdims=True)
    yn = ((y - mu) * lax.rsqrt(var + GN_EPS)).reshape(b, t, RW_WIDTH) * lnx_g + lnx_b
    gate = jnp.einsum('btr,rc->btc', jax.nn.sigmoid(lg), g2)
    return (yn + bonus.reshape(b, t, RW_WIDTH)) * gate


def expert_choice_ffn(h, router, w_gate, w_up, w_down):
    b, t, d = h.shape
    n = b * t
    hf = h.reshape(n, d)
    logits = jnp.einsum('nd,de->ne', hf, router, preferred_element_type=jnp.float32)
    aff = jax.nn.softmax(logits, axis=-1)
    cap = max(1, (EC_CAPACITY_FACTOR * n) // N_EXPERTS)
    gate, idx = lax.top_k(aff.T, cap)
    xs = hf[idx]
    hid = jax.nn.silu(jnp.einsum('ecd,edf->ecf', xs, w_gate)) * jnp.einsum('ecd,edf->ecf', xs, w_up)
    ye = jnp.einsum('ecf,efd->ecd', hid, w_down) * gate[..., None].astype(xs.dtype)
    out = jnp.zeros((n, d), ye.dtype).at[idx.reshape(-1)].add(ye.reshape(-1, d))
    return out.reshape(b, t, d).astype(h.dtype)


def encoder_layer(x, ln1_g, ln2_g, w_in, mu_prev, mu_next, q_norm_g, k_norm_g, w0, w2, a0, a2,
                  g2, k_k, k_a, r_k, lnx_g, lnx_b, w_out, router, w_gate, w_up, w_down):
    b, t, _ = x.shape
    h = rms_norm(x, ln1_g)
    proj = jnp.einsum('btd,dc->btc', h, w_in)
    q, k, v, s = jnp.split(proj, [ATT_WIDTH, 2 * ATT_WIDTH, ATT_COLS], axis=-1)
    grp = lambda z: z.reshape(b, t, N_DIL, ATT_SLOTS, ATT_HEAD_DIM)
    pos = jnp.arange(t)
    q = partial_rope(rms_norm(grp(q), q_norm_g), pos)
    k = partial_rope(rms_norm(grp(k), k_norm_g), pos)
    att = dilated_attention(q, k, grp(v)).reshape(b, t, ATT_OUT).astype(x.dtype)
    u = centred_token_shift(s, mu_prev, mu_next)
    rw = rwkv7_mixer(u, w0, w2, a0, a2, g2, k_k, k_a, r_k, lnx_g, lnx_b).astype(x.dtype)
    x = x + jnp.einsum('btc,cd->btd', jnp.concatenate([att, rw], -1), w_out).astype(x.dtype)
    x = x + expert_choice_ffn(rms_norm(x, ln2_g), router, w_gate, w_up, w_down)
    return x


def setup_inputs(seed: int = 0) -> dict:
    key = jax.random.key(seed)
    ks = jax.random.split(key, 26)
    nrm = lambda i, shape, scale: jax.random.normal(ks[i], shape, jnp.float32) * scale
    L = DEPTH
    return {
        'x_prompt': nrm(0, (BATCH, SEQ, D_MODEL), 1.0),
        'x_sample': nrm(1, (DEC_BATCH, DEC_SEQ, D_MODEL), 1.0),
        'ln1_g': 1.0 + nrm(2, (L, D_MODEL), 0.02),
        'ln2_g': 1.0 + nrm(3, (L, D_MODEL), 0.02),
        'w_in': nrm(4, (L, D_MODEL, IN_COLS), D_MODEL ** -0.5),
        'mu_prev': jax.random.uniform(ks[5], (L, SHIFT_COLS), jnp.float32, 0.0, 0.5),
        'mu_next': jax.random.uniform(ks[6], (L, SHIFT_COLS), jnp.float32, 0.0, 0.5),
        'q_norm_g': 1.0 + nrm(7, (L, ATT_HEAD_DIM), 0.02),
        'k_norm_g': 1.0 + nrm(8, (L, ATT_HEAD_DIM), 0.02),
        'w0': jax.random.uniform(ks[9], (L, N_DIR, RW_WIDTH), jnp.float32, -6.0, 1.0),
        'w2': nrm(10, (L, N_DIR, DECAY_LORA, RW_WIDTH), 0.1 * DECAY_LORA ** -0.5),
        'a0': nrm(11, (L, N_DIR, RW_WIDTH), 0.5),
        'a2': nrm(12, (L, N_DIR, AAA_LORA, RW_WIDTH), 0.5 * AAA_LORA ** -0.5),
        'g2': nrm(13, (L, GATE_LORA, RW_WIDTH), GATE_LORA ** -0.5),
        'k_k': 0.85 + nrm(14, (L, RW_WIDTH), 0.05),
        'k_a': 1.0 + nrm(15, (L, RW_WIDTH), 0.05),
        'r_k': nrm(16, (L, RW_HEADS, RW_HEAD_DIM), 0.1),
        'lnx_g': 1.0 + nrm(17, (L, RW_WIDTH), 0.02),
        'lnx_b': nrm(18, (L, RW_WIDTH), 0.02),
        'w_out': nrm(19, (L, OUT_IN, D_MODEL), OUT_IN ** -0.5),
        'router': nrm(20, (L, D_MODEL, N_EXPERTS), D_MODEL ** -0.5),
        'w_gate': nrm(21, (L, N_EXPERTS, D_MODEL, EXPERT_FF), D_MODEL ** -0.5),
        'w_up': nrm(22, (L, N_EXPERTS, D_MODEL, EXPERT_FF), D_MODEL ** -0.5),
        'w_down': nrm(23, (L, N_EXPERTS, EXPERT_FF, D_MODEL), EXPERT_FF ** -0.5),
    }


def reference(x_prompt, x_sample, ln1_g, ln2_g, w_in, mu_prev, mu_next, q_norm_g, k_norm_g,
              w0, w2, a0, a2, g2, k_k, k_a, r_k, lnx_g, lnx_b, w_out, router, w_gate, w_up, w_down):
    y_prompt = x_prompt
    y_sample = x_sample
    for l in range(DEPTH):
        lp = (ln1_g[l], ln2_g[l], w_in[l], mu_prev[l], mu_next[l], q_norm_g[l], k_norm_g[l],
              w0[l], w2[l], a0[l], a2[l], g2[l], k_k[l], k_a[l], r_k[l], lnx_g[l], lnx_b[l],
              w_out[l], router[l], w_gate[l], w_up[l], w_down[l])
        y_prompt = encoder_layer(y_prompt, *lp)
        y_sample = encoder_layer(y_sample, *lp)
    return (y_prompt, y_sample)
```

```python
import functools

import jax
import jax.numpy as jnp
import numpy as np
from jax import lax
from jax.experimental import pallas as pl
from jax.experimental.pallas import tpu as pltpu

D_MODEL = 1024
RW_HEADS = 4
HEAD_DIM = 64
RW_WIDTH = RW_HEADS * HEAD_DIM
DECAY_LORA = 64
AAA_LORA = 64
GATE_LORA = 128
N_DIR = 2
GN_EPS = 64e-5
ATT_SLOTS = 4
DILATIONS = (1, 4, 16)
BAND_HALF = 64
N_DIL = len(DILATIONS)
GROUP_WIDTH = ATT_SLOTS * HEAD_DIM
ATT_WIDTH = N_DIL * GROUP_WIDTH
ATT_COLS = 3 * ATT_WIDTH
SHIFT_COLS = 3 * RW_WIDTH + N_DIR * DECAY_LORA + N_DIR * AAA_LORA + GATE_LORA
IN_COLS = ATT_COLS + SHIFT_COLS
ROPE_THETA = 500000.0
ROPE_DIMS = HEAD_DIM // 4
ROPE_HALF = ROPE_DIMS // 2
N_EXPERTS = 16
EC_CAPACITY_FACTOR = 2
EXPERT_FF = 2048
NORM_EPS = 1e-6
NEG_INF = -1e30
LANES = 128

ROW_TILE = 512
ATT_Q_BLOCK = 128
SCAN_CHUNK = 128
FFN_ROW_TILE = 512
VMEM_LIMIT = 56 * 1024 * 1024

HIGHEST = lax.Precision.HIGHEST


def _cparams(sem):
    return pltpu.CompilerParams(dimension_semantics=sem, vmem_limit_bytes=VMEM_LIMIT)


def _head_sum_matrix():
    h = np.arange(GROUP_WIDTH) // HEAD_DIM
    return jnp.asarray((h[:, None] == h[None, :]).astype(np.float32))


def _head_sum(x, bd):
    return jnp.dot(x, bd, precision=HIGHEST, preferred_element_type=jnp.float32)


def _rope_tables(t):
    inv = jnp.power(jnp.float32(ROPE_THETA), -jnp.arange(ROPE_HALF, dtype=jnp.float32) * 2.0 / ROPE_DIMS)
    ang = jnp.arange(t, dtype=jnp.float32)[:, None] * inv[None, :]
    cos, sin = jnp.cos(ang), jnp.sin(ang)
    zeros = jnp.zeros((t, HEAD_DIM - ROPE_DIMS), jnp.float32)
    zero8 = jnp.zeros((t, ROPE_HALF), jnp.float32)
    c = jnp.concatenate([cos, cos, zeros + 1.0], -1)
    s1 = jnp.concatenate([zero8, sin, zeros], -1)
    s2 = jnp.concatenate([-sin, zero8, zeros], -1)
    tile = lambda z: jnp.tile(z, (1, ATT_SLOTS))
    return tile(c), tile(s1), tile(s2)


def _in_proj_kernel(x_ref, g_ref, w_ref, qg_ref, kg_ref, c_ref, s1_ref, s2_ref, bd_ref,
                    q0_ref, q1_ref, q2_ref, k0_ref, k1_ref, k2_ref, v0_ref, v1_ref, v2_ref, s_ref,
                    tmp_ref):
    x = x_ref[...]
    h = x * lax.rsqrt(jnp.mean(x * x, -1, keepdims=True) + NORM_EPS) * g_ref[...]
    hb = h.astype(jnp.bfloat16)
    bd = bd_ref[...]
    c, s1, s2 = c_ref[...], s1_ref[...], s2_ref[...]
    rows = x.shape[0]

    def deinterleave(val, out_ref, dil):
        if dil == 1:
            out_ref[0] = val.astype(out_ref.dtype)
            return
        for hf in range(GROUP_WIDTH // LANES):
            tmp_ref[hf] = val[:, hf * LANES:(hf + 1) * LANES]
        for r in range(dil):
            for hf in range(GROUP_WIDTH // LANES):
                out_ref[r, :, hf * LANES:(hf + 1) * LANES] = (
                    tmp_ref[hf, pl.ds(r, rows // dil, stride=dil), :].astype(out_ref.dtype))

    def normed_rope(col0, gain, scale):
        p = jnp.dot(hb, w_ref[:, col0:col0 + GROUP_WIDTH], preferred_element_type=jnp.float32)
        ms = _head_sum(p * p, bd) * (1.0 / HEAD_DIM)
        n = p * lax.rsqrt(ms + NORM_EPS) * gain
        n = n * c + pltpu.roll(n, ROPE_HALF, 1) * s1 + pltpu.roll(n, GROUP_WIDTH - ROPE_HALF, 1) * s2
        return n * scale if scale != 1.0 else n

    q_refs = (q0_ref, q1_ref, q2_ref)
    k_refs = (k0_ref, k1_ref, k2_ref)
    v_refs = (v0_ref, v1_ref, v2_ref)
    for g, dil in enumerate(DILATIONS):
        deinterleave(normed_rope(g * GROUP_WIDTH, qg_ref[...], HEAD_DIM ** -0.5), q_refs[g], dil)
        deinterleave(normed_rope(ATT_WIDTH + g * GROUP_WIDTH, kg_ref[...], 1.0), k_refs[g], dil)
        col0 = 2 * ATT_WIDTH + g * GROUP_WIDTH
        v = jnp.dot(hb, w_ref[:, col0:col0 + GROUP_WIDTH], preferred_element_type=jnp.float32)
        deinterleave(v, v_refs[g], dil)
    s_ref[...] = jnp.dot(hb, w_ref[:, ATT_COLS:], preferred_element_type=jnp.float32)


def _in_proj(x, ln1_g, w_in_bf16, q_gain, k_gain, rope, bd):
    b, t, _ = x.shape
    nt = t // ROW_TILE
    row2 = lambda bi, i: (0, 0)
    grp_shapes, grp_specs = [], []
    for _ in range(3):
        for dil in DILATIONS:
            grp_shapes.append(jax.ShapeDtypeStruct((b, dil, t // dil, GROUP_WIDTH), jnp.bfloat16))
            grp_specs.append(pl.BlockSpec((None, dil, ROW_TILE // dil, GROUP_WIDTH), lambda bi, i: (bi, 0, i, 0)))
    tab_spec = pl.BlockSpec((ROW_TILE, GROUP_WIDTH), lambda bi, i: (i, 0))
    outs = pl.pallas_call(
        _in_proj_kernel,
        grid=(b, nt),
        in_specs=[
            pl.BlockSpec((None, ROW_TILE, D_MODEL), lambda bi, i: (bi, i, 0)),
            pl.BlockSpec((1, D_MODEL), row2),
            pl.BlockSpec((D_MODEL, IN_COLS), row2),
            pl.BlockSpec((1, GROUP_WIDTH), row2),
            pl.BlockSpec((1, GROUP_WIDTH), row2),
            tab_spec, tab_spec, tab_spec,
            pl.BlockSpec((GROUP_WIDTH, GROUP_WIDTH), row2),
        ],
        out_specs=grp_specs + [pl.BlockSpec((None, ROW_TILE, SHIFT_COLS), lambda bi, i: (bi, i, 0))],
        out_shape=grp_shapes + [jax.ShapeDtypeStruct((b, t, SHIFT_COLS), jnp.float32)],
        scratch_shapes=[pltpu.VMEM((GROUP_WIDTH // LANES, ROW_TILE, LANES), jnp.float32)],
        compiler_params=_cparams(("parallel", "parallel")),
        name="in_proj",
    )(x, ln1_g.reshape(1, D_MODEL), w_in_bf16, q_gain, k_gain, *rope, bd)
    q = outs[0:3]
    k = outs[3:6]
    v = outs[6:9]
    return q, k, v, outs[9]


def _band_attn_kernel(q_ref, kp_ref, kc_ref, kn_ref, vp_ref, vc_ref, vn_ref, o_ref, lse_ref, *, seq_len, q_chunk):
    i = pl.program_id(1)
    kext = jnp.concatenate([kp_ref[...], kc_ref[...], kn_ref[...]], axis=0)
    vext = jnp.concatenate([vp_ref[...], vc_ref[...], vn_ref[...]], axis=0)
    kw = ATT_Q_BLOCK + 2 * BAND_HALF
    lane_head = lax.broadcasted_iota(jnp.int32, (ATT_Q_BLOCK, GROUP_WIDTH), 1) // HEAD_DIM
    qi = lax.broadcasted_iota(jnp.int32, (ATT_Q_BLOCK, kw), 0)
    kc = lax.broadcasted_iota(jnp.int32, (ATT_Q_BLOCK, kw), 1)
    diff = kc - BAND_HALF - qi
    band = (diff <= BAND_HALF) & (diff >= -BAND_HALF)
    for a in range(0, q_chunk, ATT_Q_BLOCK):
        kpos = i * q_chunk + (a - BAND_HALF) + kc
        valid = band & (kpos >= 0) & (kpos < seq_len)
        qb = q_ref[a:a + ATT_Q_BLOCK, :]
        kb = kext[a:a + kw]
        vb = vext[a:a + kw]
        acc = jnp.zeros((ATT_Q_BLOCK, GROUP_WIDTH), jnp.float32)
        lse = jnp.zeros((ATT_Q_BLOCK, GROUP_WIDTH), jnp.float32)
        for h in range(ATT_SLOTS):
            mine = lane_head == h
            qm = jnp.where(mine, qb, jnp.zeros_like(qb))
            s = lax.dot_general(qm, kb, (((1,), (1,)), ((), ())), preferred_element_type=jnp.float32)
            s = jnp.where(valid, s, NEG_INF)
            m = jnp.max(s, -1, keepdims=True)
            p = jnp.exp(s - m)
            den = jnp.sum(p, -1, keepdims=True)
            pv = jnp.dot(p.astype(jnp.bfloat16), vb, preferred_element_type=jnp.float32)
            acc = jnp.where(mine, pv / den, acc)
            lse = jnp.where(mine, m + jnp.log(den), lse)
        o_ref[a:a + ATT_Q_BLOCK, :] = acc
        lse_ref[a:a + ATT_Q_BLOCK, :] = lse


def _band_attn(q, k, v):
    b, dil, seq_len, _ = q.shape
    ns = b * dil
    q, k, v = (z.reshape(ns, seq_len, GROUP_WIDTH) for z in (q, k, v))
    q_chunk = min(seq_len, 512)
    halo_per_chunk = q_chunk // BAND_HALF
    last_halo = seq_len // BAND_HALF - 1
    cur = pl.BlockSpec((None, q_chunk, GROUP_WIDTH), lambda s, i: (s, i, 0))
    prev = pl.BlockSpec((None, BAND_HALF, GROUP_WIDTH), lambda s, i: (s, jnp.maximum(i * halo_per_chunk - 1, 0), 0))
    nxt = pl.BlockSpec((None, BAND_HALF, GROUP_WIDTH),
                       lambda s, i: (s, jnp.minimum((i + 1) * halo_per_chunk, last_halo), 0))
    o, lse = pl.pallas_call(
        functools.partial(_band_attn_kernel, seq_len=seq_len, q_chunk=q_chunk),
        grid=(ns, seq_len // q_chunk),
        in_specs=[cur, prev, cur, nxt, prev, cur, nxt],
        out_specs=[cur, cur],
        out_shape=[jax.ShapeDtypeStruct((ns, seq_len, GROUP_WIDTH), jnp.float32)] * 2,
        compiler_params=_cparams(("parallel", "parallel")),
        name="band_attn",
    )(q, k, k, k, v, v, v)
    shape4 = (b, dil, seq_len, GROUP_WIDTH)
    return o.reshape(shape4), lse.reshape(shape4)


def _softplus(x):
    return jnp.maximum(x, 0.0) + jnp.log1p(jnp.exp(-jnp.abs(x)))


def _sigmoid(x):
    return 1.0 / (1.0 + jnp.exp(-x))


def _rwkv_prep_kernel(sp_ref, sc_ref, sn_ref, mup_ref, mun_ref, w0_ref, w2_ref, a0_ref, a2_ref, g2_ref,
                      kk_ref, ka_ref, rk_ref, bd_ref,
                      nkkT_ref, rT_ref, wT_ref, bT_ref, kdT_ref, v_ref, bonus_ref, gate_ref):
    i = pl.program_id(1)
    last = pl.num_programs(1) - 1
    p = sc_ref[...]
    rows = p.shape[0]
    row = lax.broadcasted_iota(jnp.int32, p.shape, 0)
    halo_prev = sp_ref[7:8, :] * (i > 0).astype(jnp.float32)
    halo_next = sn_ref[0:1, :] * (i < last).astype(jnp.float32)
    p_prev = jnp.where(row == 0, halo_prev, pltpu.roll(p, 1, 0))
    p_next = jnp.where(row == rows - 1, halo_next, pltpu.roll(p, rows - 1, 0))
    u = p + mup_ref[...] * (p_prev - p) + mun_ref[...] * (p_next - p)

    bd = bd_ref[...]
    r = u[:, 0:RW_WIDTH]
    k = u[:, RW_WIDTH:2 * RW_WIDTH]
    v = u[:, 2 * RW_WIDTH:3 * RW_WIDTH]
    c0 = 3 * RW_WIDTH
    lw = jnp.tanh(u[:, c0:c0 + 128]).astype(jnp.bfloat16)
    la = u[:, c0 + 128:c0 + 256].astype(jnp.bfloat16)
    lg = _sigmoid(u[:, c0 + 256:c0 + 384]).astype(jnp.bfloat16)

    kk = k * kk_ref[...]
    kk = kk * lax.rsqrt(_head_sum(kk * kk, bd) + 1e-12)
    nkkT_ref[...] = (-kk).T
    rT_ref[...] = r.T
    v_ref[...] = v
    gate_ref[...] = jnp.dot(lg, g2_ref[...], preferred_element_type=jnp.float32)
    bonus = jnp.zeros_like(v)
    for d in range(N_DIR):
        z = w0_ref[d:d + 1, :] + jnp.dot(lw, w2_ref[d], preferred_element_type=jnp.float32)
        w_log = -_softplus(-z) - 0.5
        wT_ref[d] = jnp.exp(-jnp.exp(w_log)).T
        a = _sigmoid(a0_ref[d:d + 1, :] + jnp.dot(la, a2_ref[d], preferred_element_type=jnp.float32))
        kd = k * (1.0 + (a - 1.0) * ka_ref[...])
        kdT_ref[d] = kd.T
        bT_ref[d] = (kk * a).T
        bonus = bonus + _head_sum(r * kd * rk_ref[...], bd) * v
    bonus_ref[...] = bonus


def _rwkv_prep(s, mu_prev, mu_next, w0, w2p, a0, a2p, g2, k_k, k_a, r_k, bd):
    b, t, _ = s.shape
    nt = t // ROW_TILE
    halo_per_tile = ROW_TILE // 8
    last_halo = t // 8 - 1
    row2 = lambda bi, i: (0, 0)
    row3 = lambda bi, i: (0, 0, 0)
    vec = lambda n: pl.BlockSpec((1, n), row2)
    tspec = pl.BlockSpec((None, RW_WIDTH, ROW_TILE), lambda bi, i: (bi, 0, i))
    tspec2 = pl.BlockSpec((N_DIR, None, RW_WIDTH, ROW_TILE), lambda bi, i: (0, bi, 0, i))
    nspec = pl.BlockSpec((None, ROW_TILE, RW_WIDTH), lambda bi, i: (bi, i, 0))
    tshape = jax.ShapeDtypeStruct((b, RW_WIDTH, t), jnp.float32)
    tshape2 = jax.ShapeDtypeStruct((N_DIR, b, RW_WIDTH, t), jnp.float32)
    nshape = jax.ShapeDtypeStruct((b, t, RW_WIDTH), jnp.float32)
    return pl.pallas_call(
        _rwkv_prep_kernel,
        grid=(b, nt),
        in_specs=[
            pl.BlockSpec((None, 8, SHIFT_COLS), lambda bi, i: (bi, jnp.maximum(i * halo_per_tile - 1, 0), 0)),
            pl.BlockSpec((None, ROW_TILE, SHIFT_COLS), lambda bi, i: (bi, i, 0)),
            pl.BlockSpec((None, 8, SHIFT_COLS), lambda bi, i: (bi, jnp.minimum((i + 1) * halo_per_tile, last_halo), 0)),
            vec(SHIFT_COLS), vec(SHIFT_COLS),
            pl.BlockSpec((N_DIR, RW_WIDTH), row2),
            pl.BlockSpec((N_DIR, 128, RW_WIDTH), row3),
            pl.BlockSpec((N_DIR, RW_WIDTH), row2),
            pl.BlockSpec((N_DIR, 128, RW_WIDTH), row3),
            pl.BlockSpec((GATE_LORA, RW_WIDTH), row2),
            vec(RW_WIDTH), vec(RW_WIDTH), vec(RW_WIDTH),
            pl.BlockSpec((GROUP_WIDTH, GROUP_WIDTH), row2),
        ],
        out_specs=[tspec, tspec, tspec2, tspec2, tspec2, nspec, nspec, nspec],
        out_shape=[tshape, tshape, tshape2, tshape2, tshape2, nshape, nshape, nshape],
        compiler_params=_cparams(("parallel", "parallel")),
        name="rwkv_prep",
    )(s, s, s, mu_prev, mu_next, w0, w2p, a0, a2p, g2, k_k, k_a, r_k, bd)


SCAN_UNROLL = 8


def _rwkv_scan_kernel(nkkT_ref, rT_ref, wT_ref, bT_ref, kdT_ref, v_ref, y_ref, state_ref):
    d = pl.program_id(0)
    c = pl.program_id(2)

    @pl.when(c == 0)
    def _():
        state_ref[...] = jnp.zeros_like(state_ref)

    col_refs = (nkkT_ref, wT_ref, bT_ref, kdT_ref, rT_ref)

    def run(reverse):
        def block(blk, states):
            base = (SCAN_CHUNK - SCAN_UNROLL * (blk + 1)) if reverse else SCAN_UNROLL * blk
            shift = (SCAN_CHUNK - base) % SCAN_CHUNK
            tiles = [pltpu.roll(ref[...], shift, 1) for ref in col_refs]
            rows = pl.ds(pl.multiple_of(base, SCAN_UNROLL), SCAN_UNROLL)
            v_rows = v_ref[rows, :]
            states = list(states)
            y_rows = [None] * SCAN_UNROLL
            order = range(SCAN_UNROLL - 1, -1, -1) if reverse else range(SCAN_UNROLL)
            for u in order:
                y_heads = []
                for h in range(RW_HEADS):
                    hs = slice(h * HEAD_DIM, (h + 1) * HEAD_DIM)
                    col = lambda z: jnp.broadcast_to(z[hs, u:u + 1], (HEAD_DIM, HEAD_DIM))
                    nkk, w, bb, kd, r = (col(z) for z in tiles)
                    st = states[h]
                    sa = jnp.sum(st * nkk, axis=0, keepdims=True)
                    st = st * w + bb * sa + kd * v_rows[u:u + 1, hs]
                    y_heads.append(jnp.sum(st * r, axis=0, keepdims=True))
                    states[h] = st
                y_rows[u] = jnp.concatenate(y_heads, axis=1)
            y_ref[rows, :] = jnp.concatenate(y_rows, axis=0)
            return tuple(states)

        init = tuple(state_ref[h] for h in range(RW_HEADS))
        final = lax.fori_loop(0, SCAN_CHUNK // SCAN_UNROLL, block, init)
        for h in range(RW_HEADS):
            state_ref[h] = final[h]

    @pl.when(d == 0)
    def _():
        run(False)

    @pl.when(d == 1)
    def _():
        run(True)


def _rwkv_scan(nkkT, rT, wT, bT, kdT, v):
    b, t, _ = v.shape
    nc = t // SCAN_CHUNK
    chunk = lambda d, c: c + d * (nc - 1 - 2 * c)
    shared = pl.BlockSpec((None, RW_WIDTH, SCAN_CHUNK), lambda d, bi, c: (bi, 0, chunk(d, c)))
    per_dir = pl.BlockSpec((None, None, RW_WIDTH, SCAN_CHUNK), lambda d, bi, c: (d, bi, 0, chunk(d, c)))
    return pl.pallas_call(
        _rwkv_scan_kernel,
        grid=(N_DIR, b, nc),
        in_specs=[shared, shared, per_dir, per_dir, per_dir,
                  pl.BlockSpec((None, SCAN_CHUNK, RW_WIDTH), lambda d, bi, c: (bi, chunk(d, c), 0))],
        out_specs=pl.BlockSpec((None, None, SCAN_CHUNK, RW_WIDTH), lambda d, bi, c: (d, bi, chunk(d, c), 0)),
        out_shape=jax.ShapeDtypeStruct((N_DIR, b, t, RW_WIDTH), jnp.float32),
        scratch_shapes=[pltpu.VMEM((RW_HEADS, HEAD_DIM, HEAD_DIM), jnp.float32)],
        compiler_params=_cparams(("parallel", "parallel", "arbitrary")),
        name="rwkv_scan",
    )(nkkT, rT, wT, bT, kdT, v)


def _out_proj_kernel(x_ref, o0_ref, l0_ref, o1_ref, l1_ref, o2_ref, l2_ref, y_ref, bonus_ref, gate_ref,
                     lnxg_ref, lnxb_ref, wout_ref, ln2_ref, router_ref, bd_ref,
                     xn_ref, h2_ref, aff_ref, so1_ref, sl1_ref, so2_ref, sl2_ref):
    rows = x_ref.shape[0]

    def interleave(src_ref, dst_ref, dil):
        halves = range(GROUP_WIDTH // LANES)
        for r in range(dil):
            for hf in halves:
                dst_ref[hf, pl.ds(r, rows // dil, stride=dil), :] = src_ref[r, :, hf * LANES:(hf + 1) * LANES]
        return jnp.concatenate([dst_ref[hf] for hf in halves], axis=-1)

    o1 = interleave(o1_ref, so1_ref, DILATIONS[1])
    l1 = interleave(l1_ref, sl1_ref, DILATIONS[1])
    o2 = interleave(o2_ref, so2_ref, DILATIONS[2])
    l2 = interleave(l2_ref, sl2_ref, DILATIONS[2])
    l0 = l0_ref[0]
    m = jnp.maximum(jnp.maximum(l0, l1), l2)
    e0, e1, e2 = jnp.exp(l0 - m), jnp.exp(l1 - m), jnp.exp(l2 - m)
    att = (e0 * o0_ref[0] + e1 * o1 + e2 * o2) / (e0 + e1 + e2)

    bd = bd_ref[...]
    y = y_ref[0] + y_ref[1]
    mu = _head_sum(y, bd) * (1.0 / HEAD_DIM)
    dlt = y - mu
    var = _head_sum(dlt * dlt, bd) * (1.0 / HEAD_DIM)
    yn = dlt * lax.rsqrt(var + GN_EPS) * lnxg_ref[...] + lnxb_ref[...]
    rw = (yn + bonus_ref[...]) * gate_ref[...]

    mixed = (jnp.dot(att.astype(jnp.bfloat16), wout_ref[0:GROUP_WIDTH, :], preferred_element_type=jnp.float32)
             + jnp.dot(rw.astype(jnp.bfloat16), wout_ref[GROUP_WIDTH:, :], preferred_element_type=jnp.float32))
    xn = x_ref[...] + mixed
    xn_ref[...] = xn
    h2 = xn * lax.rsqrt(jnp.mean(xn * xn, -1, keepdims=True) + NORM_EPS) * ln2_ref[...]
    h2_ref[...] = h2.astype(h2_ref.dtype)
    logits = lax.dot_general(router_ref[...], h2, (((1,), (1,)), ((), ())),
                             precision=HIGHEST, preferred_element_type=jnp.float32)
    ex = jnp.exp(logits - jnp.max(logits, 0, keepdims=True))
    aff_ref[...] = ex / jnp.sum(ex, 0, keepdims=True)


def _out_proj(x, att_parts, y, bonus, gate, lnx_g, lnx_b, w_out_bf16, ln2_g, router_t, bd):
    b, t, _ = x.shape
    nt = t // ROW_TILE
    row2 = lambda bi, i: (0, 0)
    tok = lambda w: pl.BlockSpec((None, ROW_TILE, w), lambda bi, i: (bi, i, 0))
    grp = lambda dil: pl.BlockSpec((None, dil, ROW_TILE // dil, GROUP_WIDTH), lambda bi, i: (bi, 0, i, 0))
    vec = lambda n: pl.BlockSpec((1, n), row2)
    grp_specs, grp_args = [], []
    for (o, lse), dil in zip(att_parts, DILATIONS):
        grp_specs += [grp(dil), grp(dil)]
        grp_args += [o, lse]
    return pl.pallas_call(
        _out_proj_kernel,
        grid=(b, nt),
        in_specs=[tok(D_MODEL)] + grp_specs + [
            pl.BlockSpec((N_DIR, None, ROW_TILE, RW_WIDTH), lambda bi, i: (0, bi, i, 0)),
            tok(RW_WIDTH), tok(RW_WIDTH), vec(RW_WIDTH), vec(RW_WIDTH),
            pl.BlockSpec((GROUP_WIDTH + RW_WIDTH, D_MODEL), row2),
            vec(D_MODEL),
            pl.BlockSpec((N_EXPERTS, D_MODEL), row2),
            pl.BlockSpec((GROUP_WIDTH, GROUP_WIDTH), row2),
        ],
        out_specs=[tok(D_MODEL), tok(D_MODEL),
                   pl.BlockSpec((N_EXPERTS, ROW_TILE), lambda bi, i: (0, bi * nt + i))],
        out_shape=[jax.ShapeDtypeStruct((b, t, D_MODEL), jnp.float32),
                   jax.ShapeDtypeStruct((b, t, D_MODEL), jnp.bfloat16),
                   jax.ShapeDtypeStruct((N_EXPERTS, b * t), jnp.float32)],
        scratch_shapes=[pltpu.VMEM((GROUP_WIDTH // LANES, ROW_TILE, LANES), jnp.float32)] * 4,
        compiler_params=_cparams(("parallel", "parallel")),
        name="out_proj",
    )(x, *grp_args, y, bonus, gate, lnx_g, lnx_b, w_out_bf16, ln2_g, router_t, bd)


FF_CHUNK = 512


def _expert_ffn_kernel(x_ref, gate_ref, wg_ref, wu_ref, wd_ref, o_ref):
    x = x_ref[...]
    acc = jnp.zeros(o_ref.shape, jnp.float32)
    for f0 in range(0, EXPERT_FF, FF_CHUNK):
        hg = jnp.dot(x, wg_ref[:, f0:f0 + FF_CHUNK], preferred_element_type=jnp.float32)
        hu = jnp.dot(x, wu_ref[:, f0:f0 + FF_CHUNK], preferred_element_type=jnp.float32)
        hid = (hg * _sigmoid(hg) * hu).astype(jnp.bfloat16)
        acc = acc + jnp.dot(hid, wd_ref[f0:f0 + FF_CHUNK, :], preferred_element_type=jnp.float32)
    o_ref[...] = acc * gate_ref[...]


def _expert_ffn(xs, gate, wg, wu, wd):
    e, cap, _ = xs.shape
    tile = min(FFN_ROW_TILE, cap)
    tok = lambda w: pl.BlockSpec((None, tile, w), lambda ei, i: (ei, i, 0))
    wspec = lambda r, c: pl.BlockSpec((None, r, c), lambda ei, i: (ei, 0, 0))
    return pl.pallas_call(
        _expert_ffn_kernel,
        grid=(e, cap // tile),
        in_specs=[tok(D_MODEL), tok(1), wspec(D_MODEL, EXPERT_FF), wspec(D_MODEL, EXPERT_FF), wspec(EXPERT_FF, D_MODEL)],
        out_specs=tok(D_MODEL),
        out_shape=jax.ShapeDtypeStruct((e, cap, D_MODEL), jnp.float32),
        compiler_params=_cparams(("parallel", "arbitrary")),
        name="expert_ffn",
    )(xs, gate, wg, wu, wd)


def _pad_lora(w):
    z = jnp.zeros((N_DIR, N_DIR * w.shape[1], w.shape[2]), w.dtype)
    for d in range(N_DIR):
        z = z.at[d, d * w.shape[1]:(d + 1) * w.shape[1]].set(w[d])
    return z.astype(jnp.bfloat16)


def _layer_params(l, ln1_g, ln2_g, w_in, mu_prev, mu_next, q_norm_g, k_norm_g, w0, w2, a0, a2, g2, k_k, k_a,
                  r_k, lnx_g, lnx_b, w_out, router, w_gate, w_up, w_down):
    bf = lambda z: z.astype(jnp.bfloat16)
    row = lambda z: z.reshape(1, -1)
    return dict(
        ln1_g=ln1_g[l], ln2_g=row(ln2_g[l]), w_in=bf(w_in[l]), mu_prev=row(mu_prev[l]), mu_next=row(mu_next[l]),
        q_gain=row(jnp.tile(q_norm_g[l], ATT_SLOTS)), k_gain=row(jnp.tile(k_norm_g[l], ATT_SLOTS)),
        w0=w0[l], w2=_pad_lora(w2[l]), a0=a0[l], a2=_pad_lora(a2[l]), g2=bf(g2[l]),
        k_k=row(k_k[l]), k_a=row(k_a[l]), r_k=row(r_k[l]), lnx_g=row(lnx_g[l]), lnx_b=row(lnx_b[l]),
        w_out=bf(w_out[l]), router_t=router[l].T, w_gate=bf(w_gate[l]), w_up=bf(w_up[l]), w_down=bf(w_down[l]),
    )


def _encoder_layer(x, p, rope, bd):
    b, t, _ = x.shape
    n = b * t
    q, k, v, s = _in_proj(x, p['ln1_g'], p['w_in'], p['q_gain'], p['k_gain'], rope, bd)
    att_parts = [_band_attn(q[g], k[g], v[g]) for g in range(N_DIL)]
    nkkT, rT, wT, bT, kdT, vv, bonus, gate = _rwkv_prep(
        s, p['mu_prev'], p['mu_next'], p['w0'], p['w2'], p['a0'], p['a2'], p['g2'], p['k_k'], p['k_a'], p['r_k'], bd)
    y = _rwkv_scan(nkkT, rT, wT, bT, kdT, vv)
    xn, h2, aff_t = _out_proj(x, att_parts, y, bonus, gate, p['lnx_g'], p['lnx_b'], p['w_out'], p['ln2_g'],
                              p['router_t'], bd)
    cap = max(1, (EC_CAPACITY_FACTOR * n) // N_EXPERTS)
    gate_e, idx = lax.top_k(aff_t, cap)
    xs = h2.reshape(n, D_MODEL)[idx]
    ye = _expert_ffn(xs, gate_e[..., None], p['w_gate'], p['w_up'], p['w_down'])
    moe = jnp.zeros((n, D_MODEL), jnp.float32).at[idx.reshape(-1)].add(ye.reshape(-1, D_MODEL))
    return xn + moe.reshape(b, t, D_MODEL)


def kernel(x_prompt, x_sample, ln1_g, ln2_g, w_in, mu_prev, mu_next, q_norm_g, k_norm_g, w0, w2, a0, a2, g2,
           k_k, k_a, r_k, lnx_g, lnx_b, w_out, router, w_gate, w_up, w_down):
    depth = w_in.shape[0]
    bd = _head_sum_matrix()
    rope_p = _rope_tables(x_prompt.shape[1])
    rope_s = _rope_tables(x_sample.shape[1])
    y_prompt, y_sample = x_prompt, x_sample
    for l in range(depth):
        p = _layer_params(l, ln1_g, ln2_g, w_in, mu_prev, mu_next, q_norm_g, k_norm_g, w0, w2, a0, a2, g2, k_k,
                          k_a, r_k, lnx_g, lnx_b, w_out, router, w_gate, w_up, w_down)
        y_prompt = _encoder_layer(y_prompt, p, rope_p, bd)
        y_sample = _encoder_layer(y_sample, p, rope_s, bd)
    return (y_prompt, y_sample)
```

```python
import functools

import jax
import jax.numpy as jnp
import numpy as np
from jax import lax
from jax.experimental import pallas as pl
from jax.experimental.pallas import tpu as pltpu

D_MODEL = 1024
RW_HEADS = 4
HEAD_DIM = 64
RW_WIDTH = RW_HEADS * HEAD_DIM
DECAY_LORA = 64
AAA_LORA = 64
GATE_LORA = 128
N_DIR = 2
GN_EPS = 64e-5
ATT_SLOTS = 4
DILATIONS = (1, 4, 16)
BAND_HALF = 64
N_DIL = len(DILATIONS)
GROUP_WIDTH = ATT_SLOTS * HEAD_DIM
ATT_WIDTH = N_DIL * GROUP_WIDTH
ATT_COLS = 3 * ATT_WIDTH
SHIFT_COLS = 3 * RW_WIDTH + N_DIR * DECAY_LORA + N_DIR * AAA_LORA + GATE_LORA
IN_COLS = ATT_COLS + SHIFT_COLS
ROPE_THETA = 500000.0
ROPE_DIMS = HEAD_DIM // 4
ROPE_HALF = ROPE_DIMS // 2
N_EXPERTS = 16
EC_CAPACITY_FACTOR = 2
EXPERT_FF = 2048
NORM_EPS = 1e-6
NEG_INF = -1e30
LANES = 128

ROW_TILE = 512
ATT_Q_BLOCK = 128
CHUNK = 64
FFN_ROW_TILE = 512
VMEM_LIMIT = 56 * 1024 * 1024

HIGHEST = lax.Precision.HIGHEST


def _cparams(sem):
    return pltpu.CompilerParams(dimension_semantics=sem, vmem_limit_bytes=VMEM_LIMIT)


def _head_sum_matrix():
    h = np.arange(GROUP_WIDTH) // HEAD_DIM
    return jnp.asarray((h[:, None] == h[None, :]).astype(np.float32))


def _head_sum(x, bd):
    return jnp.dot(x, bd, precision=HIGHEST, preferred_element_type=jnp.float32)


def _rope_tables(t):
    inv = jnp.power(jnp.float32(ROPE_THETA), -jnp.arange(ROPE_HALF, dtype=jnp.float32) * 2.0 / ROPE_DIMS)
    ang = jnp.arange(t, dtype=jnp.float32)[:, None] * inv[None, :]
    cos, sin = jnp.cos(ang), jnp.sin(ang)
    zeros = jnp.zeros((t, HEAD_DIM - ROPE_DIMS), jnp.float32)
    zero8 = jnp.zeros((t, ROPE_HALF), jnp.float32)
    c = jnp.concatenate([cos, cos, zeros + 1.0], -1)
    s1 = jnp.concatenate([zero8, sin, zeros], -1)
    s2 = jnp.concatenate([-sin, zero8, zeros], -1)
    tile = lambda z: jnp.tile(z, (1, ATT_SLOTS))
    return tile(c), tile(s1), tile(s2)


def _in_proj_kernel(x_ref, g_ref, w_ref, qg_ref, kg_ref, c_ref, s1_ref, s2_ref, bd_ref,
                    q0_ref, q1_ref, q2_ref, k0_ref, k1_ref, k2_ref, v0_ref, v1_ref, v2_ref, s_ref,
                    tmp_ref):
    x = x_ref[...]
    h = x * lax.rsqrt(jnp.mean(x * x, -1, keepdims=True) + NORM_EPS) * g_ref[...]
    hb = h.astype(jnp.bfloat16)
    bd = bd_ref[...]
    c, s1, s2 = c_ref[...], s1_ref[...], s2_ref[...]
    rows = x.shape[0]

    def deinterleave(val, out_ref, dil):
        if dil == 1:
            out_ref[0] = val.astype(out_ref.dtype)
            return
        for hf in range(GROUP_WIDTH // LANES):
            tmp_ref[hf] = val[:, hf * LANES:(hf + 1) * LANES]
        for r in range(dil):
            for hf in range(GROUP_WIDTH // LANES):
                out_ref[r, :, hf * LANES:(hf + 1) * LANES] = (
                    tmp_ref[hf, pl.ds(r, rows // dil, stride=dil), :].astype(out_ref.dtype))

    def normed_rope(col0, gain, scale):
        p = jnp.dot(hb, w_ref[:, col0:col0 + GROUP_WIDTH], preferred_element_type=jnp.float32)
        ms = _head_sum(p * p, bd) * (1.0 / HEAD_DIM)
        n = p * lax.rsqrt(ms + NORM_EPS) * gain
        n = n * c + pltpu.roll(n, ROPE_HALF, 1) * s1 + pltpu.roll(n, GROUP_WIDTH - ROPE_HALF, 1) * s2
        return n * scale if scale != 1.0 else n

    q_refs = (q0_ref, q1_ref, q2_ref)
    k_refs = (k0_ref, k1_ref, k2_ref)
    v_refs = (v0_ref, v1_ref, v2_ref)
    for g, dil in enumerate(DILATIONS):
        deinterleave(normed_rope(g * GROUP_WIDTH, qg_ref[...], HEAD_DIM ** -0.5), q_refs[g], dil)
        deinterleave(normed_rope(ATT_WIDTH + g * GROUP_WIDTH, kg_ref[...], 1.0), k_refs[g], dil)
        col0 = 2 * ATT_WIDTH + g * GROUP_WIDTH
        v = jnp.dot(hb, w_ref[:, col0:col0 + GROUP_WIDTH], preferred_element_type=jnp.float32)
        deinterleave(v, v_refs[g], dil)
    s_ref[...] = jnp.dot(hb, w_ref[:, ATT_COLS:], preferred_element_type=jnp.float32)


def _in_proj(x, ln1_g, w_in_bf16, q_gain, k_gain, rope, bd):
    b, t, _ = x.shape
    nt = t // ROW_TILE
    row2 = lambda bi, i: (0, 0)
    grp_shapes, grp_specs = [], []
    for _ in range(3):
        for dil in DILATIONS:
            grp_shapes.append(jax.ShapeDtypeStruct((b, dil, t // dil, GROUP_WIDTH), jnp.bfloat16))
            grp_specs.append(pl.BlockSpec((None, dil, ROW_TILE // dil, GROUP_WIDTH), lambda bi, i: (bi, 0, i, 0)))
    tab_spec = pl.BlockSpec((ROW_TILE, GROUP_WIDTH), lambda bi, i: (i, 0))
    outs = pl.pallas_call(
        _in_proj_kernel,
        grid=(b, nt),
        in_specs=[
            pl.BlockSpec((None, ROW_TILE, D_MODEL), lambda bi, i: (bi, i, 0)),
            pl.BlockSpec((1, D_MODEL), row2),
            pl.BlockSpec((D_MODEL, IN_COLS), row2),
            pl.BlockSpec((1, GROUP_WIDTH), row2),
            pl.BlockSpec((1, GROUP_WIDTH), row2),
            tab_spec, tab_spec, tab_spec,
            pl.BlockSpec((GROUP_WIDTH, GROUP_WIDTH), row2),
        ],
        out_specs=grp_specs + [pl.BlockSpec((None, ROW_TILE, SHIFT_COLS), lambda bi, i: (bi, i, 0))],
        out_shape=grp_shapes + [jax.ShapeDtypeStruct((b, t, SHIFT_COLS), jnp.float32)],
        scratch_shapes=[pltpu.VMEM((GROUP_WIDTH // LANES, ROW_TILE, LANES), jnp.float32)],
        compiler_params=_cparams(("parallel", "parallel")),
        name="in_proj",
    )(x, ln1_g.reshape(1, D_MODEL), w_in_bf16, q_gain, k_gain, *rope, bd)
    q = outs[0:3]
    k = outs[3:6]
    v = outs[6:9]
    return q, k, v, outs[9]


def _band_attn_kernel(q_ref, kp_ref, kc_ref, kn_ref, vp_ref, vc_ref, vn_ref, o_ref, lse_ref, *, seq_len, q_chunk):
    i = pl.program_id(1)
    kext = jnp.concatenate([kp_ref[...], kc_ref[...], kn_ref[...]], axis=0)
    vext = jnp.concatenate([vp_ref[...], vc_ref[...], vn_ref[...]], axis=0)
    kw = ATT_Q_BLOCK + 2 * BAND_HALF
    lane_head = lax.broadcasted_iota(jnp.int32, (ATT_Q_BLOCK, GROUP_WIDTH), 1) // HEAD_DIM
    qi = lax.broadcasted_iota(jnp.int32, (ATT_Q_BLOCK, kw), 0)
    kc = lax.broadcasted_iota(jnp.int32, (ATT_Q_BLOCK, kw), 1)
    diff = kc - BAND_HALF - qi
    band = (diff <= BAND_HALF) & (diff >= -BAND_HALF)
    for a in range(0, q_chunk, ATT_Q_BLOCK):
        kpos = i * q_chunk + (a - BAND_HALF) + kc
        valid = band & (kpos >= 0) & (kpos < seq_len)
        qb = q_ref[a:a + ATT_Q_BLOCK, :]
        kb = kext[a:a + kw]
        vb = vext[a:a + kw]
        acc = jnp.zeros((ATT_Q_BLOCK, GROUP_WIDTH), jnp.float32)
        lse = jnp.zeros((ATT_Q_BLOCK, GROUP_WIDTH), jnp.float32)
        for h in range(ATT_SLOTS):
            mine = lane_head == h
            qm = jnp.where(mine, qb, jnp.zeros_like(qb))
            s = lax.dot_general(qm, kb, (((1,), (1,)), ((), ())), preferred_element_type=jnp.float32)
            s = jnp.where(valid, s, NEG_INF)
            m = jnp.max(s, -1, keepdims=True)
            p = jnp.exp(s - m)
            den = jnp.sum(p, -1, keepdims=True)
            pv = jnp.dot(p.astype(jnp.bfloat16), vb, preferred_element_type=jnp.float32)
            acc = jnp.where(mine, pv / den, acc)
            lse = jnp.where(mine, m + jnp.log(den), lse)
        o_ref[a:a + ATT_Q_BLOCK, :] = acc
        lse_ref[a:a + ATT_Q_BLOCK, :] = lse


def _band_attn(q, k, v):
    b, dil, seq_len, _ = q.shape
    ns = b * dil
    q, k, v = (z.reshape(ns, seq_len, GROUP_WIDTH) for z in (q, k, v))
    q_chunk = min(seq_len, 512)
    halo_per_chunk = q_chunk // BAND_HALF
    last_halo = seq_len // BAND_HALF - 1
    cur = pl.BlockSpec((None, q_chunk, GROUP_WIDTH), lambda s, i: (s, i, 0))
    prev = pl.BlockSpec((None, BAND_HALF, GROUP_WIDTH), lambda s, i: (s, jnp.maximum(i * halo_per_chunk - 1, 0), 0))
    nxt = pl.BlockSpec((None, BAND_HALF, GROUP_WIDTH),
                       lambda s, i: (s, jnp.minimum((i + 1) * halo_per_chunk, last_halo), 0))
    o, lse = pl.pallas_call(
        functools.partial(_band_attn_kernel, seq_len=seq_len, q_chunk=q_chunk),
        grid=(ns, seq_len // q_chunk),
        in_specs=[cur, prev, cur, nxt, prev, cur, nxt],
        out_specs=[cur, cur],
        out_shape=[jax.ShapeDtypeStruct((ns, seq_len, GROUP_WIDTH), jnp.float32)] * 2,
        compiler_params=_cparams(("parallel", "parallel")),
        name="band_attn",
    )(q, k, k, k, v, v, v)
    shape4 = (b, dil, seq_len, GROUP_WIDTH)
    return o.reshape(shape4), lse.reshape(shape4)


def _softplus(x):
    return jnp.maximum(x, 0.0) + jnp.log1p(jnp.exp(-jnp.abs(x)))


def _sigmoid(x):
    return 1.0 / (1.0 + jnp.exp(-x))


def _rwkv_prep_kernel(sp_ref, sc_ref, sn_ref, mup_ref, mun_ref, w0_ref, w2_ref, a0_ref, a2_ref, g2_ref,
                      kk_ref, ka_ref, rk_ref, bd_ref,
                      lhs_ref, add_ref, bonus_ref, gate_ref,
                      r_s, v_s, nkk_s, e_s, kd_s, beta_s):
    i = pl.program_id(1)
    last = pl.num_programs(1) - 1
    p = sc_ref[...]
    rows = p.shape[0]
    row = lax.broadcasted_iota(jnp.int32, p.shape, 0)
    halo_prev = sp_ref[7:8, :] * (i > 0).astype(jnp.float32)
    halo_next = sn_ref[0:1, :] * (i < last).astype(jnp.float32)
    p_prev = jnp.where(row == 0, halo_prev, pltpu.roll(p, 1, 0))
    p_next = jnp.where(row == rows - 1, halo_next, pltpu.roll(p, rows - 1, 0))
    u = p + mup_ref[...] * (p_prev - p) + mun_ref[...] * (p_next - p)

    bd = bd_ref[...]
    r = u[:, 0:RW_WIDTH]
    k = u[:, RW_WIDTH:2 * RW_WIDTH]
    v = u[:, 2 * RW_WIDTH:3 * RW_WIDTH]
    c0 = 3 * RW_WIDTH
    lw = jnp.tanh(u[:, c0:c0 + 128]).astype(jnp.bfloat16)
    la = u[:, c0 + 128:c0 + 256].astype(jnp.bfloat16)
    lg = _sigmoid(u[:, c0 + 256:c0 + 384]).astype(jnp.bfloat16)

    kk = k * kk_ref[...]
    kk = kk * lax.rsqrt(_head_sum(kk * kk, bd) + 1e-12)
    r_s[...] = r
    v_s[...] = v
    nkk_s[...] = -kk
    gate_ref[...] = jnp.dot(lg, g2_ref[...], preferred_element_type=jnp.float32)
    bonus = jnp.zeros_like(v)
    for d in range(N_DIR):
        z = w0_ref[d:d + 1, :] + jnp.dot(lw, w2_ref[d], preferred_element_type=jnp.float32)
        w_log = -_softplus(-z) - 0.5
        e_s[d] = jnp.exp(w_log)
        a = _sigmoid(a0_ref[d:d + 1, :] + jnp.dot(la, a2_ref[d], preferred_element_type=jnp.float32))
        kd = k * (1.0 + (a - 1.0) * ka_ref[...])
        kd_s[d] = kd
        beta_s[d] = kk * a
        bonus = bonus + _head_sum(r * kd * rk_ref[...], bd) * v
    bonus_ref[...] = bonus

    idx_t = lax.broadcasted_iota(jnp.int32, (CHUNK, CHUNK), 0)
    idx_s = lax.broadcasted_iota(jnp.int32, (CHUNK, CHUNK), 1)
    eye = (idx_t == idx_s).astype(jnp.float32)
    bf = lambda z: z.astype(jnp.bfloat16)
    mm = lambda a_, b_: jnp.dot(bf(a_), bf(b_), preferred_element_type=jnp.float32)
    mm_nt = lambda a_, b_: lax.dot_general(bf(a_), bf(b_), (((1,), (1,)), ((), ())), preferred_element_type=jnp.float32)
    mm_tn = lambda a_, b_: lax.dot_general(bf(a_), bf(b_), (((0,), (0,)), ((), ())), preferred_element_type=jnp.float32)

    def chunk_body(c, carry):
        rows_c = pl.ds(pl.multiple_of(c * CHUNK, CHUNK), CHUNK)
        r_c, v_c, nkk_c = r_s[rows_c, :], v_s[rows_c, :], nkk_s[rows_c, :]
        units = []
        for d in range(N_DIR):
            before = (idx_s < idx_t) if d == 0 else (idx_s > idx_t)
            before_eq = (idx_s <= idx_t) if d == 0 else (idx_s >= idx_t)
            last = CHUNK - 1 if d == 0 else 0
            e_c = e_s[d, rows_c, :]
            cl = jnp.dot(before_eq.astype(jnp.float32), e_c, precision=HIGHEST, preferred_element_type=jnp.float32)
            w_inc = jnp.exp(-cl)
            w_exc = jnp.exp(e_c - cl)
            w_inv = jnp.exp(cl)
            w_tot = w_inc[last:last + 1, :]
            a_t = nkk_c * w_exc
            r_t = r_c * w_inc
            b_h = beta_s[d, rows_c, :] * w_inv
            k_h = kd_s[d, rows_c, :] * w_inv
            b_t = b_h * w_tot
            k_t = k_h * w_tot
            for h in range(RW_HEADS):
                hs = slice(h * HEAD_DIM, (h + 1) * HEAD_DIM)
                units.append(dict(before=before, before_eq=before_eq, a_t=a_t[:, hs], r_t=r_t[:, hs], b_h=b_h[:, hs],
                                  k_h=k_h[:, hs], b_t=b_t[:, hs], k_t=k_t[:, hs], w_tot=w_tot[:, hs], v=v_c[:, hs]))
        each = lambda fn: [fn(un) for un in units]
        g = each(lambda un: mm_nt(jnp.concatenate([un['a_t'], un['r_t']], axis=0),
                                  jnp.concatenate([un['b_h'], un['k_h']], axis=0)))
        for un, g_u in zip(units, g):
            un['m_ab'] = jnp.where(un['before'], g_u[:CHUNK, :CHUNK], 0.0)
            un['m_ak'] = jnp.where(un['before'], g_u[:CHUNK, CHUNK:], 0.0)
            un['n_rb'] = jnp.where(un['before_eq'], g_u[CHUNK:, :CHUNK], 0.0)
            un['n_rk'] = jnp.where(un['before_eq'], g_u[CHUNK:, CHUNK:], 0.0)
        t_inv = each(lambda un: eye + un['m_ab'])
        pw = each(lambda un: un['m_ab'])
        for _ in range(CHUNK.bit_length() - 2):
            pw = [mm(p_, p_) for p_ in pw]
            t_inv = [t_ + mm(t_, p_) for t_, p_ in zip(t_inv, pw)]
        mv = each(lambda un: mm(un['m_ak'], un['v']))
        nv = each(lambda un: mm(un['n_rk'], un['v']))
        kv = each(lambda un: mm_tn(un['k_t'], un['v']))
        tx = [mm(t_, jnp.concatenate([un['a_t'], mv_], axis=1)) for t_, un, mv_ in zip(t_inv, units, mv)]
        nx = [mm(un['n_rb'], tx_) for un, tx_ in zip(units, tx)]
        bx = [mm_tn(un['b_t'], tx_) for un, tx_ in zip(units, tx)]
        lhs_parts, add_parts = [], []
        for un, nx_, bx_, nv_, kv_ in zip(units, nx, bx, nv, kv):
            ra = un['r_t'] + nx_[:, :HEAD_DIM]
            y0 = nx_[:, HEAD_DIM:] + nv_
            phi = eye * un['w_tot'] + bx_[:, :HEAD_DIM]
            psi = bx_[:, HEAD_DIM:] + kv_
            lhs_parts.append(jnp.concatenate([ra, phi], axis=0))
            add_parts.append(jnp.concatenate([y0, psi], axis=0))
        for d in range(N_DIR):
            lhs_ref[d, c] = jnp.concatenate(lhs_parts[d * RW_HEADS:(d + 1) * RW_HEADS], axis=1)
            add_ref[d, c] = jnp.concatenate(add_parts[d * RW_HEADS:(d + 1) * RW_HEADS], axis=1)
        return carry

    lax.fori_loop(0, rows // CHUNK, chunk_body, 0)


def _rwkv_prep(s, mu_prev, mu_next, w0, w2p, a0, a2p, g2, k_k, k_a, r_k, bd):
    b, t, _ = s.shape
    nt = t // ROW_TILE
    halo_per_tile = ROW_TILE // 8
    last_halo = t // 8 - 1
    row2 = lambda bi, i: (0, 0)
    row3 = lambda bi, i: (0, 0, 0)
    vec = lambda n: pl.BlockSpec((1, n), row2)
    chunks_per_tile = ROW_TILE // CHUNK
    cspec = pl.BlockSpec((N_DIR, None, chunks_per_tile, 2 * CHUNK, RW_WIDTH), lambda bi, i: (0, bi, i, 0, 0))
    nspec = pl.BlockSpec((None, ROW_TILE, RW_WIDTH), lambda bi, i: (bi, i, 0))
    cshape = jax.ShapeDtypeStruct((N_DIR, b, t // CHUNK, 2 * CHUNK, RW_WIDTH), jnp.float32)
    nshape = jax.ShapeDtypeStruct((b, t, RW_WIDTH), jnp.float32)
    tile_f32 = pltpu.VMEM((ROW_TILE, RW_WIDTH), jnp.float32)
    tile2_f32 = pltpu.VMEM((N_DIR, ROW_TILE, RW_WIDTH), jnp.float32)
    return pl.pallas_call(
        _rwkv_prep_kernel,
        grid=(b, nt),
        in_specs=[
            pl.BlockSpec((None, 8, SHIFT_COLS), lambda bi, i: (bi, jnp.maximum(i * halo_per_tile - 1, 0), 0)),
            pl.BlockSpec((None, ROW_TILE, SHIFT_COLS), lambda bi, i: (bi, i, 0)),
            pl.BlockSpec((None, 8, SHIFT_COLS), lambda bi, i: (bi, jnp.minimum((i + 1) * halo_per_tile, last_halo), 0)),
            vec(SHIFT_COLS), vec(SHIFT_COLS),
            pl.BlockSpec((N_DIR, RW_WIDTH), row2),
            pl.BlockSpec((N_DIR, 128, RW_WIDTH), row3),
            pl.BlockSpec((N_DIR, RW_WIDTH), row2),
            pl.BlockSpec((N_DIR, 128, RW_WIDTH), row3),
            pl.BlockSpec((GATE_LORA, RW_WIDTH), row2),
            vec(RW_WIDTH), vec(RW_WIDTH), vec(RW_WIDTH),
            pl.BlockSpec((GROUP_WIDTH, GROUP_WIDTH), row2),
        ],
        out_specs=[cspec, cspec, nspec, nspec],
        out_shape=[cshape, cshape, nshape, nshape],
        scratch_shapes=[tile_f32, tile_f32, tile_f32, tile2_f32, tile2_f32, tile2_f32],
        compiler_params=_cparams(("parallel", "parallel")),
        name="rwkv_prep",
    )(s, s, s, mu_prev, mu_next, w0, w2p, a0, a2p, g2, k_k, k_a, r_k, bd)


def _split_bf16(x):
    hi = x.astype(jnp.bfloat16)
    return hi, (x - hi.astype(jnp.float32)).astype(jnp.bfloat16)


def _rwkv_scan_kernel(lhs_ref, add_ref, y_ref, state_ref):
    d = pl.program_id(0)
    n_chunks = lhs_ref.shape[0]

    @pl.when(pl.program_id(2) == 0)
    def _():
        state_ref[...] = jnp.zeros_like(state_ref)

    def chunk_body(cc, carry):
        c = cc + d * (n_chunks - 1 - 2 * cc)
        y_heads = []
        for h in range(RW_HEADS):
            hs = slice(h * HEAD_DIM, (h + 1) * HEAD_DIM)
            l_hi, l_lo = _split_bf16(lhs_ref[c, :, hs])
            s_hi, s_lo = _split_bf16(state_ref[h])
            dot = lambda a_, b_: jnp.dot(a_, b_, preferred_element_type=jnp.float32)
            res = dot(l_hi, s_hi) + dot(l_lo, s_hi) + dot(l_hi, s_lo) + add_ref[c, :, hs]
            y_heads.append(res[:CHUNK])
            state_ref[h] = res[CHUNK:]
        y_ref[pl.ds(pl.multiple_of(c * CHUNK, CHUNK), CHUNK), :] = jnp.concatenate(y_heads, axis=1)
        return carry

    lax.fori_loop(0, n_chunks, chunk_body, 0)


def _rwkv_scan(lhs, add):
    _, b, nchunk, _, _ = lhs.shape
    per_step = ROW_TILE // CHUNK
    nc = nchunk // per_step
    blk = lambda d, c: c + d * (nc - 1 - 2 * c)
    cspec = pl.BlockSpec((None, None, per_step, 2 * CHUNK, RW_WIDTH), lambda d, bi, c: (d, bi, blk(d, c), 0, 0))
    return pl.pallas_call(
        _rwkv_scan_kernel,
        grid=(N_DIR, b, nc),
        in_specs=[cspec, cspec],
        out_specs=pl.BlockSpec((None, None, ROW_TILE, RW_WIDTH), lambda d, bi, c: (d, bi, blk(d, c), 0)),
        out_shape=jax.ShapeDtypeStruct((N_DIR, b, nchunk * CHUNK, RW_WIDTH), jnp.float32),
        scratch_shapes=[pltpu.VMEM((RW_HEADS, HEAD_DIM, HEAD_DIM), jnp.float32)],
        compiler_params=_cparams(("parallel", "parallel", "arbitrary")),
        name="rwkv_scan",
    )(lhs, add)


def _out_proj_kernel(x_ref, o0_ref, l0_ref, o1_ref, l1_ref, o2_ref, l2_ref, y_ref, bonus_ref, gate_ref,
                     lnxg_ref, lnxb_ref, wout_ref, ln2_ref, router_ref, bd_ref,
                     xn_ref, h2_ref, aff_ref, so1_ref, sl1_ref, so2_ref, sl2_ref):
    rows = x_ref.shape[0]

    def interleave(src_ref, dst_ref, dil):
        halves = range(GROUP_WIDTH // LANES)
        for r in range(dil):
            for hf in halves:
                dst_ref[hf, pl.ds(r, rows // dil, stride=dil), :] = src_ref[r, :, hf * LANES:(hf + 1) * LANES]
        return jnp.concatenate([dst_ref[hf] for hf in halves], axis=-1)

    o1 = interleave(o1_ref, so1_ref, DILATIONS[1])
    l1 = interleave(l1_ref, sl1_ref, DILATIONS[1])
    o2 = interleave(o2_ref, so2_ref, DILATIONS[2])
    l2 = interleave(l2_ref, sl2_ref, DILATIONS[2])
    l0 = l0_ref[0]
    m = jnp.maximum(jnp.maximum(l0, l1), l2)
    e0, e1, e2 = jnp.exp(l0 - m), jnp.exp(l1 - m), jnp.exp(l2 - m)
    att = (e0 * o0_ref[0] + e1 * o1 + e2 * o2) / (e0 + e1 + e2)

    bd = bd_ref[...]
    y = y_ref[0] + y_ref[1]
    mu = _head_sum(y, bd) * (1.0 / HEAD_DIM)
    dlt = y - mu
    var = _head_sum(dlt * dlt, bd) * (1.0 / HEAD_DIM)
    yn = dlt * lax.rsqrt(var + GN_EPS) * lnxg_ref[...] + lnxb_ref[...]
    rw = (yn + bonus_ref[...]) * gate_ref[...]

    mixed = (jnp.dot(att.astype(jnp.bfloat16), wout_ref[0:GROUP_WIDTH, :], preferred_element_type=jnp.float32)
             + jnp.dot(rw.astype(jnp.bfloat16), wout_ref[GROUP_WIDTH:, :], preferred_element_type=jnp.float32))
    xn = x_ref[...] + mixed
    xn_ref[...] = xn
    h2 = xn * lax.rsqrt(jnp.mean(xn * xn, -1, keepdims=True) + NORM_EPS) * ln2_ref[...]
    h2_ref[...] = h2.astype(h2_ref.dtype)
    logits = lax.dot_general(router_ref[...], h2, (((1,), (1,)), ((), ())),
                             precision=HIGHEST, preferred_element_type=jnp.float32)
    ex = jnp.exp(logits - jnp.max(logits, 0, keepdims=True))
    aff_ref[...] = ex / jnp.sum(ex, 0, keepdims=True)


def _out_proj(x, att_parts, y, bonus, gate, lnx_g, lnx_b, w_out_bf16, ln2_g, router_t, bd):
    b, t, _ = x.shape
    nt = t // ROW_TILE
    row2 = lambda bi, i: (0, 0)
    tok = lambda w: pl.BlockSpec((None, ROW_TILE, w), lambda bi, i: (bi, i, 0))
    grp = lambda dil: pl.BlockSpec((None, dil, ROW_TILE // dil, GROUP_WIDTH), lambda bi, i: (bi, 0, i, 0))
    vec = lambda n: pl.BlockSpec((1, n), row2)
    grp_specs, grp_args = [], []
    for (o, lse), dil in zip(att_parts, DILATIONS):
        grp_specs += [grp(dil), grp(dil)]
        grp_args += [o, lse]
    return pl.pallas_call(
        _out_proj_kernel,
        grid=(b, nt),
        in_specs=[tok(D_MODEL)] + grp_specs + [
            pl.BlockSpec((N_DIR, None, ROW_TILE, RW_WIDTH), lambda bi, i: (0, bi, i, 0)),
            tok(RW_WIDTH), tok(RW_WIDTH), vec(RW_WIDTH), vec(RW_WIDTH),
            pl.BlockSpec((GROUP_WIDTH + RW_WIDTH, D_MODEL), row2),
            vec(D_MODEL),
            pl.BlockSpec((N_EXPERTS, D_MODEL), row2),
            pl.BlockSpec((GROUP_WIDTH, GROUP_WIDTH), row2),
        ],
        out_specs=[tok(D_MODEL), tok(D_MODEL),
                   pl.BlockSpec((N_EXPERTS, ROW_TILE), lambda bi, i: (0, bi * nt + i))],
        out_shape=[jax.ShapeDtypeStruct((b, t, D_MODEL), jnp.float32),
                   jax.ShapeDtypeStruct((b, t, D_MODEL), jnp.bfloat16),
                   jax.ShapeDtypeStruct((N_EXPERTS, b * t), jnp.float32)],
        scratch_shapes=[pltpu.VMEM((GROUP_WIDTH // LANES, ROW_TILE, LANES), jnp.float32)] * 4,
        compiler_params=_cparams(("parallel", "parallel")),
        name="out_proj",
    )(x, *grp_args, y, bonus, gate, lnx_g, lnx_b, w_out_bf16, ln2_g, router_t, bd)


FF_CHUNK = 512


def _expert_ffn_kernel(x_ref, gate_ref, wg_ref, wu_ref, wd_ref, o_ref):
    x = x_ref[...]
    acc = jnp.zeros(o_ref.shape, jnp.float32)
    for f0 in range(0, EXPERT_FF, FF_CHUNK):
        hg = jnp.dot(x, wg_ref[:, f0:f0 + FF_CHUNK], preferred_element_type=jnp.float32)
        hu = jnp.dot(x, wu_ref[:, f0:f0 + FF_CHUNK], preferred_element_type=jnp.float32)
        hid = (hg * _sigmoid(hg) * hu).astype(jnp.bfloat16)
        acc = acc + jnp.dot(hid, wd_ref[f0:f0 + FF_CHUNK, :], preferred_element_type=jnp.float32)
    o_ref[...] = acc * gate_ref[...]


def _expert_ffn(xs, gate, wg, wu, wd):
    e, cap, _ = xs.shape
    tile = min(FFN_ROW_TILE, cap)
    tok = lambda w: pl.BlockSpec((None, tile, w), lambda ei, i: (ei, i, 0))
    wspec = lambda r, c: pl.BlockSpec((None, r, c), lambda ei, i: (ei, 0, 0))
    return pl.pallas_call(
        _expert_ffn_kernel,
        grid=(e, cap // tile),
        in_specs=[tok(D_MODEL), tok(1), wspec(D_MODEL, EXPERT_FF), wspec(D_MODEL, EXPERT_FF), wspec(EXPERT_FF, D_MODEL)],
        out_specs=tok(D_MODEL),
        out_shape=jax.ShapeDtypeStruct((e, cap, D_MODEL), jnp.float32),
        compiler_params=_cparams(("parallel", "arbitrary")),
        name="expert_ffn",
    )(xs, gate, wg, wu, wd)


def _pad_lora(w):
    z = jnp.zeros((N_DIR, N_DIR * w.shape[1], w.shape[2]), w.dtype)
    for d in range(N_DIR):
        z = z.at[d, d * w.shape[1]:(d + 1) * w.shape[1]].set(w[d])
    return z.astype(jnp.bfloat16)


def _layer_params(l, ln1_g, ln2_g, w_in, mu_prev, mu_next, q_norm_g, k_norm_g, w0, w2, a0, a2, g2, k_k, k_a,
                  r_k, lnx_g, lnx_b, w_out, router, w_gate, w_up, w_down):
    bf = lambda z: z.astype(jnp.bfloat16)
    row = lambda z: z.reshape(1, -1)
    return dict(
        ln1_g=ln1_g[l], ln2_g=row(ln2_g[l]), w_in=bf(w_in[l]), mu_prev=row(mu_prev[l]), mu_next=row(mu_next[l]),
        q_gain=row(jnp.tile(q_norm_g[l], ATT_SLOTS)), k_gain=row(jnp.tile(k_norm_g[l], ATT_SLOTS)),
        w0=w0[l], w2=_pad_lora(w2[l]), a0=a0[l], a2=_pad_lora(a2[l]), g2=bf(g2[l]),
        k_k=row(k_k[l]), k_a=row(k_a[l]), r_k=row(r_k[l]), lnx_g=row(lnx_g[l]), lnx_b=row(lnx_b[l]),
        w_out=bf(w_out[l]), router_t=router[l].T, w_gate=bf(w_gate[l]), w_up=bf(w_up[l]), w_down=bf(w_down[l]),
    )


def _encoder_layer(x, p, rope, bd):
    b, t, _ = x.shape
    n = b * t
    q, k, v, s = _in_proj(x, p['ln1_g'], p['w_in'], p['q_gain'], p['k_gain'], rope, bd)
    att_parts = [_band_attn(q[g], k[g], v[g]) for g in range(N_DIL)]
    lhs, add, bonus, gate = _rwkv_prep(
        s, p['mu_prev'], p['mu_next'], p['w0'], p['w2'], p['a0'], p['a2'], p['g2'], p['k_k'], p['k_a'], p['r_k'], bd)
    y = _rwkv_scan(lhs, add)
    xn, h2, aff_t = _out_proj(x, att_parts, y, bonus, gate, p['lnx_g'], p['lnx_b'], p['w_out'], p['ln2_g'],
                              p['router_t'], bd)
    cap = max(1, (EC_CAPACITY_FACTOR * n) // N_EXPERTS)
    gate_e, idx = lax.top_k(aff_t, cap)
    xs = h2.reshape(n, D_MODEL)[idx]
    ye = _expert_ffn(xs, gate_e[..., None], p['w_gate'], p['w_up'], p['w_down'])
    moe = jnp.zeros((n, D_MODEL), jnp.float32).at[idx.reshape(-1)].add(ye.reshape(-1, D_MODEL))
    return xn + moe.reshape(b, t, D_MODEL)


def kernel(x_prompt, x_sample, ln1_g, ln2_g, w_in, mu_prev, mu_next, q_norm_g, k_norm_g, w0, w2, a0, a2, g2,
           k_k, k_a, r_k, lnx_g, lnx_b, w_out, router, w_gate, w_up, w_down):
    depth = w_in.shape[0]
    bd = _head_sum_matrix()
    rope_p = _rope_tables(x_prompt.shape[1])
    rope_s = _rope_tables(x_sample.shape[1])
    y_prompt, y_sample = x_prompt, x_sample
    for l in range(depth):
        p = _layer_params(l, ln1_g, ln2_g, w_in, mu_prev, mu_next, q_norm_g, k_norm_g, w0, w2, a0, a2, g2, k_k,
                          k_a, r_k, lnx_g, lnx_b, w_out, router, w_gate, w_up, w_down)
        y_prompt = _encoder_layer(y_prompt, p, rope_p, bd)
        y_sample = _encoder_layer(y_sample, p, rope_s, bd)
    return (y_prompt, y_sample)
```

```python
import functools

import jax
import jax.numpy as jnp
import numpy as np
from jax import lax
from jax.experimental import pallas as pl
from jax.experimental.pallas import tpu as pltpu

D_MODEL = 1024
RW_HEADS = 4
HEAD_DIM = 64
RW_WIDTH = RW_HEADS * HEAD_DIM
DECAY_LORA = 64
AAA_LORA = 64
GATE_LORA = 128
N_DIR = 2
GN_EPS = 64e-5
ATT_SLOTS = 4
DILATIONS = (1, 4, 16)
BAND_HALF = 64
N_DIL = len(DILATIONS)
GROUP_WIDTH = ATT_SLOTS * HEAD_DIM
ATT_WIDTH = N_DIL * GROUP_WIDTH
ATT_COLS = 3 * ATT_WIDTH
SHIFT_COLS = 3 * RW_WIDTH + N_DIR * DECAY_LORA + N_DIR * AAA_LORA + GATE_LORA
IN_COLS = ATT_COLS + SHIFT_COLS
ROPE_THETA = 500000.0
ROPE_DIMS = HEAD_DIM // 4
ROPE_HALF = ROPE_DIMS // 2
N_EXPERTS = 16
EC_CAPACITY_FACTOR = 2
EXPERT_FF = 2048
NORM_EPS = 1e-6
NEG_INF = -1e30
LANES = 128

ROW_TILE = 512
ATT_Q_BLOCK = 128
CHUNK = 64
CHUNKS_PER_ITER = 4
FFN_ROW_TILE = 512
VMEM_LIMIT = 56 * 1024 * 1024

HIGHEST = lax.Precision.HIGHEST


def _cparams(sem):
    return pltpu.CompilerParams(dimension_semantics=sem, vmem_limit_bytes=VMEM_LIMIT)


def _head_sum_matrix():
    h = np.arange(GROUP_WIDTH) // HEAD_DIM
    return jnp.asarray((h[:, None] == h[None, :]).astype(np.float32), dtype=jnp.bfloat16)


def _head_sum(x, bd):
    hi = x.astype(jnp.bfloat16)
    lo = (x - hi.astype(jnp.float32)).astype(jnp.bfloat16)
    return (jnp.dot(hi, bd, preferred_element_type=jnp.float32)
            + jnp.dot(lo, bd, preferred_element_type=jnp.float32))


def _rope_tables(t):
    inv = jnp.power(jnp.float32(ROPE_THETA), -jnp.arange(ROPE_HALF, dtype=jnp.float32) * 2.0 / ROPE_DIMS)
    ang = jnp.arange(t, dtype=jnp.float32)[:, None] * inv[None, :]
    cos, sin = jnp.cos(ang), jnp.sin(ang)
    zeros = jnp.zeros((t, HEAD_DIM - ROPE_DIMS), jnp.float32)
    zero8 = jnp.zeros((t, ROPE_HALF), jnp.float32)
    c = jnp.concatenate([cos, cos, zeros + 1.0], -1)
    s1 = jnp.concatenate([zero8, sin, zeros], -1)
    s2 = jnp.concatenate([-sin, zero8, zeros], -1)
    tile = lambda z: jnp.tile(z, (1, ATT_SLOTS))
    return tile(c), tile(s1), tile(s2)


def _in_proj_kernel(x_ref, g_ref, w_ref, qg_ref, kg_ref, c_ref, s1_ref, s2_ref, bd_ref,
                    q0_ref, q1_ref, q2_ref, k0_ref, k1_ref, k2_ref, v0_ref, v1_ref, v2_ref, s_ref,
                    tmp_ref):
    x = x_ref[...]
    h = x * lax.rsqrt(jnp.mean(x * x, -1, keepdims=True) + NORM_EPS) * g_ref[...]
    hb = h.astype(jnp.bfloat16)
    bd = bd_ref[...]
    c, s1, s2 = c_ref[...], s1_ref[...], s2_ref[...]
    rows = x.shape[0]

    def deinterleave(val, out_ref, dil):
        if dil == 1:
            out_ref[0] = val.astype(out_ref.dtype)
            return
        for hf in range(GROUP_WIDTH // LANES):
            tmp_ref[hf] = val[:, hf * LANES:(hf + 1) * LANES]
        for r in range(dil):
            for hf in range(GROUP_WIDTH // LANES):
                out_ref[r, :, hf * LANES:(hf + 1) * LANES] = (
                    tmp_ref[hf, pl.ds(r, rows // dil, stride=dil), :].astype(out_ref.dtype))

    def normed_rope(col0, gain, scale):
        p = jnp.dot(hb, w_ref[:, col0:col0 + GROUP_WIDTH], preferred_element_type=jnp.float32)
        ms = _head_sum(p * p, bd) * (1.0 / HEAD_DIM)
        n = p * lax.rsqrt(ms + NORM_EPS) * gain
        n = n * c + pltpu.roll(n, ROPE_HALF, 1) * s1 + pltpu.roll(n, GROUP_WIDTH - ROPE_HALF, 1) * s2
        return n * scale if scale != 1.0 else n

    q_refs = (q0_ref, q1_ref, q2_ref)
    k_refs = (k0_ref, k1_ref, k2_ref)
    v_refs = (v0_ref, v1_ref, v2_ref)
    for g, dil in enumerate(DILATIONS):
        deinterleave(normed_rope(g * GROUP_WIDTH, qg_ref[...], HEAD_DIM ** -0.5), q_refs[g], dil)
        deinterleave(normed_rope(ATT_WIDTH + g * GROUP_WIDTH, kg_ref[...], 1.0), k_refs[g], dil)
        col0 = 2 * ATT_WIDTH + g * GROUP_WIDTH
        v = jnp.dot(hb, w_ref[:, col0:col0 + GROUP_WIDTH], preferred_element_type=jnp.float32)
        deinterleave(v, v_refs[g], dil)
    s_ref[...] = jnp.dot(hb, w_ref[:, ATT_COLS:], preferred_element_type=jnp.float32)


def _in_proj(x, ln1_g, w_in_bf16, q_gain, k_gain, rope, bd):
    b, t, _ = x.shape
    nt = t // ROW_TILE
    row2 = lambda bi, i: (0, 0)
    grp_shapes, grp_specs = [], []
    for _ in range(3):
        for dil in DILATIONS:
            grp_shapes.append(jax.ShapeDtypeStruct((b, dil, t // dil, GROUP_WIDTH), jnp.bfloat16))
            grp_specs.append(pl.BlockSpec((None, dil, ROW_TILE // dil, GROUP_WIDTH), lambda bi, i: (bi, 0, i, 0)))
    tab_spec = pl.BlockSpec((ROW_TILE, GROUP_WIDTH), lambda bi, i: (i, 0))
    outs = pl.pallas_call(
        _in_proj_kernel,
        grid=(b, nt),
        in_specs=[
            pl.BlockSpec((None, ROW_TILE, D_MODEL), lambda bi, i: (bi, i, 0)),
            pl.BlockSpec((1, D_MODEL), row2),
            pl.BlockSpec((D_MODEL, IN_COLS), row2),
            pl.BlockSpec((1, GROUP_WIDTH), row2),
            pl.BlockSpec((1, GROUP_WIDTH), row2),
            tab_spec, tab_spec, tab_spec,
            pl.BlockSpec((GROUP_WIDTH, GROUP_WIDTH), row2),
        ],
        out_specs=grp_specs + [pl.BlockSpec((None, ROW_TILE, SHIFT_COLS), lambda bi, i: (bi, i, 0))],
        out_shape=grp_shapes + [jax.ShapeDtypeStruct((b, t, SHIFT_COLS), jnp.float32)],
        scratch_shapes=[pltpu.VMEM((GROUP_WIDTH // LANES, ROW_TILE, LANES), jnp.float32)],
        compiler_params=_cparams(("parallel", "parallel")),
        name="in_proj",
    )(x, ln1_g.reshape(1, D_MODEL), w_in_bf16, q_gain, k_gain, *rope, bd)
    q = outs[0:3]
    k = outs[3:6]
    v = outs[6:9]
    return q, k, v, outs[9]


def _band_attn_kernel(q_ref, kp_ref, kc_ref, kn_ref, vp_ref, vc_ref, vn_ref, o_ref, lse_ref, *, seq_len, q_chunk):
    i = pl.program_id(1)
    kext = jnp.concatenate([kp_ref[...], kc_ref[...], kn_ref[...]], axis=0)
    vext = jnp.concatenate([vp_ref[...], vc_ref[...], vn_ref[...]], axis=0)
    kw = ATT_Q_BLOCK + 2 * BAND_HALF
    lane_head = lax.broadcasted_iota(jnp.int32, (ATT_Q_BLOCK, GROUP_WIDTH), 1) // HEAD_DIM
    qi = lax.broadcasted_iota(jnp.int32, (ATT_Q_BLOCK, kw), 0)
    kc = lax.broadcasted_iota(jnp.int32, (ATT_Q_BLOCK, kw), 1)
    diff = kc - BAND_HALF - qi
    band = (diff <= BAND_HALF) & (diff >= -BAND_HALF)
    for a in range(0, q_chunk, ATT_Q_BLOCK):
        kpos = i * q_chunk + (a - BAND_HALF) + kc
        valid = band & (kpos >= 0) & (kpos < seq_len)
        qb = q_ref[a:a + ATT_Q_BLOCK, :]
        kb = kext[a:a + kw]
        vb = vext[a:a + kw]
        heads = range(ATT_SLOTS)
        mine = [lane_head == h for h in heads]
        s = [lax.dot_general(jnp.where(mine[h], qb, jnp.zeros_like(qb)), kb, (((1,), (1,)), ((), ())),
                             preferred_element_type=jnp.float32) for h in heads]
        s = [jnp.where(valid, s_h, NEG_INF) for s_h in s]
        m = [jnp.max(s_h, -1, keepdims=True) for s_h in s]
        p = [jnp.exp(s_h - m_h) for s_h, m_h in zip(s, m)]
        den = [jnp.sum(p_h, -1, keepdims=True) for p_h in p]
        pv = [jnp.dot(p_h.astype(jnp.bfloat16), vb, preferred_element_type=jnp.float32) for p_h in p]
        acc = jnp.zeros((ATT_Q_BLOCK, GROUP_WIDTH), jnp.float32)
        lse = jnp.zeros((ATT_Q_BLOCK, GROUP_WIDTH), jnp.float32)
        for h in heads:
            acc = jnp.where(mine[h], pv[h] / den[h], acc)
            lse = jnp.where(mine[h], m[h] + jnp.log(den[h]), lse)
        o_ref[a:a + ATT_Q_BLOCK, :] = acc
        lse_ref[a:a + ATT_Q_BLOCK, :] = lse


def _band_attn(q, k, v):
    b, dil, seq_len, _ = q.shape
    ns = b * dil
    q, k, v = (z.reshape(ns, seq_len, GROUP_WIDTH) for z in (q, k, v))
    q_chunk = min(seq_len, 512)
    halo_per_chunk = q_chunk // BAND_HALF
    last_halo = seq_len // BAND_HALF - 1
    cur = pl.BlockSpec((None, q_chunk, GROUP_WIDTH), lambda s, i: (s, i, 0))
    prev = pl.BlockSpec((None, BAND_HALF, GROUP_WIDTH), lambda s, i: (s, jnp.maximum(i * halo_per_chunk - 1, 0), 0))
    nxt = pl.BlockSpec((None, BAND_HALF, GROUP_WIDTH),
                       lambda s, i: (s, jnp.minimum((i + 1) * halo_per_chunk, last_halo), 0))
    o, lse = pl.pallas_call(
        functools.partial(_band_attn_kernel, seq_len=seq_len, q_chunk=q_chunk),
        grid=(ns, seq_len // q_chunk),
        in_specs=[cur, prev, cur, nxt, prev, cur, nxt],
        out_specs=[cur, cur],
        out_shape=[jax.ShapeDtypeStruct((ns, seq_len, GROUP_WIDTH), jnp.float32)] * 2,
        compiler_params=_cparams(("parallel", "parallel")),
        name="band_attn",
    )(q, k, k, k, v, v, v)
    shape4 = (b, dil, seq_len, GROUP_WIDTH)
    return o.reshape(shape4), lse.reshape(shape4)


def _softplus(x):
    return jnp.maximum(x, 0.0) + jnp.log1p(jnp.exp(-jnp.abs(x)))


def _sigmoid(x):
    return 1.0 / (1.0 + jnp.exp(-x))


def _rwkv_prep_kernel(sp_ref, sc_ref, sn_ref, mup_ref, mun_ref, w0_ref, w2_ref, a0_ref, a2_ref, g2_ref,
                      kk_ref, ka_ref, rk_ref, bd_ref,
                      lhs_ref, add_ref, bonus_ref, gate_ref,
                      r_s, v_s, nkk_s, e_s, kd_s, beta_s):
    i = pl.program_id(1)
    last = pl.num_programs(1) - 1
    p = sc_ref[...]
    rows = p.shape[0]
    row = lax.broadcasted_iota(jnp.int32, p.shape, 0)
    halo_prev = sp_ref[7:8, :] * (i > 0).astype(jnp.float32)
    halo_next = sn_ref[0:1, :] * (i < last).astype(jnp.float32)
    p_prev = jnp.where(row == 0, halo_prev, pltpu.roll(p, 1, 0))
    p_next = jnp.where(row == rows - 1, halo_next, pltpu.roll(p, rows - 1, 0))
    u = p + mup_ref[...] * (p_prev - p) + mun_ref[...] * (p_next - p)

    bd = bd_ref[...]
    r = u[:, 0:RW_WIDTH]
    k = u[:, RW_WIDTH:2 * RW_WIDTH]
    v = u[:, 2 * RW_WIDTH:3 * RW_WIDTH]
    c0 = 3 * RW_WIDTH
    lw = jnp.tanh(u[:, c0:c0 + 128]).astype(jnp.bfloat16)
    la = u[:, c0 + 128:c0 + 256].astype(jnp.bfloat16)
    lg = _sigmoid(u[:, c0 + 256:c0 + 384]).astype(jnp.bfloat16)

    kk = k * kk_ref[...]
    kk = kk * lax.rsqrt(_head_sum(kk * kk, bd) + 1e-12)
    r_s[...] = r
    v_s[...] = v
    nkk_s[...] = -kk
    gate_ref[...] = jnp.dot(lg, g2_ref[...], preferred_element_type=jnp.float32)
    bonus = jnp.zeros_like(v)
    for d in range(N_DIR):
        z = w0_ref[d:d + 1, :] + jnp.dot(lw, w2_ref[d], preferred_element_type=jnp.float32)
        w_log = -_softplus(-z) - 0.5
        e_s[d] = jnp.exp(w_log)
        a = _sigmoid(a0_ref[d:d + 1, :] + jnp.dot(la, a2_ref[d], preferred_element_type=jnp.float32))
        kd = k * (1.0 + (a - 1.0) * ka_ref[...])
        kd_s[d] = kd
        beta_s[d] = kk * a
        bonus = bonus + _head_sum(r * kd * rk_ref[...], bd) * v
    bonus_ref[...] = bonus

    idx_t = lax.broadcasted_iota(jnp.int32, (CHUNK, CHUNK), 0)
    idx_s = lax.broadcasted_iota(jnp.int32, (CHUNK, CHUNK), 1)
    eye = (idx_t == idx_s).astype(jnp.float32)
    bf = lambda z: z.astype(jnp.bfloat16)
    mm = lambda a_, b_: jnp.dot(bf(a_), bf(b_), preferred_element_type=jnp.float32)
    mm_nt = lambda a_, b_: lax.dot_general(bf(a_), bf(b_), (((1,), (1,)), ((), ())), preferred_element_type=jnp.float32)
    mm_tn = lambda a_, b_: lax.dot_general(bf(a_), bf(b_), (((0,), (0,)), ((), ())), preferred_element_type=jnp.float32)

    def chunk_body(cp, carry):
        units = []
        for c, d in [(cp * CHUNKS_PER_ITER + j, d_) for j in range(CHUNKS_PER_ITER) for d_ in range(N_DIR)]:
            rows_c = pl.ds(pl.multiple_of(c * CHUNK, CHUNK), CHUNK)
            r_c, v_c, nkk_c = r_s[rows_c, :], v_s[rows_c, :], nkk_s[rows_c, :]
            before = (idx_s < idx_t) if d == 0 else (idx_s > idx_t)
            before_eq = (idx_s <= idx_t) if d == 0 else (idx_s >= idx_t)
            last = CHUNK - 1 if d == 0 else 0
            e_c = e_s[d, rows_c, :]
            cl = jnp.dot(before_eq.astype(jnp.float32), e_c, precision=HIGHEST, preferred_element_type=jnp.float32)
            w_inc = jnp.exp(-cl)
            w_exc = jnp.exp(e_c - cl)
            w_inv = jnp.exp(cl)
            w_tot = w_inc[last:last + 1, :]
            a_t = nkk_c * w_exc
            r_t = r_c * w_inc
            b_h = beta_s[d, rows_c, :] * w_inv
            k_h = kd_s[d, rows_c, :] * w_inv
            b_t = b_h * w_tot
            k_t = k_h * w_tot
            for h in range(RW_HEADS):
                hs = slice(h * HEAD_DIM, (h + 1) * HEAD_DIM)
                units.append(dict(chunk=c, dir=d, before=before, before_eq=before_eq, a_t=a_t[:, hs], r_t=r_t[:, hs], b_h=b_h[:, hs],
                                  k_h=k_h[:, hs], b_t=b_t[:, hs], k_t=k_t[:, hs], w_tot=w_tot[:, hs], v=v_c[:, hs]))
        each = lambda fn: [fn(un) for un in units]
        g = each(lambda un: mm_nt(jnp.concatenate([un['a_t'], un['r_t']], axis=0),
                                  jnp.concatenate([un['b_h'], un['k_h']], axis=0)))
        for un, g_u in zip(units, g):
            un['m_ab'] = jnp.where(un['before'], g_u[:CHUNK, :CHUNK], 0.0)
            un['m_ak'] = jnp.where(un['before'], g_u[:CHUNK, CHUNK:], 0.0)
            un['n_rb'] = jnp.where(un['before_eq'], g_u[CHUNK:, :CHUNK], 0.0)
            un['n_rk'] = jnp.where(un['before_eq'], g_u[CHUNK:, CHUNK:], 0.0)
        t_inv = each(lambda un: eye + un['m_ab'])
        pw = each(lambda un: un['m_ab'])
        for _ in range(CHUNK.bit_length() - 2):
            pw = [mm(p_, p_) for p_ in pw]
            t_inv = [t_ + mm(t_, p_) for t_, p_ in zip(t_inv, pw)]
        mv = each(lambda un: mm(un['m_ak'], un['v']))
        nv = each(lambda un: mm(un['n_rk'], un['v']))
        kv = each(lambda un: mm_tn(un['k_t'], un['v']))
        tx = [mm(t_, jnp.concatenate([un['a_t'], mv_], axis=1)) for t_, un, mv_ in zip(t_inv, units, mv)]
        nx = [mm(un['n_rb'], tx_) for un, tx_ in zip(units, tx)]
        bx = [mm_tn(un['b_t'], tx_) for un, tx_ in zip(units, tx)]
        lhs_parts, add_parts = [], []
        for un, nx_, bx_, nv_, kv_ in zip(units, nx, bx, nv, kv):
            ra = un['r_t'] + nx_[:, :HEAD_DIM]
            y0 = nx_[:, HEAD_DIM:] + nv_
            phi = eye * un['w_tot'] + bx_[:, :HEAD_DIM]
            psi = bx_[:, HEAD_DIM:] + kv_
            lhs_parts.append(jnp.concatenate([ra, phi], axis=0))
            add_parts.append(jnp.concatenate([y0, psi], axis=0))
        for g0 in range(0, len(units), RW_HEADS):
            c, d = units[g0]['chunk'], units[g0]['dir']
            lhs_ref[d, c] = jnp.concatenate(lhs_parts[g0:g0 + RW_HEADS], axis=1)
            add_ref[d, c] = jnp.concatenate(add_parts[g0:g0 + RW_HEADS], axis=1)
        return carry

    lax.fori_loop(0, rows // (CHUNK * CHUNKS_PER_ITER), chunk_body, 0)


def _rwkv_prep(s, mu_prev, mu_next, w0, w2p, a0, a2p, g2, k_k, k_a, r_k, bd):
    b, t, _ = s.shape
    nt = t // ROW_TILE
    halo_per_tile = ROW_TILE // 8
    last_halo = t // 8 - 1
    row2 = lambda bi, i: (0, 0)
    row3 = lambda bi, i: (0, 0, 0)
    vec = lambda n: pl.BlockSpec((1, n), row2)
    chunks_per_tile = ROW_TILE // CHUNK
    cspec = pl.BlockSpec((N_DIR, None, chunks_per_tile, 2 * CHUNK, RW_WIDTH), lambda bi, i: (0, bi, i, 0, 0))
    nspec = pl.BlockSpec((None, ROW_TILE, RW_WIDTH), lambda bi, i: (bi, i, 0))
    cshape = jax.ShapeDtypeStruct((N_DIR, b, t // CHUNK, 2 * CHUNK, RW_WIDTH), jnp.float32)
    nshape = jax.ShapeDtypeStruct((b, t, RW_WIDTH), jnp.float32)
    tile_f32 = pltpu.VMEM((ROW_TILE, RW_WIDTH), jnp.float32)
    tile2_f32 = pltpu.VMEM((N_DIR, ROW_TILE, RW_WIDTH), jnp.float32)
    return pl.pallas_call(
        _rwkv_prep_kernel,
        grid=(b, nt),
        in_specs=[
            pl.BlockSpec((None, 8, SHIFT_COLS), lambda bi, i: (bi, jnp.maximum(i * halo_per_tile - 1, 0), 0)),
            pl.BlockSpec((None, ROW_TILE, SHIFT_COLS), lambda bi, i: (bi, i, 0)),
            pl.BlockSpec((None, 8, SHIFT_COLS), lambda bi, i: (bi, jnp.minimum((i + 1) * halo_per_tile, last_halo), 0)),
            vec(SHIFT_COLS), vec(SHIFT_COLS),
            pl.BlockSpec((N_DIR, RW_WIDTH), row2),
            pl.BlockSpec((N_DIR, 128, RW_WIDTH), row3),
            pl.BlockSpec((N_DIR, RW_WIDTH), row2),
            pl.BlockSpec((N_DIR, 128, RW_WIDTH), row3),
            pl.BlockSpec((GATE_LORA, RW_WIDTH), row2),
            vec(RW_WIDTH), vec(RW_WIDTH), vec(RW_WIDTH),
            pl.BlockSpec((GROUP_WIDTH, GROUP_WIDTH), row2),
        ],
        out_specs=[cspec, cspec, nspec, nspec],
        out_shape=[cshape, cshape, nshape, nshape],
        scratch_shapes=[tile_f32, tile_f32, tile_f32, tile2_f32, tile2_f32, tile2_f32],
        compiler_params=_cparams(("parallel", "parallel")),
        name="rwkv_prep",
    )(s, s, s, mu_prev, mu_next, w0, w2p, a0, a2p, g2, k_k, k_a, r_k, bd)


def _split_bf16(x):
    hi = x.astype(jnp.bfloat16)
    return hi, (x - hi.astype(jnp.float32)).astype(jnp.bfloat16)


def _rwkv_scan_kernel(lhs_ref, add_ref, y_ref, state_ref):
    d = pl.program_id(0)
    n_chunks = lhs_ref.shape[0]

    @pl.when(pl.program_id(2) == 0)
    def _():
        state_ref[...] = jnp.zeros_like(state_ref)

    def chunk_body(cc, carry):
        c = cc + d * (n_chunks - 1 - 2 * cc)
        y_heads = []
        for h in range(RW_HEADS):
            hs = slice(h * HEAD_DIM, (h + 1) * HEAD_DIM)
            l_hi, l_lo = _split_bf16(lhs_ref[c, :, hs])
            s_hi, s_lo = _split_bf16(state_ref[h])
            dot = lambda a_, b_: jnp.dot(a_, b_, preferred_element_type=jnp.float32)
            res = dot(l_hi, s_hi) + dot(l_lo, s_hi) + dot(l_hi, s_lo) + add_ref[c, :, hs]
            y_heads.append(res[:CHUNK])
            state_ref[h] = res[CHUNK:]
        y_ref[pl.ds(pl.multiple_of(c * CHUNK, CHUNK), CHUNK), :] = jnp.concatenate(y_heads, axis=1)
        return carry

    lax.fori_loop(0, n_chunks, chunk_body, 0)


def _rwkv_scan(lhs, add):
    _, b, nchunk, _, _ = lhs.shape
    per_step = ROW_TILE // CHUNK
    nc = nchunk // per_step
    blk = lambda d, c: c + d * (nc - 1 - 2 * c)
    cspec = pl.BlockSpec((None, None, per_step, 2 * CHUNK, RW_WIDTH), lambda d, bi, c: (d, bi, blk(d, c), 0, 0))
    return pl.pallas_call(
        _rwkv_scan_kernel,
        grid=(N_DIR, b, nc),
        in_specs=[cspec, cspec],
        out_specs=pl.BlockSpec((None, None, ROW_TILE, RW_WIDTH), lambda d, bi, c: (d, bi, blk(d, c), 0)),
        out_shape=jax.ShapeDtypeStruct((N_DIR, b, nchunk * CHUNK, RW_WIDTH), jnp.float32),
        scratch_shapes=[pltpu.VMEM((RW_HEADS, HEAD_DIM, HEAD_DIM), jnp.float32)],
        compiler_params=_cparams(("parallel", "parallel", "arbitrary")),
        name="rwkv_scan",
    )(lhs, add)


def _out_proj_kernel(x_ref, o0_ref, l0_ref, o1_ref, l1_ref, o2_ref, l2_ref, y_ref, bonus_ref, gate_ref,
                     lnxg_ref, lnxb_ref, wout_ref, ln2_ref, router_ref, bd_ref,
                     xn_ref, h2_ref, aff_ref, so1_ref, sl1_ref, so2_ref, sl2_ref):
    rows = x_ref.shape[0]

    def interleave(src_ref, dst_ref, dil):
        halves = range(GROUP_WIDTH // LANES)
        for r in range(dil):
            for hf in halves:
                dst_ref[hf, pl.ds(r, rows // dil, stride=dil), :] = src_ref[r, :, hf * LANES:(hf + 1) * LANES]
        return jnp.concatenate([dst_ref[hf] for hf in halves], axis=-1)

    o1 = interleave(o1_ref, so1_ref, DILATIONS[1])
    l1 = interleave(l1_ref, sl1_ref, DILATIONS[1])
    o2 = interleave(o2_ref, so2_ref, DILATIONS[2])
    l2 = interleave(l2_ref, sl2_ref, DILATIONS[2])
    l0 = l0_ref[0]
    m = jnp.maximum(jnp.maximum(l0, l1), l2)
    e0, e1, e2 = jnp.exp(l0 - m), jnp.exp(l1 - m), jnp.exp(l2 - m)
    att = (e0 * o0_ref[0] + e1 * o1 + e2 * o2) / (e0 + e1 + e2)

    bd = bd_ref[...]
    y = y_ref[0] + y_ref[1]
    mu = _head_sum(y, bd) * (1.0 / HEAD_DIM)
    dlt = y - mu
    var = _head_sum(dlt * dlt, bd) * (1.0 / HEAD_DIM)
    yn = dlt * lax.rsqrt(var + GN_EPS) * lnxg_ref[...] + lnxb_ref[...]
    rw = (yn + bonus_ref[...]) * gate_ref[...]

    mixed = (jnp.dot(att.astype(jnp.bfloat16), wout_ref[0:GROUP_WIDTH, :], preferred_element_type=jnp.float32)
             + jnp.dot(rw.astype(jnp.bfloat16), wout_ref[GROUP_WIDTH:, :], preferred_element_type=jnp.float32))
    xn = x_ref[...] + mixed
    xn_ref[...] = xn
    h2 = xn * lax.rsqrt(jnp.mean(xn * xn, -1, keepdims=True) + NORM_EPS) * ln2_ref[...]
    h2_ref[...] = h2.astype(h2_ref.dtype)
    logits = lax.dot_general(router_ref[...], h2, (((1,), (1,)), ((), ())),
                             precision=HIGHEST, preferred_element_type=jnp.float32)
    ex = jnp.exp(logits - jnp.max(logits, 0, keepdims=True))
    aff_ref[...] = ex / jnp.sum(ex, 0, keepdims=True)


def _out_proj(x, att_parts, y, bonus, gate, lnx_g, lnx_b, w_out_bf16, ln2_g, router_t, bd):
    b, t, _ = x.shape
    nt = t // ROW_TILE
    row2 = lambda bi, i: (0, 0)
    tok = lambda w: pl.BlockSpec((None, ROW_TILE, w), lambda bi, i: (bi, i, 0))
    grp = lambda dil: pl.BlockSpec((None, dil, ROW_TILE // dil, GROUP_WIDTH), lambda bi, i: (bi, 0, i, 0))
    vec = lambda n: pl.BlockSpec((1, n), row2)
    grp_specs, grp_args = [], []
    for (o, lse), dil in zip(att_parts, DILATIONS):
        grp_specs += [grp(dil), grp(dil)]
        grp_args += [o, lse]
    return pl.pallas_call(
        _out_proj_kernel,
        grid=(b, nt),
        in_specs=[tok(D_MODEL)] + grp_specs + [
            pl.BlockSpec((N_DIR, None, ROW_TILE, RW_WIDTH), lambda bi, i: (0, bi, i, 0)),
            tok(RW_WIDTH), tok(RW_WIDTH), vec(RW_WIDTH), vec(RW_WIDTH),
            pl.BlockSpec((GROUP_WIDTH + RW_WIDTH, D_MODEL), row2),
            vec(D_MODEL),
            pl.BlockSpec((N_EXPERTS, D_MODEL), row2),
            pl.BlockSpec((GROUP_WIDTH, GROUP_WIDTH), row2),
        ],
        out_specs=[tok(D_MODEL), tok(D_MODEL),
                   pl.BlockSpec((N_EXPERTS, ROW_TILE), lambda bi, i: (0, bi * nt + i))],
        out_shape=[jax.ShapeDtypeStruct((b, t, D_MODEL), jnp.float32),
                   jax.ShapeDtypeStruct((b, t, D_MODEL), jnp.bfloat16),
                   jax.ShapeDtypeStruct((N_EXPERTS, b * t), jnp.float32)],
        scratch_shapes=[pltpu.VMEM((GROUP_WIDTH // LANES, ROW_TILE, LANES), jnp.float32)] * 4,
        compiler_params=_cparams(("parallel", "parallel")),
        name="out_proj",
    )(x, *grp_args, y, bonus, gate, lnx_g, lnx_b, w_out_bf16, ln2_g, router_t, bd)


FF_CHUNK = 512


def _expert_ffn_kernel(x_ref, gate_ref, wg_ref, wu_ref, wd_ref, o_ref):
    x = x_ref[...]
    acc = jnp.zeros(o_ref.shape, jnp.float32)
    for f0 in range(0, EXPERT_FF, FF_CHUNK):
        hg = jnp.dot(x, wg_ref[:, f0:f0 + FF_CHUNK], preferred_element_type=jnp.float32)
        hu = jnp.dot(x, wu_ref[:, f0:f0 + FF_CHUNK], preferred_element_type=jnp.float32)
        hid = (hg * _sigmoid(hg) * hu).astype(jnp.bfloat16)
        acc = acc + jnp.dot(hid, wd_ref[f0:f0 + FF_CHUNK, :], preferred_element_type=jnp.float32)
    o_ref[...] = acc * gate_ref[...]


def _expert_ffn(xs, gate, wg, wu, wd):
    e, cap, _ = xs.shape
    tile = min(FFN_ROW_TILE, cap)
    tok = lambda w: pl.BlockSpec((None, tile, w), lambda ei, i: (ei, i, 0))
    wspec = lambda r, c: pl.BlockSpec((None, r, c), lambda ei, i: (ei, 0, 0))
    return pl.pallas_call(
        _expert_ffn_kernel,
        grid=(e, cap // tile),
        in_specs=[tok(D_MODEL), tok(1), wspec(D_MODEL, EXPERT_FF), wspec(D_MODEL, EXPERT_FF), wspec(EXPERT_FF, D_MODEL)],
        out_specs=tok(D_MODEL),
        out_shape=jax.ShapeDtypeStruct((e, cap, D_MODEL), jnp.float32),
        compiler_params=_cparams(("parallel", "arbitrary")),
        name="expert_ffn",
    )(xs, gate, wg, wu, wd)


def _pad_lora(w):
    z = jnp.zeros((N_DIR, N_DIR * w.shape[1], w.shape[2]), w.dtype)
    for d in range(N_DIR):
        z = z.at[d, d * w.shape[1]:(d + 1) * w.shape[1]].set(w[d])
    return z.astype(jnp.bfloat16)


def _layer_params(l, ln1_g, ln2_g, w_in, mu_prev, mu_next, q_norm_g, k_norm_g, w0, w2, a0, a2, g2, k_k, k_a,
                  r_k, lnx_g, lnx_b, w_out, router, w_gate, w_up, w_down):
    bf = lambda z: z.astype(jnp.bfloat16)
    row = lambda z: z.reshape(1, -1)
    return dict(
        ln1_g=ln1_g[l], ln2_g=row(ln2_g[l]), w_in=bf(w_in[l]), mu_prev=row(mu_prev[l]), mu_next=row(mu_next[l]),
        q_gain=row(jnp.tile(q_norm_g[l], ATT_SLOTS)), k_gain=row(jnp.tile(k_norm_g[l], ATT_SLOTS)),
        w0=w0[l], w2=_pad_lora(w2[l]), a0=a0[l], a2=_pad_lora(a2[l]), g2=bf(g2[l]),
        k_k=row(k_k[l]), k_a=row(k_a[l]), r_k=row(r_k[l]), lnx_g=row(lnx_g[l]), lnx_b=row(lnx_b[l]),
        w_out=bf(w_out[l]), router_t=router[l].T, w_gate=bf(w_gate[l]), w_up=bf(w_up[l]), w_down=bf(w_down[l]),
    )


def _encoder_layer(x, p, rope, bd):
    b, t, _ = x.shape
    n = b * t
    q, k, v, s = _in_proj(x, p['ln1_g'], p['w_in'], p['q_gain'], p['k_gain'], rope, bd)
    att_parts = [_band_attn(q[g], k[g], v[g]) for g in range(N_DIL)]
    lhs, add, bonus, gate = _rwkv_prep(
        s, p['mu_prev'], p['mu_next'], p['w0'], p['w2'], p['a0'], p['a2'], p['g2'], p['k_k'], p['k_a'], p['r_k'], bd)
    y = _rwkv_scan(lhs, add)
    xn, h2, aff_t = _out_proj(x, att_parts, y, bonus, gate, p['lnx_g'], p['lnx_b'], p['w_out'], p['ln2_g'],
                              p['router_t'], bd)
    cap = max(1, (EC_CAPACITY_FACTOR * n) // N_EXPERTS)
    gate_e, idx = lax.top_k(aff_t, cap)
    xs = h2.reshape(n, D_MODEL)[idx]
    ye = _expert_ffn(xs, gate_e[..., None], p['w_gate'], p['w_up'], p['w_down'])
    moe = jnp.zeros((n, D_MODEL), jnp.float32).at[idx.reshape(-1)].add(ye.reshape(-1, D_MODEL))
    return xn + moe.reshape(b, t, D_MODEL)


def kernel(x_prompt, x_sample, ln1_g, ln2_g, w_in, mu_prev, mu_next, q_norm_g, k_norm_g, w0, w2, a0, a2, g2,
           k_k, k_a, r_k, lnx_g, lnx_b, w_out, router, w_gate, w_up, w_down):
    depth = w_in.shape[0]
    bd = _head_sum_matrix()
    rope_p = _rope_tables(x_prompt.shape[1])
    rope_s = _rope_tables(x_sample.shape[1])
    y_prompt, y_sample = x_prompt, x_sample
    for l in range(depth):
        p = _layer_params(l, ln1_g, ln2_g, w_in, mu_prev, mu_next, q_norm_g, k_norm_g, w0, w2, a0, a2, g2, k_k,
                          k_a, r_k, lnx_g, lnx_b, w_out, router, w_gate, w_up, w_down)
        y_prompt = _encoder_layer(y_prompt, p, rope_p, bd)
        y_sample = _encoder_layer(y_sample, p, rope_s, bd)
    return (y_prompt, y_sample)
```

```python
import functools

import jax
import jax.numpy as jnp
import numpy as np
from jax import lax
from jax.experimental import pallas as pl
from jax.experimental.pallas import tpu as pltpu

D_MODEL = 1024
RW_HEADS = 4
HEAD_DIM = 64
RW_WIDTH = RW_HEADS * HEAD_DIM
DECAY_LORA = 64
AAA_LORA = 64
GATE_LORA = 128
N_DIR = 2
GN_EPS = 64e-5
ATT_SLOTS = 4
DILATIONS = (1, 4, 16)
BAND_HALF = 64
N_DIL = len(DILATIONS)
GROUP_WIDTH = ATT_SLOTS * HEAD_DIM
ATT_WIDTH = N_DIL * GROUP_WIDTH
ATT_COLS = 3 * ATT_WIDTH
SHIFT_COLS = 3 * RW_WIDTH + N_DIR * DECAY_LORA + N_DIR * AAA_LORA + GATE_LORA
IN_COLS = ATT_COLS + SHIFT_COLS
ROPE_THETA = 500000.0
ROPE_DIMS = HEAD_DIM // 4
ROPE_HALF = ROPE_DIMS // 2
N_EXPERTS = 16
EC_CAPACITY_FACTOR = 2
EXPERT_FF = 2048
NORM_EPS = 1e-6
NEG_INF = -1e30
LANES = 128

ROW_TILE = 512
ATT_Q_BLOCK = 128
CHUNK = 64
CHUNKS_PER_ITER = 4
FFN_ROW_TILE = 512
VMEM_LIMIT = 56 * 1024 * 1024

HIGHEST = lax.Precision.HIGHEST


def _cparams(sem):
    return pltpu.CompilerParams(dimension_semantics=sem, vmem_limit_bytes=VMEM_LIMIT)


def _head_sum_matrix():
    h = np.arange(GROUP_WIDTH) // HEAD_DIM
    return jnp.asarray((h[:, None] == h[None, :]).astype(np.float32), dtype=jnp.bfloat16)


def _head_sum(x, bd):
    hi = x.astype(jnp.bfloat16)
    lo = (x - hi.astype(jnp.float32)).astype(jnp.bfloat16)
    return (jnp.dot(hi, bd, preferred_element_type=jnp.float32)
            + jnp.dot(lo, bd, preferred_element_type=jnp.float32))


def _rope_tables(t):
    inv = jnp.power(jnp.float32(ROPE_THETA), -jnp.arange(ROPE_HALF, dtype=jnp.float32) * 2.0 / ROPE_DIMS)
    ang = jnp.arange(t, dtype=jnp.float32)[:, None] * inv[None, :]
    cos, sin = jnp.cos(ang), jnp.sin(ang)
    zeros = jnp.zeros((t, HEAD_DIM - ROPE_DIMS), jnp.float32)
    zero8 = jnp.zeros((t, ROPE_HALF), jnp.float32)
    c = jnp.concatenate([cos, cos, zeros + 1.0], -1)
    s1 = jnp.concatenate([zero8, sin, zeros], -1)
    s2 = jnp.concatenate([-sin, zero8, zeros], -1)
    tile = lambda z: jnp.tile(z, (1, ATT_SLOTS))
    return tile(c), tile(s1), tile(s2)


def _in_proj_kernel(x_ref, g_ref, w_ref, qg_ref, kg_ref, c_ref, s1_ref, s2_ref, bd_ref,
                    q0_ref, q1_ref, q2_ref, k0_ref, k1_ref, k2_ref, v0_ref, v1_ref, v2_ref, s_ref,
                    tmp_ref):
    x = x_ref[...]
    h = x * lax.rsqrt(jnp.mean(x * x, -1, keepdims=True) + NORM_EPS) * g_ref[...]
    hb = h.astype(jnp.bfloat16)
    bd = bd_ref[...]
    c, s1, s2 = c_ref[...], s1_ref[...], s2_ref[...]
    rows = x.shape[0]

    def deinterleave(val, out_ref, dil):
        if dil == 1:
            out_ref[0] = val.astype(out_ref.dtype)
            return
        for hf in range(GROUP_WIDTH // LANES):
            tmp_ref[hf] = val[:, hf * LANES:(hf + 1) * LANES]
        for r in range(dil):
            for hf in range(GROUP_WIDTH // LANES):
                out_ref[r, :, hf * LANES:(hf + 1) * LANES] = (
                    tmp_ref[hf, pl.ds(r, rows // dil, stride=dil), :].astype(out_ref.dtype))

    def normed_rope(col0, gain, scale):
        p = jnp.dot(hb, w_ref[:, col0:col0 + GROUP_WIDTH], preferred_element_type=jnp.float32)
        ms = _head_sum(p * p, bd) * (1.0 / HEAD_DIM)
        n = p * lax.rsqrt(ms + NORM_EPS) * gain
        n = n * c + pltpu.roll(n, ROPE_HALF, 1) * s1 + pltpu.roll(n, GROUP_WIDTH - ROPE_HALF, 1) * s2
        return n * scale if scale != 1.0 else n

    q_refs = (q0_ref, q1_ref, q2_ref)
    k_refs = (k0_ref, k1_ref, k2_ref)
    v_refs = (v0_ref, v1_ref, v2_ref)
    for g, dil in enumerate(DILATIONS):
        deinterleave(normed_rope(g * GROUP_WIDTH, qg_ref[...], HEAD_DIM ** -0.5), q_refs[g], dil)
        deinterleave(normed_rope(ATT_WIDTH + g * GROUP_WIDTH, kg_ref[...], 1.0), k_refs[g], dil)
        col0 = 2 * ATT_WIDTH + g * GROUP_WIDTH
        v = jnp.dot(hb, w_ref[:, col0:col0 + GROUP_WIDTH], preferred_element_type=jnp.float32)
        deinterleave(v, v_refs[g], dil)
    s_ref[...] = jnp.dot(hb, w_ref[:, ATT_COLS:], preferred_element_type=jnp.float32)


def _in_proj(x, ln1_g, w_in_bf16, q_gain, k_gain, rope, bd):
    b, t, _ = x.shape
    nt = t // ROW_TILE
    row2 = lambda bi, i: (0, 0)
    grp_shapes, grp_specs = [], []
    for _ in range(3):
        for dil in DILATIONS:
            grp_shapes.append(jax.ShapeDtypeStruct((b, dil, t // dil, GROUP_WIDTH), jnp.bfloat16))
            grp_specs.append(pl.BlockSpec((None, dil, ROW_TILE // dil, GROUP_WIDTH), lambda bi, i: (bi, 0, i, 0)))
    tab_spec = pl.BlockSpec((ROW_TILE, GROUP_WIDTH), lambda bi, i: (i, 0))
    outs = pl.pallas_call(
        _in_proj_kernel,
        grid=(b, nt),
        in_specs=[
            pl.BlockSpec((None, ROW_TILE, D_MODEL), lambda bi, i: (bi, i, 0)),
            pl.BlockSpec((1, D_MODEL), row2),
            pl.BlockSpec((D_MODEL, IN_COLS), row2),
            pl.BlockSpec((1, GROUP_WIDTH), row2),
            pl.BlockSpec((1, GROUP_WIDTH), row2),
            tab_spec, tab_spec, tab_spec,
            pl.BlockSpec((GROUP_WIDTH, GROUP_WIDTH), row2),
        ],
        out_specs=grp_specs + [pl.BlockSpec((None, ROW_TILE, SHIFT_COLS), lambda bi, i: (bi, i, 0))],
        out_shape=grp_shapes + [jax.ShapeDtypeStruct((b, t, SHIFT_COLS), jnp.float32)],
        scratch_shapes=[pltpu.VMEM((GROUP_WIDTH // LANES, ROW_TILE, LANES), jnp.float32)],
        compiler_params=_cparams(("parallel", "parallel")),
        name="in_proj",
    )(x, ln1_g.reshape(1, D_MODEL), w_in_bf16, q_gain, k_gain, *rope, bd)
    q = outs[0:3]
    k = outs[3:6]
    v = outs[6:9]
    return q, k, v, outs[9]


def _band_attn_kernel(q_ref, kp_ref, kc_ref, kn_ref, vp_ref, vc_ref, vn_ref, o_ref, lse_ref, *, seq_len, q_chunk):
    i = pl.program_id(1)
    kext = jnp.concatenate([kp_ref[...], kc_ref[...], kn_ref[...]], axis=0)
    vext = jnp.concatenate([vp_ref[...], vc_ref[...], vn_ref[...]], axis=0)
    kw = ATT_Q_BLOCK + 2 * BAND_HALF
    lane_head = lax.broadcasted_iota(jnp.int32, (ATT_Q_BLOCK, GROUP_WIDTH), 1) // HEAD_DIM
    qi = lax.broadcasted_iota(jnp.int32, (ATT_Q_BLOCK, kw), 0)
    kc = lax.broadcasted_iota(jnp.int32, (ATT_Q_BLOCK, kw), 1)
    diff = kc - BAND_HALF - qi
    band = (diff <= BAND_HALF) & (diff >= -BAND_HALF)
    for a in range(0, q_chunk, ATT_Q_BLOCK):
        kpos = i * q_chunk + (a - BAND_HALF) + kc
        valid = band & (kpos >= 0) & (kpos < seq_len)
        qb = q_ref[a:a + ATT_Q_BLOCK, :]
        kb = kext[a:a + kw]
        vb = vext[a:a + kw]
        heads = range(ATT_SLOTS)
        mine = [lane_head == h for h in heads]
        s = [lax.dot_general(jnp.where(mine[h], qb, jnp.zeros_like(qb)), kb, (((1,), (1,)), ((), ())),
                             preferred_element_type=jnp.float32) for h in heads]
        s = [jnp.where(valid, s_h, NEG_INF) for s_h in s]
        m = [jnp.max(s_h, -1, keepdims=True) for s_h in s]
        p = [jnp.exp(s_h - m_h) for s_h, m_h in zip(s, m)]
        den = [jnp.sum(p_h, -1, keepdims=True) for p_h in p]
        pv = [jnp.dot(p_h.astype(jnp.bfloat16), vb, preferred_element_type=jnp.float32) for p_h in p]
        acc = jnp.zeros((ATT_Q_BLOCK, GROUP_WIDTH), jnp.float32)
        lse = jnp.zeros((ATT_Q_BLOCK, GROUP_WIDTH), jnp.float32)
        for h in heads:
            acc = jnp.where(mine[h], pv[h] / den[h], acc)
            lse = jnp.where(mine[h], m[h] + jnp.log(den[h]), lse)
        o_ref[a:a + ATT_Q_BLOCK, :] = acc
        lse_ref[a:a + ATT_Q_BLOCK, :] = lse


def _band_attn(q, k, v):
    b, dil, seq_len, _ = q.shape
    ns = b * dil
    q, k, v = (z.reshape(ns, seq_len, GROUP_WIDTH) for z in (q, k, v))
    q_chunk = min(seq_len, 512)
    halo_per_chunk = q_chunk // BAND_HALF
    last_halo = seq_len // BAND_HALF - 1
    cur = pl.BlockSpec((None, q_chunk, GROUP_WIDTH), lambda s, i: (s, i, 0))
    prev = pl.BlockSpec((None, BAND_HALF, GROUP_WIDTH), lambda s, i: (s, jnp.maximum(i * halo_per_chunk - 1, 0), 0))
    nxt = pl.BlockSpec((None, BAND_HALF, GROUP_WIDTH),
                       lambda s, i: (s, jnp.minimum((i + 1) * halo_per_chunk, last_halo), 0))
    o, lse = pl.pallas_call(
        functools.partial(_band_attn_kernel, seq_len=seq_len, q_chunk=q_chunk),
        grid=(ns, seq_len // q_chunk),
        in_specs=[cur, prev, cur, nxt, prev, cur, nxt],
        out_specs=[cur, cur],
        out_shape=[jax.ShapeDtypeStruct((ns, seq_len, GROUP_WIDTH), jnp.float32)] * 2,
        compiler_params=_cparams(("parallel", "parallel")),
        name="band_attn",
    )(q, k, k, k, v, v, v)
    shape4 = (b, dil, seq_len, GROUP_WIDTH)
    return o.reshape(shape4), lse.reshape(shape4)


def _softplus(x):
    return jnp.maximum(x, 0.0) + jnp.log1p(jnp.exp(-jnp.abs(x)))


def _sigmoid(x):
    return 1.0 / (1.0 + jnp.exp(-x))


def _rwkv_prep_kernel(sp_ref, sc_ref, sn_ref, mup_ref, mun_ref, w0_ref, w2_ref, a0_ref, a2_ref, g2_ref,
                      kk_ref, ka_ref, rk_ref, bd_ref,
                      lhs_ref, add_ref, bonus_ref, gate_ref,
                      r_s, v_s, nkk_s, e_s, kd_s, beta_s):
    i = pl.program_id(1)
    last = pl.num_programs(1) - 1
    p = sc_ref[...]
    rows = p.shape[0]
    row = lax.broadcasted_iota(jnp.int32, p.shape, 0)
    halo_prev = sp_ref[7:8, :] * (i > 0).astype(jnp.float32)
    halo_next = sn_ref[0:1, :] * (i < last).astype(jnp.float32)
    p_prev = jnp.where(row == 0, halo_prev, pltpu.roll(p, 1, 0))
    p_next = jnp.where(row == rows - 1, halo_next, pltpu.roll(p, rows - 1, 0))
    u = p + mup_ref[...] * (p_prev - p) + mun_ref[...] * (p_next - p)

    bd = bd_ref[...]
    r = u[:, 0:RW_WIDTH]
    k = u[:, RW_WIDTH:2 * RW_WIDTH]
    v = u[:, 2 * RW_WIDTH:3 * RW_WIDTH]
    c0 = 3 * RW_WIDTH
    lw = jnp.tanh(u[:, c0:c0 + 128]).astype(jnp.bfloat16)
    la = u[:, c0 + 128:c0 + 256].astype(jnp.bfloat16)
    lg = _sigmoid(u[:, c0 + 256:c0 + 384]).astype(jnp.bfloat16)

    kk = k * kk_ref[...]
    kk = kk * lax.rsqrt(_head_sum(kk * kk, bd) + 1e-12)
    r_s[...] = r
    v_s[...] = v
    nkk_s[...] = -kk
    gate_ref[...] = jnp.dot(lg, g2_ref[...], preferred_element_type=jnp.float32)
    bonus = jnp.zeros_like(v)
    for d in range(N_DIR):
        z = w0_ref[d:d + 1, :] + jnp.dot(lw, w2_ref[d], preferred_element_type=jnp.float32)
        w_log = -_softplus(-z) - 0.5
        e_s[d] = jnp.exp(w_log)
        a = _sigmoid(a0_ref[d:d + 1, :] + jnp.dot(la, a2_ref[d], preferred_element_type=jnp.float32))
        kd = k * (1.0 + (a - 1.0) * ka_ref[...])
        kd_s[d] = kd
        beta_s[d] = kk * a
        bonus = bonus + _head_sum(r * kd * rk_ref[...], bd) * v
    bonus_ref[...] = bonus

    idx_t = lax.broadcasted_iota(jnp.int32, (CHUNK, CHUNK), 0)
    idx_s = lax.broadcasted_iota(jnp.int32, (CHUNK, CHUNK), 1)
    eye = (idx_t == idx_s).astype(jnp.float32)
    bf = lambda z: z.astype(jnp.bfloat16)
    mm = lambda a_, b_: jnp.dot(bf(a_), bf(b_), preferred_element_type=jnp.float32)
    mm_nt = lambda a_, b_: lax.dot_general(bf(a_), bf(b_), (((1,), (1,)), ((), ())), preferred_element_type=jnp.float32)
    mm_tn = lambda a_, b_: lax.dot_general(bf(a_), bf(b_), (((0,), (0,)), ((), ())), preferred_element_type=jnp.float32)

    def chunk_body(cp, carry):
        units = []
        for c, d in [(cp * CHUNKS_PER_ITER + j, d_) for j in range(CHUNKS_PER_ITER) for d_ in range(N_DIR)]:
            rows_c = pl.ds(pl.multiple_of(c * CHUNK, CHUNK), CHUNK)
            r_c, v_c, nkk_c = r_s[rows_c, :], v_s[rows_c, :], nkk_s[rows_c, :]
            before = (idx_s < idx_t) if d == 0 else (idx_s > idx_t)
            before_eq = (idx_s <= idx_t) if d == 0 else (idx_s >= idx_t)
            last = CHUNK - 1 if d == 0 else 0
            e_c = e_s[d, rows_c, :]
            cl = jnp.dot(before_eq.astype(jnp.float32), e_c, precision=HIGHEST, preferred_element_type=jnp.float32)
            w_inc = jnp.exp(-cl)
            w_exc = jnp.exp(e_c - cl)
            w_inv = jnp.exp(cl)
            w_tot = w_inc[last:last + 1, :]
            a_t = nkk_c * w_exc
            r_t = r_c * w_inc
            b_h = beta_s[d, rows_c, :] * w_inv
            k_h = kd_s[d, rows_c, :] * w_inv
            b_t = b_h * w_tot
            k_t = k_h * w_tot
            for h in range(RW_HEADS):
                hs = slice(h * HEAD_DIM, (h + 1) * HEAD_DIM)
                units.append(dict(chunk=c, dir=d, before=before, before_eq=before_eq, a_t=a_t[:, hs], r_t=r_t[:, hs], b_h=b_h[:, hs],
                                  k_h=k_h[:, hs], b_t=b_t[:, hs], k_t=k_t[:, hs], w_tot=w_tot[:, hs], v=v_c[:, hs]))
        each = lambda fn: [fn(un) for un in units]
        g = each(lambda un: mm_nt(jnp.concatenate([un['a_t'], un['r_t']], axis=0),
                                  jnp.concatenate([un['b_h'], un['k_h']], axis=0)))
        for un, g_u in zip(units, g):
            un['m_ab'] = jnp.where(un['before'], g_u[:CHUNK, :CHUNK], 0.0)
            un['m_ak'] = jnp.where(un['before'], g_u[:CHUNK, CHUNK:], 0.0)
            un['n_rb'] = jnp.where(un['before_eq'], g_u[CHUNK:, :CHUNK], 0.0)
            un['n_rk'] = jnp.where(un['before_eq'], g_u[CHUNK:, CHUNK:], 0.0)
        t_inv = each(lambda un: eye + un['m_ab'])
        pw = each(lambda un: un['m_ab'])
        for _ in range(CHUNK.bit_length() - 2):
            pw = [mm(p_, p_) for p_ in pw]
            t_inv = [t_ + mm(t_, p_) for t_, p_ in zip(t_inv, pw)]
        mv = each(lambda un: mm(un['m_ak'], un['v']))
        nv = each(lambda un: mm(un['n_rk'], un['v']))
        kv = each(lambda un: mm_tn(un['k_t'], un['v']))
        tx = [mm(t_, jnp.concatenate([un['a_t'], mv_], axis=1)) for t_, un, mv_ in zip(t_inv, units, mv)]
        nx = [mm(un['n_rb'], tx_) for un, tx_ in zip(units, tx)]
        bx = [mm_tn(un['b_t'], tx_) for un, tx_ in zip(units, tx)]
        lhs_parts, add_parts = [], []
        for un, nx_, bx_, nv_, kv_ in zip(units, nx, bx, nv, kv):
            ra = un['r_t'] + nx_[:, :HEAD_DIM]
            y0 = nx_[:, HEAD_DIM:] + nv_
            phi = eye * un['w_tot'] + bx_[:, :HEAD_DIM]
            psi = bx_[:, HEAD_DIM:] + kv_
            lhs_parts.append(jnp.concatenate([ra, phi], axis=0))
            add_parts.append(jnp.concatenate([y0, psi], axis=0))
        for g0 in range(0, len(units), RW_HEADS):
            c, d = units[g0]['chunk'], units[g0]['dir']
            lhs_ref[d, c] = jnp.concatenate(lhs_parts[g0:g0 + RW_HEADS], axis=1)
            add_ref[d, c] = jnp.concatenate(add_parts[g0:g0 + RW_HEADS], axis=1)
        return carry

    lax.fori_loop(0, rows // (CHUNK * CHUNKS_PER_ITER), chunk_body, 0)


def _rwkv_prep(s, mu_prev, mu_next, w0, w2p, a0, a2p, g2, k_k, k_a, r_k, bd):
    b, t, _ = s.shape
    nt = t // ROW_TILE
    halo_per_tile = ROW_TILE // 8
    last_halo = t // 8 - 1
    row2 = lambda bi, i: (0, 0)
    row3 = lambda bi, i: (0, 0, 0)
    vec = lambda n: pl.BlockSpec((1, n), row2)
    chunks_per_tile = ROW_TILE // CHUNK
    cspec = pl.BlockSpec((N_DIR, None, chunks_per_tile, 2 * CHUNK, RW_WIDTH), lambda bi, i: (0, bi, i, 0, 0))
    nspec = pl.BlockSpec((None, ROW_TILE, RW_WIDTH), lambda bi, i: (bi, i, 0))
    cshape = jax.ShapeDtypeStruct((N_DIR, b, t // CHUNK, 2 * CHUNK, RW_WIDTH), jnp.float32)
    nshape = jax.ShapeDtypeStruct((b, t, RW_WIDTH), jnp.float32)
    tile_f32 = pltpu.VMEM((ROW_TILE, RW_WIDTH), jnp.float32)
    tile2_f32 = pltpu.VMEM((N_DIR, ROW_TILE, RW_WIDTH), jnp.float32)
    return pl.pallas_call(
        _rwkv_prep_kernel,
        grid=(b, nt),
        in_specs=[
            pl.BlockSpec((None, 8, SHIFT_COLS), lambda bi, i: (bi, jnp.maximum(i * halo_per_tile - 1, 0), 0)),
            pl.BlockSpec((None, ROW_TILE, SHIFT_COLS), lambda bi, i: (bi, i, 0)),
            pl.BlockSpec((None, 8, SHIFT_COLS), lambda bi, i: (bi, jnp.minimum((i + 1) * halo_per_tile, last_halo), 0)),
            vec(SHIFT_COLS), vec(SHIFT_COLS),
            pl.BlockSpec((N_DIR, RW_WIDTH), row2),
            pl.BlockSpec((N_DIR, 128, RW_WIDTH), row3),
            pl.BlockSpec((N_DIR, RW_WIDTH), row2),
            pl.BlockSpec((N_DIR, 128, RW_WIDTH), row3),
            pl.BlockSpec((GATE_LORA, RW_WIDTH), row2),
            vec(RW_WIDTH), vec(RW_WIDTH), vec(RW_WIDTH),
            pl.BlockSpec((GROUP_WIDTH, GROUP_WIDTH), row2),
        ],
        out_specs=[cspec, cspec, nspec, nspec],
        out_shape=[cshape, cshape, nshape, nshape],
        scratch_shapes=[tile_f32, tile_f32, tile_f32, tile2_f32, tile2_f32, tile2_f32],
        compiler_params=_cparams(("parallel", "parallel")),
        name="rwkv_prep",
    )(s, s, s, mu_prev, mu_next, w0, w2p, a0, a2p, g2, k_k, k_a, r_k, bd)


def _split_bf16(x):
    hi = x.astype(jnp.bfloat16)
    return hi, (x - hi.astype(jnp.float32)).astype(jnp.bfloat16)


def _rwkv_scan_kernel(lhs_ref, add_ref, y_ref, state_ref):
    d = pl.program_id(0)
    n_chunks = lhs_ref.shape[0]

    @pl.when(pl.program_id(2) == 0)
    def _():
        state_ref[...] = jnp.zeros_like(state_ref)

    def chunk_body(cc, carry):
        c = cc + d * (n_chunks - 1 - 2 * cc)
        y_heads = []
        for h in range(RW_HEADS):
            hs = slice(h * HEAD_DIM, (h + 1) * HEAD_DIM)
            l_hi, l_lo = _split_bf16(lhs_ref[c, :, hs])
            s_hi, s_lo = _split_bf16(state_ref[h])
            dot = lambda a_, b_: jnp.dot(a_, b_, preferred_element_type=jnp.float32)
            res = dot(l_hi, s_hi) + dot(l_lo, s_hi) + dot(l_hi, s_lo) + add_ref[c, :, hs]
            y_heads.append(res[:CHUNK])
            state_ref[h] = res[CHUNK:]
        y_ref[pl.ds(pl.multiple_of(c * CHUNK, CHUNK), CHUNK), :] = jnp.concatenate(y_heads, axis=1)
        return carry

    lax.fori_loop(0, n_chunks, chunk_body, 0)


def _rwkv_scan(lhs, add):
    _, b, nchunk, _, _ = lhs.shape
    per_step = ROW_TILE // CHUNK
    nc = nchunk // per_step
    blk = lambda d, c: c + d * (nc - 1 - 2 * c)
    cspec = pl.BlockSpec((None, None, per_step, 2 * CHUNK, RW_WIDTH), lambda d, bi, c: (d, bi, blk(d, c), 0, 0))
    return pl.pallas_call(
        _rwkv_scan_kernel,
        grid=(N_DIR, b, nc),
        in_specs=[cspec, cspec],
        out_specs=pl.BlockSpec((None, None, ROW_TILE, RW_WIDTH), lambda d, bi, c: (d, bi, blk(d, c), 0)),
        out_shape=jax.ShapeDtypeStruct((N_DIR, b, nchunk * CHUNK, RW_WIDTH), jnp.float32),
        scratch_shapes=[pltpu.VMEM((RW_HEADS, HEAD_DIM, HEAD_DIM), jnp.float32)],
        compiler_params=_cparams(("parallel", "parallel", "arbitrary")),
        name="rwkv_scan",
    )(lhs, add)


def _out_proj_kernel(x_ref, o0_ref, l0_ref, o1_ref, l1_ref, o2_ref, l2_ref, y_ref, bonus_ref, gate_ref,
                     lnxg_ref, lnxb_ref, wout_ref, ln2_ref, router_ref, bd_ref,
                     xn_ref, h2_ref, aff_ref, so1_ref, sl1_ref, so2_ref, sl2_ref):
    rows = x_ref.shape[0]

    def interleave(src_ref, dst_ref, dil):
        halves = range(GROUP_WIDTH // LANES)
        for r in range(dil):
            for hf in halves:
                dst_ref[hf, pl.ds(r, rows // dil, stride=dil), :] = src_ref[r, :, hf * LANES:(hf + 1) * LANES]
        return jnp.concatenate([dst_ref[hf] for hf in halves], axis=-1)

    o1 = interleave(o1_ref, so1_ref, DILATIONS[1])
    l1 = interleave(l1_ref, sl1_ref, DILATIONS[1])
    o2 = interleave(o2_ref, so2_ref, DILATIONS[2])
    l2 = interleave(l2_ref, sl2_ref, DILATIONS[2])
    l0 = l0_ref[0]
    m = jnp.maximum(jnp.maximum(l0, l1), l2)
    e0, e1, e2 = jnp.exp(l0 - m), jnp.exp(l1 - m), jnp.exp(l2 - m)
    att = (e0 * o0_ref[0] + e1 * o1 + e2 * o2) / (e0 + e1 + e2)

    bd = bd_ref[...]
    y = y_ref[0] + y_ref[1]
    mu = _head_sum(y, bd) * (1.0 / HEAD_DIM)
    dlt = y - mu
    var = _head_sum(dlt * dlt, bd) * (1.0 / HEAD_DIM)
    yn = dlt * lax.rsqrt(var + GN_EPS) * lnxg_ref[...] + lnxb_ref[...]
    rw = (yn + bonus_ref[...]) * gate_ref[...]

    mixed = (jnp.dot(att.astype(jnp.bfloat16), wout_ref[0:GROUP_WIDTH, :], preferred_element_type=jnp.float32)
             + jnp.dot(rw.astype(jnp.bfloat16), wout_ref[GROUP_WIDTH:, :], preferred_element_type=jnp.float32))
    xn = x_ref[...] + mixed
    xn_ref[...] = xn
    h2 = xn * lax.rsqrt(jnp.mean(xn * xn, -1, keepdims=True) + NORM_EPS) * ln2_ref[...]
    h2_ref[...] = h2.astype(h2_ref.dtype)
    logits = lax.dot_general(router_ref[...], h2, (((1,), (1,)), ((), ())),
                             precision=HIGHEST, preferred_element_type=jnp.float32)
    ex = jnp.exp(logits - jnp.max(logits, 0, keepdims=True))
    aff_ref[...] = ex / jnp.sum(ex, 0, keepdims=True)


def _out_proj(x, att_parts, y, bonus, gate, lnx_g, lnx_b, w_out_bf16, ln2_g, router_t, bd):
    b, t, _ = x.shape
    nt = t // ROW_TILE
    row2 = lambda bi, i: (0, 0)
    tok = lambda w: pl.BlockSpec((None, ROW_TILE, w), lambda bi, i: (bi, i, 0))
    grp = lambda dil: pl.BlockSpec((None, dil, ROW_TILE // dil, GROUP_WIDTH), lambda bi, i: (bi, 0, i, 0))
    vec = lambda n: pl.BlockSpec((1, n), row2)
    grp_specs, grp_args = [], []
    for (o, lse), dil in zip(att_parts, DILATIONS):
        grp_specs += [grp(dil), grp(dil)]
        grp_args += [o, lse]
    return pl.pallas_call(
        _out_proj_kernel,
        grid=(b, nt),
        in_specs=[tok(D_MODEL)] + grp_specs + [
            pl.BlockSpec((N_DIR, None, ROW_TILE, RW_WIDTH), lambda bi, i: (0, bi, i, 0)),
            tok(RW_WIDTH), tok(RW_WIDTH), vec(RW_WIDTH), vec(RW_WIDTH),
            pl.BlockSpec((GROUP_WIDTH + RW_WIDTH, D_MODEL), row2),
            vec(D_MODEL),
            pl.BlockSpec((N_EXPERTS, D_MODEL), row2),
            pl.BlockSpec((GROUP_WIDTH, GROUP_WIDTH), row2),
        ],
        out_specs=[tok(D_MODEL), tok(D_MODEL),
                   pl.BlockSpec((N_EXPERTS, ROW_TILE), lambda bi, i: (0, bi * nt + i))],
        out_shape=[jax.ShapeDtypeStruct((b, t, D_MODEL), jnp.float32),
                   jax.ShapeDtypeStruct((b, t, D_MODEL), jnp.bfloat16),
                   jax.ShapeDtypeStruct((N_EXPERTS, b * t), jnp.float32)],
        scratch_shapes=[pltpu.VMEM((GROUP_WIDTH // LANES, ROW_TILE, LANES), jnp.float32)] * 4,
        compiler_params=_cparams(("parallel", "parallel")),
        name="out_proj",
    )(x, *grp_args, y, bonus, gate, lnx_g, lnx_b, w_out_bf16, ln2_g, router_t, bd)


FF_CHUNK = 512


def _expert_ffn_kernel(x_ref, gate_ref, wg_ref, wu_ref, wd_ref, o_ref):
    x = x_ref[...]
    acc = jnp.zeros(o_ref.shape, jnp.float32)
    for f0 in range(0, EXPERT_FF, FF_CHUNK):
        hg = jnp.dot(x, wg_ref[:, f0:f0 + FF_CHUNK], preferred_element_type=jnp.float32)
        hu = jnp.dot(x, wu_ref[:, f0:f0 + FF_CHUNK], preferred_element_type=jnp.float32)
        hid = (hg * _sigmoid(hg) * hu).astype(jnp.bfloat16)
        acc = acc + jnp.dot(hid, wd_ref[f0:f0 + FF_CHUNK, :], preferred_element_type=jnp.float32)
    o_ref[...] = (acc * gate_ref[...]).astype(o_ref.dtype)


def _expert_ffn(xs, gate, wg, wu, wd):
    e, cap, _ = xs.shape
    tile = min(FFN_ROW_TILE, cap)
    tok = lambda w: pl.BlockSpec((None, tile, w), lambda ei, i: (ei, i, 0))
    wspec = lambda r, c: pl.BlockSpec((None, r, c), lambda ei, i: (ei, 0, 0))
    return pl.pallas_call(
        _expert_ffn_kernel,
        grid=(e, cap // tile),
        in_specs=[tok(D_MODEL), tok(1), wspec(D_MODEL, EXPERT_FF), wspec(D_MODEL, EXPERT_FF), wspec(EXPERT_FF, D_MODEL)],
        out_specs=tok(D_MODEL),
        out_shape=jax.ShapeDtypeStruct((e, cap, D_MODEL), jnp.bfloat16),
        compiler_params=_cparams(("parallel", "arbitrary")),
        name="expert_ffn",
    )(xs, gate, wg, wu, wd)


ONE_BITS_PLUS_1 = 0x3F800001
LIST_ROWS = 16


def _prefix_counts(mask_bf16, tri_ones, strict_lower):
    t = jnp.dot(mask_bf16, tri_ones, preferred_element_type=jnp.float32)
    offs = jnp.dot(strict_lower, t[:, LANES:].astype(jnp.bfloat16), preferred_element_type=jnp.float32)
    return t[:, :LANES] + offs, offs, t[:, LANES:]


def _route_kernel(aff_ref, pos_ref, offs_ref, idx_ref, gate_ref, *, cap):
    x = pltpu.bitcast(aff_ref[...], jnp.int32)
    nb = x.shape[0]

    def count_ge(thr):
        hit = (x >= thr).astype(jnp.int32)
        return jnp.sum(jnp.sum(hit, axis=0, keepdims=True), axis=1, keepdims=True)

    def bisect(_, bounds):
        lo, hi = bounds
        mid = lo + ((hi - lo) >> 1)
        ok = count_ge(mid) >= cap
        return jnp.where(ok, mid, lo), jnp.where(ok, hi, mid)

    lo0 = jnp.zeros((1, 1), jnp.int32)
    hi0 = jnp.full((1, 1), ONE_BITS_PLUS_1, jnp.int32)
    thr, _ = lax.fori_loop(0, 31, bisect, (lo0, hi0))
    above = x > thr
    tie = x == thr
    need = cap - count_ge(thr + 1)

    r_i = lax.broadcasted_iota(jnp.int32, (LANES, 2 * LANES), 0)
    c_i = lax.broadcasted_iota(jnp.int32, (LANES, 2 * LANES), 1)
    tri_ones = ((c_i >= LANES) | (r_i <= c_i)).astype(jnp.bfloat16)
    rr = lax.broadcasted_iota(jnp.int32, (nb, nb), 0)
    cc = lax.broadcasted_iota(jnp.int32, (nb, nb), 1)
    strict_lower = (cc < rr).astype(jnp.bfloat16)

    tie_f = tie.astype(jnp.float32)
    tie_incl, _, _ = _prefix_counts(tie_f.astype(jnp.bfloat16), tri_ones, strict_lower)
    sel = above | (tie & ((tie_incl - tie_f) < need.astype(jnp.float32)))
    sel_incl, offs, row_total = _prefix_counts(sel.astype(jnp.bfloat16), tri_ones, strict_lower)
    in_row = sel_incl - offs
    row_end = offs + row_total
    pos_ref[...] = jnp.where(sel, sel_incl.astype(jnp.int32) - 1, -1)
    offs_ref[...] = offs.astype(jnp.int32)

    aff_t = aff_ref[...].T
    g_hi = aff_t.astype(jnp.bfloat16)
    g_mid = (aff_t - g_hi.astype(jnp.float32)).astype(jnp.bfloat16)
    g_lo = (aff_t - g_hi.astype(jnp.float32) - g_mid.astype(jnp.float32)).astype(jnp.bfloat16)
    row_id = lax.broadcasted_iota(jnp.int32, (1, nb), 1)
    offs_t = offs.T[0:1, :].astype(jnp.int32)
    facts = jnp.concatenate([row_id >> 4, row_id & 15, offs_t >> 7, offs_t & (LANES - 1)], axis=0)
    facts = jnp.concatenate([facts.astype(jnp.float32), jnp.zeros((LIST_ROWS - 4, nb), jnp.float32)], axis=0)
    stack = jnp.concatenate([in_row.T.astype(jnp.bfloat16), facts.astype(jnp.bfloat16), g_hi, g_mid, g_lo], axis=0)
    sub = lax.broadcasted_iota(jnp.int32, (LANES, LANES), 0).astype(jnp.float32)

    def window(w, carry):
        slot = (w * LANES + lax.broadcasted_iota(jnp.int32, (1, LANES), 1)).astype(jnp.float32)
        onehot = ((offs <= slot) & (slot < row_end)).astype(jnp.bfloat16)
        got = jnp.dot(stack, onehot, preferred_element_type=jnp.float32)
        counts = got[:LANES]
        fact = got[LANES:LANES + LIST_ROWS]
        gates = got[LANES + LIST_ROWS:LANES + LIST_ROWS + LANES] + got[2 * LANES + LIST_ROWS:3 * LANES + LIST_ROWS] \
            + got[3 * LANES + LIST_ROWS:]
        row = fact[0:1] * 16.0 + fact[1:2]
        local = slot - (fact[2:3] * float(LANES) + fact[3:4])
        lane = jnp.sum((counts <= local).astype(jnp.float32), axis=0, keepdims=True)
        idx = (row * float(LANES) + lane).astype(jnp.int32)
        gate = jnp.sum(jnp.where(sub == lane, gates, 0.0), axis=0, keepdims=True)
        idx_ref[w] = jnp.broadcast_to(idx, (8, LANES))
        gate_ref[w] = jnp.broadcast_to(gate, (8, LANES))
        return carry

    lax.fori_loop(0, cap // LANES, window, 0)


def _route(aff3, cap):
    e, nb, _ = aff3.shape
    n_win = cap // LANES
    spec = pl.BlockSpec((None, nb, LANES), lambda ei: (ei, 0, 0))
    lspec = pl.BlockSpec((None, n_win, 8, LANES), lambda ei: (ei, 0, 0, 0))
    shape = jax.ShapeDtypeStruct((e, nb, LANES), jnp.int32)
    pos, offs, idx, gate = pl.pallas_call(
        functools.partial(_route_kernel, cap=cap),
        grid=(e,),
        in_specs=[spec],
        out_specs=[spec, spec, lspec, lspec],
        out_shape=[shape, shape, jax.ShapeDtypeStruct((e, n_win, 8, LANES), jnp.int32),
                   jax.ShapeDtypeStruct((e, n_win, 8, LANES), jnp.float32)],
        compiler_params=_cparams(("parallel",)),
        name="route",
    )(aff3)
    return pos, offs, idx[:, :, 0, :].reshape(e, cap), gate[:, :, 0, :].reshape(e, cap)


COMBINE_TILE = 512
WINDOW = LANES
GROUP = 4
MAX_WINDOWS = N_EXPERTS * (COMBINE_TILE // WINDOW + 1)
NO_SLOT = -(1 << 20)


def _combine_kernel(first_ref, count_ref, x_ref, pos_ref, ye_hbm, o_ref, stage, sems, win_e, win_s):
    i = pl.program_id(0)

    def fetch(j, e, win):
        return pltpu.make_async_copy(ye_hbm.at[e, pl.ds(win * WINDOW, WINDOW), :], stage.at[j], sems.at[j])

    def issue(j, e, win, base):
        win_e[j] = e
        win_s[j] = base
        fetch(j, e, win).start()

    n_fetched = jnp.int32(0)
    for e in range(N_EXPERTS):
        first = first_ref[e, i]
        count = count_ref[e, i]

        def issue_one(k, j, e=e, first=first):
            issue(j, e, first + k, (first + k) * WINDOW)
            return j + 1

        n_fetched = lax.fori_loop(0, count, issue_one, n_fetched)
    n_groups = (n_fetched + GROUP - 1) // GROUP
    lax.fori_loop(n_fetched, n_groups * GROUP, lambda j, c: (issue(j, 0, 0, NO_SLOT), c)[1], 0)

    slot_iota = lax.broadcasted_iota(jnp.int32, (WINDOW, COMBINE_TILE), 0)

    def group_body(g, acc):
        onehots = []
        for u in range(GROUP):
            j = g * GROUP + u
            fetch(j, win_e[j], 0).wait()
            pos_row = pos_ref[win_e[j]]
            onehots.append((pos_row == win_s[j] + slot_iota).astype(jnp.bfloat16))
        onehot = jnp.concatenate(onehots, axis=0)
        rows = stage[pl.ds(g * GROUP, GROUP)].reshape(GROUP * WINDOW, D_MODEL)
        return acc + lax.dot_general(onehot, rows, (((0,), (0,)), ((), ())), preferred_element_type=jnp.float32)

    o_ref[...] = lax.fori_loop(0, n_groups, group_body, x_ref[...])


def _combine(xn, pos, offs, ye, cap):
    n = xn.shape[0]
    nt = n // COMBINE_TILE
    rows_per_tile = COMBINE_TILE // LANES
    start = offs[:, ::rows_per_tile, 0]
    stop = jnp.concatenate([start[:, 1:], jnp.full((N_EXPERTS, 1), cap, jnp.int32)], axis=1)
    first = start // WINDOW
    count = jnp.where(stop > start, (stop - 1) // WINDOW - first + 1, 0)
    pos_t = jnp.transpose(pos.reshape(N_EXPERTS, nt, COMBINE_TILE), (1, 0, 2)).reshape(nt, N_EXPERTS, 1, COMBINE_TILE)
    tok = pl.BlockSpec((COMBINE_TILE, D_MODEL), lambda i, f, c: (i, 0))
    return pl.pallas_call(
        _combine_kernel,
        grid_spec=pltpu.PrefetchScalarGridSpec(
            num_scalar_prefetch=2,
            grid=(nt,),
            in_specs=[tok,
                      pl.BlockSpec((None, N_EXPERTS, 1, COMBINE_TILE), lambda i, f, c: (i, 0, 0, 0)),
                      pl.BlockSpec(memory_space=pl.ANY)],
            out_specs=tok,
            scratch_shapes=[pltpu.VMEM((MAX_WINDOWS, WINDOW, D_MODEL), jnp.bfloat16),
                            pltpu.SemaphoreType.DMA((MAX_WINDOWS,)),
                            pltpu.SMEM((MAX_WINDOWS,), jnp.int32),
                            pltpu.SMEM((MAX_WINDOWS,), jnp.int32)],
        ),
        out_shape=jax.ShapeDtypeStruct((n, D_MODEL), jnp.float32),
        compiler_params=_cparams(("arbitrary",)),
        name="combine",
    )(first, count, xn, pos_t, ye)


def _pad_lora(w):
    z = jnp.zeros((N_DIR, N_DIR * w.shape[1], w.shape[2]), w.dtype)
    for d in range(N_DIR):
        z = z.at[d, d * w.shape[1]:(d + 1) * w.shape[1]].set(w[d])
    return z.astype(jnp.bfloat16)


def _layer_params(l, ln1_g, ln2_g, w_in, mu_prev, mu_next, q_norm_g, k_norm_g, w0, w2, a0, a2, g2, k_k, k_a,
                  r_k, lnx_g, lnx_b, w_out, router, w_gate, w_up, w_down):
    bf = lambda z: z.astype(jnp.bfloat16)
    row = lambda z: z.reshape(1, -1)
    return dict(
        ln1_g=ln1_g[l], ln2_g=row(ln2_g[l]), w_in=bf(w_in[l]), mu_prev=row(mu_prev[l]), mu_next=row(mu_next[l]),
        q_gain=row(jnp.tile(q_norm_g[l], ATT_SLOTS)), k_gain=row(jnp.tile(k_norm_g[l], ATT_SLOTS)),
        w0=w0[l], w2=_pad_lora(w2[l]), a0=a0[l], a2=_pad_lora(a2[l]), g2=bf(g2[l]),
        k_k=row(k_k[l]), k_a=row(k_a[l]), r_k=row(r_k[l]), lnx_g=row(lnx_g[l]), lnx_b=row(lnx_b[l]),
        w_out=bf(w_out[l]), router_t=router[l].T, w_gate=bf(w_gate[l]), w_up=bf(w_up[l]), w_down=bf(w_down[l]),
    )


def _encoder_layer(x, p, rope, bd):
    b, t, _ = x.shape
    n = b * t
    q, k, v, s = _in_proj(x, p['ln1_g'], p['w_in'], p['q_gain'], p['k_gain'], rope, bd)
    att_parts = [_band_attn(q[g], k[g], v[g]) for g in range(N_DIL)]
    lhs, add, bonus, gate = _rwkv_prep(
        s, p['mu_prev'], p['mu_next'], p['w0'], p['w2'], p['a0'], p['a2'], p['g2'], p['k_k'], p['k_a'], p['r_k'], bd)
    y = _rwkv_scan(lhs, add)
    xn, h2, aff_t = _out_proj(x, att_parts, y, bonus, gate, p['lnx_g'], p['lnx_b'], p['w_out'], p['ln2_g'],
                              p['router_t'], bd)
    cap = max(1, (EC_CAPACITY_FACTOR * n) // N_EXPERTS)
    pos, offs, idx, gate_e = _route(aff_t.reshape(N_EXPERTS, n // LANES, LANES), cap)
    xs = h2.reshape(n, D_MODEL)[idx]
    ye = _expert_ffn(xs, gate_e[..., None], p['w_gate'], p['w_up'], p['w_down'])
    return _combine(xn.reshape(n, D_MODEL), pos, offs, ye, cap).reshape(b, t, D_MODEL)


def kernel(x_prompt, x_sample, ln1_g, ln2_g, w_in, mu_prev, mu_next, q_norm_g, k_norm_g, w0, w2, a0, a2, g2,
           k_k, k_a, r_k, lnx_g, lnx_b, w_out, router, w_gate, w_up, w_down):
    depth = w_in.shape[0]
    bd = _head_sum_matrix()
    rope_p = _rope_tables(x_prompt.shape[1])
    rope_s = _rope_tables(x_sample.shape[1])
    y_prompt, y_sample = x_prompt, x_sample
    for l in range(depth):
        p = _layer_params(l, ln1_g, ln2_g, w_in, mu_prev, mu_next, q_norm_g, k_norm_g, w0, w2, a0, a2, g2, k_k,
                          k_a, r_k, lnx_g, lnx_b, w_out, router, w_gate, w_up, w_down)
        y_prompt = _encoder_layer(y_prompt, p, rope_p, bd)
        y_sample = _encoder_layer(y_sample, p, rope_s, bd)
    return (y_prompt, y_sample)
```

```python
import functools

import jax
import jax.numpy as jnp
import numpy as np
from jax import lax
from jax.experimental import pallas as pl
from jax.experimental.pallas import tpu as pltpu

D_MODEL = 1024
RW_HEADS = 4
HEAD_DIM = 64
RW_WIDTH = RW_HEADS * HEAD_DIM
DECAY_LORA = 64
AAA_LORA = 64
GATE_LORA = 128
N_DIR = 2
GN_EPS = 64e-5
ATT_SLOTS = 4
DILATIONS = (1, 4, 16)
BAND_HALF = 64
N_DIL = len(DILATIONS)
GROUP_WIDTH = ATT_SLOTS * HEAD_DIM
ATT_WIDTH = N_DIL * GROUP_WIDTH
ATT_COLS = 3 * ATT_WIDTH
SHIFT_COLS = 3 * RW_WIDTH + N_DIR * DECAY_LORA + N_DIR * AAA_LORA + GATE_LORA
IN_COLS = ATT_COLS + SHIFT_COLS
ROPE_THETA = 500000.0
ROPE_DIMS = HEAD_DIM // 4
ROPE_HALF = ROPE_DIMS // 2
N_EXPERTS = 16
EC_CAPACITY_FACTOR = 2
EXPERT_FF = 2048
NORM_EPS = 1e-6
NEG_INF = -1e30
LANES = 128

ROW_TILE = 512
ATT_Q_BLOCK = 128
CHUNK = 64
CHUNKS_PER_ITER = 4
FFN_ROW_TILE = 512
VMEM_LIMIT = 56 * 1024 * 1024

HIGHEST = lax.Precision.HIGHEST


def _cparams(sem):
    return pltpu.CompilerParams(dimension_semantics=sem, vmem_limit_bytes=VMEM_LIMIT)


def _head_sum_matrix():
    h = np.arange(GROUP_WIDTH) // HEAD_DIM
    return jnp.asarray((h[:, None] == h[None, :]).astype(np.float32), dtype=jnp.bfloat16)


def _head_sum(x, bd):
    hi = x.astype(jnp.bfloat16)
    lo = (x - hi.astype(jnp.float32)).astype(jnp.bfloat16)
    return (jnp.dot(hi, bd, preferred_element_type=jnp.float32)
            + jnp.dot(lo, bd, preferred_element_type=jnp.float32))


def _rope_tables(t):
    inv = jnp.power(jnp.float32(ROPE_THETA), -jnp.arange(ROPE_HALF, dtype=jnp.float32) * 2.0 / ROPE_DIMS)
    ang = jnp.arange(t, dtype=jnp.float32)[:, None] * inv[None, :]
    cos, sin = jnp.cos(ang), jnp.sin(ang)
    zeros = jnp.zeros((t, HEAD_DIM - ROPE_DIMS), jnp.float32)
    zero8 = jnp.zeros((t, ROPE_HALF), jnp.float32)
    c = jnp.concatenate([cos, cos, zeros + 1.0], -1)
    s1 = jnp.concatenate([zero8, sin, zeros], -1)
    s2 = jnp.concatenate([-sin, zero8, zeros], -1)
    tile = lambda z: jnp.tile(z, (1, ATT_SLOTS))
    return tile(c), tile(s1), tile(s2)


def _in_proj_kernel(x_ref, g_ref, w_ref, qg_ref, kg_ref, c_ref, s1_ref, s2_ref, bd_ref,
                    q0_ref, q1_ref, q2_ref, k0_ref, k1_ref, k2_ref, v0_ref, v1_ref, v2_ref, s_ref,
                    tmp_ref):
    x = x_ref[...]
    h = x * lax.rsqrt(jnp.mean(x * x, -1, keepdims=True) + NORM_EPS) * g_ref[...]
    hb = h.astype(jnp.bfloat16)
    bd = bd_ref[...]
    c, s1, s2 = c_ref[...], s1_ref[...], s2_ref[...]
    rows = x.shape[0]

    def deinterleave(val, out_ref, dil):
        if dil == 1:
            out_ref[0] = val.astype(out_ref.dtype)
            return
        for hf in range(GROUP_WIDTH // LANES):
            tmp_ref[hf] = val[:, hf * LANES:(hf + 1) * LANES]
        for r in range(dil):
            for hf in range(GROUP_WIDTH // LANES):
                out_ref[r, :, hf * LANES:(hf + 1) * LANES] = (
                    tmp_ref[hf, pl.ds(r, rows // dil, stride=dil), :].astype(out_ref.dtype))

    proj = jnp.dot(hb, w_ref[:, :ATT_COLS], preferred_element_type=jnp.float32)

    def normed_rope(col0, gain, scale):
        p = proj[:, col0:col0 + GROUP_WIDTH]
        ms = _head_sum(p * p, bd) * (1.0 / HEAD_DIM)
        n = p * lax.rsqrt(ms + NORM_EPS) * gain
        n = n * c + pltpu.roll(n, ROPE_HALF, 1) * s1 + pltpu.roll(n, GROUP_WIDTH - ROPE_HALF, 1) * s2
        return n * scale if scale != 1.0 else n

    q_refs = (q0_ref, q1_ref, q2_ref)
    k_refs = (k0_ref, k1_ref, k2_ref)
    v_refs = (v0_ref, v1_ref, v2_ref)
    for g, dil in enumerate(DILATIONS):
        deinterleave(normed_rope(g * GROUP_WIDTH, qg_ref[...], HEAD_DIM ** -0.5), q_refs[g], dil)
        deinterleave(normed_rope(ATT_WIDTH + g * GROUP_WIDTH, kg_ref[...], 1.0), k_refs[g], dil)
        col0 = 2 * ATT_WIDTH + g * GROUP_WIDTH
        deinterleave(proj[:, col0:col0 + GROUP_WIDTH], v_refs[g], dil)
    s_ref[...] = jnp.dot(hb, w_ref[:, ATT_COLS:], preferred_element_type=jnp.float32)


def _in_proj(x, ln1_g, w_in_bf16, q_gain, k_gain, rope, bd):
    b, t, _ = x.shape
    nt = t // ROW_TILE
    row2 = lambda bi, i: (0, 0)
    grp_shapes, grp_specs = [], []
    for _ in range(3):
        for dil in DILATIONS:
            grp_shapes.append(jax.ShapeDtypeStruct((b, dil, t // dil, GROUP_WIDTH), jnp.bfloat16))
            grp_specs.append(pl.BlockSpec((None, dil, ROW_TILE // dil, GROUP_WIDTH), lambda bi, i: (bi, 0, i, 0)))
    tab_spec = pl.BlockSpec((ROW_TILE, GROUP_WIDTH), lambda bi, i: (i, 0))
    outs = pl.pallas_call(
        _in_proj_kernel,
        grid=(b, nt),
        in_specs=[
            pl.BlockSpec((None, ROW_TILE, D_MODEL), lambda bi, i: (bi, i, 0)),
            pl.BlockSpec((1, D_MODEL), row2),
            pl.BlockSpec((D_MODEL, IN_COLS), row2),
            pl.BlockSpec((1, GROUP_WIDTH), row2),
            pl.BlockSpec((1, GROUP_WIDTH), row2),
            tab_spec, tab_spec, tab_spec,
            pl.BlockSpec((GROUP_WIDTH, GROUP_WIDTH), row2),
        ],
        out_specs=grp_specs + [pl.BlockSpec((None, ROW_TILE, SHIFT_COLS), lambda bi, i: (bi, i, 0))],
        out_shape=grp_shapes + [jax.ShapeDtypeStruct((b, t, SHIFT_COLS), jnp.float32)],
        scratch_shapes=[pltpu.VMEM((GROUP_WIDTH // LANES, ROW_TILE, LANES), jnp.float32)],
        compiler_params=_cparams(("parallel", "parallel")),
        name="in_proj",
    )(x, ln1_g.reshape(1, D_MODEL), w_in_bf16, q_gain, k_gain, *rope, bd)
    q = outs[0:3]
    k = outs[3:6]
    v = outs[6:9]
    return q, k, v, outs[9]


def _band_attn_kernel(q_ref, kp_ref, kc_ref, kn_ref, vp_ref, vc_ref, vn_ref, o_ref, lse_ref, *, seq_len, q_chunk):
    i = pl.program_id(1)
    kext = jnp.concatenate([kp_ref[...], kc_ref[...], kn_ref[...]], axis=0)
    vext = jnp.concatenate([vp_ref[...], vc_ref[...], vn_ref[...]], axis=0)
    kw = ATT_Q_BLOCK + 2 * BAND_HALF
    lane_head = lax.broadcasted_iota(jnp.int32, (ATT_Q_BLOCK, GROUP_WIDTH), 1) // HEAD_DIM
    qi = lax.broadcasted_iota(jnp.int32, (ATT_Q_BLOCK, kw), 0)
    kc = lax.broadcasted_iota(jnp.int32, (ATT_Q_BLOCK, kw), 1)
    diff = kc - BAND_HALF - qi
    band = (diff <= BAND_HALF) & (diff >= -BAND_HALF)
    for a in range(0, q_chunk, ATT_Q_BLOCK):
        kpos = i * q_chunk + (a - BAND_HALF) + kc
        valid = band & (kpos >= 0) & (kpos < seq_len)
        qb = q_ref[a:a + ATT_Q_BLOCK, :]
        kb = kext[a:a + kw]
        vb = vext[a:a + kw]
        heads = range(ATT_SLOTS)
        mine = [lane_head == h for h in heads]
        s = [lax.dot_general(jnp.where(mine[h], qb, jnp.zeros_like(qb)), kb, (((1,), (1,)), ((), ())),
                             preferred_element_type=jnp.float32) for h in heads]
        s = [jnp.where(valid, s_h, NEG_INF) for s_h in s]
        m = [jnp.max(s_h, -1, keepdims=True) for s_h in s]
        p = [jnp.exp(s_h - m_h) for s_h, m_h in zip(s, m)]
        den = [jnp.sum(p_h, -1, keepdims=True) for p_h in p]
        pv = [jnp.dot(p_h.astype(jnp.bfloat16), vb, preferred_element_type=jnp.float32) for p_h in p]
        acc = jnp.zeros((ATT_Q_BLOCK, GROUP_WIDTH), jnp.float32)
        lse = jnp.zeros((ATT_Q_BLOCK, GROUP_WIDTH), jnp.float32)
        for h in heads:
            acc = jnp.where(mine[h], pv[h] / den[h], acc)
            lse = jnp.where(mine[h], m[h] + jnp.log(den[h]), lse)
        o_ref[a:a + ATT_Q_BLOCK, :] = acc
        lse_ref[a:a + ATT_Q_BLOCK, :] = lse


def _band_attn(q, k, v):
    b, dil, seq_len, _ = q.shape
    ns = b * dil
    q, k, v = (z.reshape(ns, seq_len, GROUP_WIDTH) for z in (q, k, v))
    q_chunk = min(seq_len, 512)
    halo_per_chunk = q_chunk // BAND_HALF
    last_halo = seq_len // BAND_HALF - 1
    cur = pl.BlockSpec((None, q_chunk, GROUP_WIDTH), lambda s, i: (s, i, 0))
    prev = pl.BlockSpec((None, BAND_HALF, GROUP_WIDTH), lambda s, i: (s, jnp.maximum(i * halo_per_chunk - 1, 0), 0))
    nxt = pl.BlockSpec((None, BAND_HALF, GROUP_WIDTH),
                       lambda s, i: (s, jnp.minimum((i + 1) * halo_per_chunk, last_halo), 0))
    o, lse = pl.pallas_call(
        functools.partial(_band_attn_kernel, seq_len=seq_len, q_chunk=q_chunk),
        grid=(ns, seq_len // q_chunk),
        in_specs=[cur, prev, cur, nxt, prev, cur, nxt],
        out_specs=[cur, cur],
        out_shape=[jax.ShapeDtypeStruct((ns, seq_len, GROUP_WIDTH), jnp.float32)] * 2,
        compiler_params=_cparams(("parallel", "parallel")),
        name="band_attn",
    )(q, k, k, k, v, v, v)
    shape4 = (b, dil, seq_len, GROUP_WIDTH)
    return o.reshape(shape4), lse.reshape(shape4)


def _softplus(x):
    return jnp.maximum(x, 0.0) + jnp.log1p(jnp.exp(-jnp.abs(x)))


def _sigmoid(x):
    return 1.0 / (1.0 + jnp.exp(-x))


def _rwkv_prep_kernel(sp_ref, sc_ref, sn_ref, mup_ref, mun_ref, w0_ref, w2_ref, a0_ref, a2_ref, g2_ref,
                      kk_ref, ka_ref, rk_ref, bd_ref,
                      lhs_ref, add_ref, bonus_ref, gate_ref,
                      r_s, v_s, nkk_s, e_s, kd_s, beta_s):
    i = pl.program_id(1)
    last = pl.num_programs(1) - 1
    p = sc_ref[...]
    rows = p.shape[0]
    row = lax.broadcasted_iota(jnp.int32, p.shape, 0)
    halo_prev = sp_ref[7:8, :] * (i > 0).astype(jnp.float32)
    halo_next = sn_ref[0:1, :] * (i < last).astype(jnp.float32)
    p_prev = jnp.where(row == 0, halo_prev, pltpu.roll(p, 1, 0))
    p_next = jnp.where(row == rows - 1, halo_next, pltpu.roll(p, rows - 1, 0))
    u = p + mup_ref[...] * (p_prev - p) + mun_ref[...] * (p_next - p)

    bd = bd_ref[...]
    r = u[:, 0:RW_WIDTH]
    k = u[:, RW_WIDTH:2 * RW_WIDTH]
    v = u[:, 2 * RW_WIDTH:3 * RW_WIDTH]
    c0 = 3 * RW_WIDTH
    lw = jnp.tanh(u[:, c0:c0 + 128]).astype(jnp.bfloat16)
    la = u[:, c0 + 128:c0 + 256].astype(jnp.bfloat16)
    lg = _sigmoid(u[:, c0 + 256:c0 + 384]).astype(jnp.bfloat16)

    kk = k * kk_ref[...]
    kk = kk * lax.rsqrt(_head_sum(kk * kk, bd) + 1e-12)
    r_s[...] = r
    v_s[...] = v
    nkk_s[...] = -kk
    gate_ref[...] = jnp.dot(lg, g2_ref[...], preferred_element_type=jnp.float32)
    bonus = jnp.zeros_like(v)
    for d in range(N_DIR):
        z = w0_ref[d:d + 1, :] + jnp.dot(lw, w2_ref[d], preferred_element_type=jnp.float32)
        w_log = -_softplus(-z) - 0.5
        e_s[d] = jnp.exp(w_log)
        a = _sigmoid(a0_ref[d:d + 1, :] + jnp.dot(la, a2_ref[d], preferred_element_type=jnp.float32))
        kd = k * (1.0 + (a - 1.0) * ka_ref[...])
        kd_s[d] = kd
        beta_s[d] = kk * a
        bonus = bonus + _head_sum(r * kd * rk_ref[...], bd) * v
    bonus_ref[...] = bonus

    idx_t = lax.broadcasted_iota(jnp.int32, (CHUNK, CHUNK), 0)
    idx_s = lax.broadcasted_iota(jnp.int32, (CHUNK, CHUNK), 1)
    eye = (idx_t == idx_s).astype(jnp.float32)
    bf = lambda z: z.astype(jnp.bfloat16)
    mm = lambda a_, b_: jnp.dot(bf(a_), bf(b_), preferred_element_type=jnp.float32)
    mm_nt = lambda a_, b_: lax.dot_general(bf(a_), bf(b_), (((1,), (1,)), ((), ())), preferred_element_type=jnp.float32)
    mm_tn = lambda a_, b_: lax.dot_general(bf(a_), bf(b_), (((0,), (0,)), ((), ())), preferred_element_type=jnp.float32)

    def chunk_body(cp, carry):
        units = []
        for c, d in [(cp * CHUNKS_PER_ITER + j, d_) for j in range(CHUNKS_PER_ITER) for d_ in range(N_DIR)]:
            rows_c = pl.ds(pl.multiple_of(c * CHUNK, CHUNK), CHUNK)
            r_c, v_c, nkk_c = r_s[rows_c, :], v_s[rows_c, :], nkk_s[rows_c, :]
            before = (idx_s < idx_t) if d == 0 else (idx_s > idx_t)
            before_eq = (idx_s <= idx_t) if d == 0 else (idx_s >= idx_t)
            last = CHUNK - 1 if d == 0 else 0
            e_c = e_s[d, rows_c, :]
            cl = jnp.dot(before_eq.astype(jnp.float32), e_c, precision=HIGHEST, preferred_element_type=jnp.float32)
            w_inc = jnp.exp(-cl)
            w_exc = jnp.exp(e_c - cl)
            w_inv = jnp.exp(cl)
            w_tot = w_inc[last:last + 1, :]
            a_t = nkk_c * w_exc
            r_t = r_c * w_inc
            b_h = beta_s[d, rows_c, :] * w_inv
            k_h = kd_s[d, rows_c, :] * w_inv
            b_t = b_h * w_tot
            k_t = k_h * w_tot
            for h in range(RW_HEADS):
                hs = slice(h * HEAD_DIM, (h + 1) * HEAD_DIM)
                units.append(dict(chunk=c, dir=d, before=before, before_eq=before_eq, a_t=a_t[:, hs], r_t=r_t[:, hs], b_h=b_h[:, hs],
                                  k_h=k_h[:, hs], b_t=b_t[:, hs], k_t=k_t[:, hs], w_tot=w_tot[:, hs], v=v_c[:, hs]))
        each = lambda fn: [fn(un) for un in units]
        g = each(lambda un: mm_nt(jnp.concatenate([un['a_t'], un['r_t']], axis=0),
                                  jnp.concatenate([un['b_h'], un['k_h']], axis=0)))
        for un, g_u in zip(units, g):
            un['m_ab'] = jnp.where(un['before'], g_u[:CHUNK, :CHUNK], 0.0)
            un['m_ak'] = jnp.where(un['before'], g_u[:CHUNK, CHUNK:], 0.0)
            un['n_rb'] = jnp.where(un['before_eq'], g_u[CHUNK:, :CHUNK], 0.0)
            un['n_rk'] = jnp.where(un['before_eq'], g_u[CHUNK:, CHUNK:], 0.0)
        t_inv = each(lambda un: eye + un['m_ab'])
        pw = each(lambda un: un['m_ab'])
        for _ in range(CHUNK.bit_length() - 2):
            pw = [mm(p_, p_) for p_ in pw]
            t_inv = [t_ + mm(t_, p_) for t_, p_ in zip(t_inv, pw)]
        mv = each(lambda un: mm(un['m_ak'], un['v']))
        nv = each(lambda un: mm(un['n_rk'], un['v']))
        kv = each(lambda un: mm_tn(un['k_t'], un['v']))
        tx = [mm(t_, jnp.concatenate([un['a_t'], mv_], axis=1)) for t_, un, mv_ in zip(t_inv, units, mv)]
        nx = [mm(un['n_rb'], tx_) for un, tx_ in zip(units, tx)]
        bx = [mm_tn(un['b_t'], tx_) for un, tx_ in zip(units, tx)]
        lhs_parts, add_parts = [], []
        for un, nx_, bx_, nv_, kv_ in zip(units, nx, bx, nv, kv):
            ra = un['r_t'] + nx_[:, :HEAD_DIM]
            y0 = nx_[:, HEAD_DIM:] + nv_
            phi = eye * un['w_tot'] + bx_[:, :HEAD_DIM]
            psi = bx_[:, HEAD_DIM:] + kv_
            lhs_parts.append(jnp.concatenate([ra, phi], axis=0))
            add_parts.append(jnp.concatenate([y0, psi], axis=0))
        for g0 in range(0, len(units), RW_HEADS):
            c, d = units[g0]['chunk'], units[g0]['dir']
            lhs_ref[d, c] = jnp.concatenate(lhs_parts[g0:g0 + RW_HEADS], axis=1)
            add_ref[d, c] = jnp.concatenate(add_parts[g0:g0 + RW_HEADS], axis=1)
        return carry

    lax.fori_loop(0, rows // (CHUNK * CHUNKS_PER_ITER), chunk_body, 0)


def _rwkv_prep(s, mu_prev, mu_next, w0, w2p, a0, a2p, g2, k_k, k_a, r_k, bd):
    b, t, _ = s.shape
    nt = t // ROW_TILE
    halo_per_tile = ROW_TILE // 8
    last_halo = t // 8 - 1
    row2 = lambda bi, i: (0, 0)
    row3 = lambda bi, i: (0, 0, 0)
    vec = lambda n: pl.BlockSpec((1, n), row2)
    chunks_per_tile = ROW_TILE // CHUNK
    cspec = pl.BlockSpec((N_DIR, None, chunks_per_tile, 2 * CHUNK, RW_WIDTH), lambda bi, i: (0, bi, i, 0, 0))
    nspec = pl.BlockSpec((None, ROW_TILE, RW_WIDTH), lambda bi, i: (bi, i, 0))
    cshape = jax.ShapeDtypeStruct((N_DIR, b, t // CHUNK, 2 * CHUNK, RW_WIDTH), jnp.float32)
    nshape = jax.ShapeDtypeStruct((b, t, RW_WIDTH), jnp.float32)
    tile_f32 = pltpu.VMEM((ROW_TILE, RW_WIDTH), jnp.float32)
    tile2_f32 = pltpu.VMEM((N_DIR, ROW_TILE, RW_WIDTH), jnp.float32)
    return pl.pallas_call(
        _rwkv_prep_kernel,
        grid=(b, nt),
        in_specs=[
            pl.BlockSpec((None, 8, SHIFT_COLS), lambda bi, i: (bi, jnp.maximum(i * halo_per_tile - 1, 0), 0)),
            pl.BlockSpec((None, ROW_TILE, SHIFT_COLS), lambda bi, i: (bi, i, 0)),
            pl.BlockSpec((None, 8, SHIFT_COLS), lambda bi, i: (bi, jnp.minimum((i + 1) * halo_per_tile, last_halo), 0)),
            vec(SHIFT_COLS), vec(SHIFT_COLS),
            pl.BlockSpec((N_DIR, RW_WIDTH), row2),
            pl.BlockSpec((N_DIR, 128, RW_WIDTH), row3),
            pl.BlockSpec((N_DIR, RW_WIDTH), row2),
            pl.BlockSpec((N_DIR, 128, RW_WIDTH), row3),
            pl.BlockSpec((GATE_LORA, RW_WIDTH), row2),
            vec(RW_WIDTH), vec(RW_WIDTH), vec(RW_WIDTH),
            pl.BlockSpec((GROUP_WIDTH, GROUP_WIDTH), row2),
        ],
        out_specs=[cspec, cspec, nspec, nspec],
        out_shape=[cshape, cshape, nshape, nshape],
        scratch_shapes=[tile_f32, tile_f32, tile_f32, tile2_f32, tile2_f32, tile2_f32],
        compiler_params=_cparams(("parallel", "parallel")),
        name="rwkv_prep",
    )(s, s, s, mu_prev, mu_next, w0, w2p, a0, a2p, g2, k_k, k_a, r_k, bd)


def _split_bf16(x):
    hi = x.astype(jnp.bfloat16)
    return hi, (x - hi.astype(jnp.float32)).astype(jnp.bfloat16)


def _rwkv_scan_kernel(lhs_f, add_f, lhs_b, add_b, y_f, y_b, state_ref):
    n_chunks = lhs_f.shape[0]

    @pl.when(pl.program_id(1) == 0)
    def _():
        state_ref[...] = jnp.zeros_like(state_ref)

    dot = lambda a_, b_: jnp.dot(a_, b_, preferred_element_type=jnp.float32)

    def chunk_body(cc, carry):
        chains = []
        for d, (lhs_ref, add_ref, c) in enumerate(((lhs_f, add_f, cc), (lhs_b, add_b, n_chunks - 1 - cc))):
            for h in range(RW_HEADS):
                hs = slice(h * HEAD_DIM, (h + 1) * HEAD_DIM)
                chains.append((d, h, c, hs, lhs_ref, add_ref))
        lsplit = [_split_bf16(lhs_ref[c, :, hs]) for (_, _, c, hs, lhs_ref, _) in chains]
        ssplit = [_split_bf16(state_ref[d, h]) for (d, h, *_) in chains]
        res = [dot(l_hi, s_hi) + dot(l_lo, s_hi) + dot(l_hi, s_lo) for (l_hi, l_lo), (s_hi, s_lo) in zip(lsplit, ssplit)]
        res = [r_ + add_ref[c, :, hs] for r_, (_, _, c, hs, _, add_ref) in zip(res, chains)]
        for r_, (d, h, *_) in zip(res, chains):
            state_ref[d, h] = r_[CHUNK:]
        for d, (y_ref, c) in enumerate(((y_f, cc), (y_b, n_chunks - 1 - cc))):
            rows = pl.ds(pl.multiple_of(c * CHUNK, CHUNK), CHUNK)
            y_ref[rows, :] = jnp.concatenate([r_[:CHUNK] for r_ in res[d * RW_HEADS:(d + 1) * RW_HEADS]], axis=1)
        return carry

    lax.fori_loop(0, n_chunks, chunk_body, 0)


def _rwkv_scan(lhs, add):
    _, b, nchunk, _, _ = lhs.shape
    per_step = ROW_TILE // CHUNK
    nc = nchunk // per_step
    fwd = pl.BlockSpec((None, None, per_step, 2 * CHUNK, RW_WIDTH), lambda bi, c: (0, bi, c, 0, 0))
    bwd = pl.BlockSpec((None, None, per_step, 2 * CHUNK, RW_WIDTH), lambda bi, c: (1, bi, nc - 1 - c, 0, 0))
    y_shape = jax.ShapeDtypeStruct((b, nchunk * CHUNK, RW_WIDTH), jnp.float32)
    return pl.pallas_call(
        _rwkv_scan_kernel,
        grid=(b, nc),
        in_specs=[fwd, fwd, bwd, bwd],
        out_specs=[pl.BlockSpec((None, ROW_TILE, RW_WIDTH), lambda bi, c: (bi, c, 0)),
                   pl.BlockSpec((None, ROW_TILE, RW_WIDTH), lambda bi, c: (bi, nc - 1 - c, 0))],
        out_shape=[y_shape, y_shape],
        scratch_shapes=[pltpu.VMEM((N_DIR, RW_HEADS, HEAD_DIM, HEAD_DIM), jnp.float32)],
        compiler_params=_cparams(("parallel", "arbitrary")),
        name="rwkv_scan",
    )(lhs, add, lhs, add)


def _out_proj_kernel(x_ref, o0_ref, l0_ref, o1_ref, l1_ref, o2_ref, l2_ref, yf_ref, yb_ref, bonus_ref, gate_ref,
                     lnxg_ref, lnxb_ref, wout_ref, ln2_ref, router_ref, bd_ref,
                     xn_ref, h2_ref, aff_ref, so1_ref, sl1_ref, so2_ref, sl2_ref):
    rows = x_ref.shape[0]

    def interleave(src_ref, dst_ref, dil):
        halves = range(GROUP_WIDTH // LANES)
        for r in range(dil):
            for hf in halves:
                dst_ref[hf, pl.ds(r, rows // dil, stride=dil), :] = src_ref[r, :, hf * LANES:(hf + 1) * LANES]
        return jnp.concatenate([dst_ref[hf] for hf in halves], axis=-1)

    o1 = interleave(o1_ref, so1_ref, DILATIONS[1])
    l1 = interleave(l1_ref, sl1_ref, DILATIONS[1])
    o2 = interleave(o2_ref, so2_ref, DILATIONS[2])
    l2 = interleave(l2_ref, sl2_ref, DILATIONS[2])
    l0 = l0_ref[0]
    m = jnp.maximum(jnp.maximum(l0, l1), l2)
    e0, e1, e2 = jnp.exp(l0 - m), jnp.exp(l1 - m), jnp.exp(l2 - m)
    att = (e0 * o0_ref[0] + e1 * o1 + e2 * o2) / (e0 + e1 + e2)

    bd = bd_ref[...]
    y = yf_ref[...] + yb_ref[...]
    mu = _head_sum(y, bd) * (1.0 / HEAD_DIM)
    dlt = y - mu
    var = _head_sum(dlt * dlt, bd) * (1.0 / HEAD_DIM)
    yn = dlt * lax.rsqrt(var + GN_EPS) * lnxg_ref[...] + lnxb_ref[...]
    rw = (yn + bonus_ref[...]) * gate_ref[...]

    mixed = (jnp.dot(att.astype(jnp.bfloat16), wout_ref[0:GROUP_WIDTH, :], preferred_element_type=jnp.float32)
             + jnp.dot(rw.astype(jnp.bfloat16), wout_ref[GROUP_WIDTH:, :], preferred_element_type=jnp.float32))
    xn = x_ref[...] + mixed
    xn_ref[...] = xn
    h2 = xn * lax.rsqrt(jnp.mean(xn * xn, -1, keepdims=True) + NORM_EPS) * ln2_ref[...]
    h2_ref[...] = h2.astype(h2_ref.dtype)
    logits = lax.dot_general(router_ref[...], h2, (((1,), (1,)), ((), ())),
                             precision=HIGHEST, preferred_element_type=jnp.float32)
    ex = jnp.exp(logits - jnp.max(logits, 0, keepdims=True))
    aff_ref[...] = ex / jnp.sum(ex, 0, keepdims=True)


def _out_proj(x, att_parts, y, bonus, gate, lnx_g, lnx_b, w_out_bf16, ln2_g, router_t, bd):
    b, t, _ = x.shape
    nt = t // ROW_TILE
    row2 = lambda bi, i: (0, 0)
    tok = lambda w: pl.BlockSpec((None, ROW_TILE, w), lambda bi, i: (bi, i, 0))
    grp = lambda dil: pl.BlockSpec((None, dil, ROW_TILE // dil, GROUP_WIDTH), lambda bi, i: (bi, 0, i, 0))
    vec = lambda n: pl.BlockSpec((1, n), row2)
    grp_specs, grp_args = [], []
    for (o, lse), dil in zip(att_parts, DILATIONS):
        grp_specs += [grp(dil), grp(dil)]
        grp_args += [o, lse]
    return pl.pallas_call(
        _out_proj_kernel,
        grid=(b, nt),
        in_specs=[tok(D_MODEL)] + grp_specs + [
            tok(RW_WIDTH), tok(RW_WIDTH), tok(RW_WIDTH), tok(RW_WIDTH), vec(RW_WIDTH), vec(RW_WIDTH),
            pl.BlockSpec((GROUP_WIDTH + RW_WIDTH, D_MODEL), row2),
            vec(D_MODEL),
            pl.BlockSpec((N_EXPERTS, D_MODEL), row2),
            pl.BlockSpec((GROUP_WIDTH, GROUP_WIDTH), row2),
        ],
        out_specs=[tok(D_MODEL), tok(D_MODEL),
                   pl.BlockSpec((N_EXPERTS, ROW_TILE), lambda bi, i: (0, bi * nt + i))],
        out_shape=[jax.ShapeDtypeStruct((b, t, D_MODEL), jnp.float32),
                   jax.ShapeDtypeStruct((b, t, D_MODEL), jnp.bfloat16),
                   jax.ShapeDtypeStruct((N_EXPERTS, b * t), jnp.float32)],
        scratch_shapes=[pltpu.VMEM((GROUP_WIDTH // LANES, ROW_TILE, LANES), jnp.float32)] * 4,
        compiler_params=_cparams(("parallel", "parallel")),
        name="out_proj",
    )(x, *grp_args, *y, bonus, gate, lnx_g, lnx_b, w_out_bf16, ln2_g, router_t, bd)


FF_CHUNK = 512


def _expert_ffn_kernel(x_ref, gate_ref, wg_ref, wu_ref, wd_ref, o_ref):
    x = x_ref[...]
    acc = jnp.zeros(o_ref.shape, jnp.float32)
    for f0 in range(0, EXPERT_FF, FF_CHUNK):
        hg = jnp.dot(x, wg_ref[:, f0:f0 + FF_CHUNK], preferred_element_type=jnp.float32)
        hu = jnp.dot(x, wu_ref[:, f0:f0 + FF_CHUNK], preferred_element_type=jnp.float32)
        hid = (hg * _sigmoid(hg) * hu).astype(jnp.bfloat16)
        acc = acc + jnp.dot(hid, wd_ref[f0:f0 + FF_CHUNK, :], preferred_element_type=jnp.float32)
    o_ref[...] = (acc * gate_ref[...]).astype(o_ref.dtype)


def _expert_ffn(xs, gate, wg, wu, wd):
    e, cap, _ = xs.shape
    tile = min(FFN_ROW_TILE, cap)
    tok = lambda w: pl.BlockSpec((None, tile, w), lambda ei, i: (ei, i, 0))
    wspec = lambda r, c: pl.BlockSpec((None, r, c), lambda ei, i: (ei, 0, 0))
    return pl.pallas_call(
        _expert_ffn_kernel,
        grid=(e, cap // tile),
        in_specs=[tok(D_MODEL), tok(1), wspec(D_MODEL, EXPERT_FF), wspec(D_MODEL, EXPERT_FF), wspec(EXPERT_FF, D_MODEL)],
        out_specs=tok(D_MODEL),
        out_shape=jax.ShapeDtypeStruct((e, cap, D_MODEL), jnp.bfloat16),
        compiler_params=_cparams(("parallel", "arbitrary")),
        name="expert_ffn",
    )(xs, gate, wg, wu, wd)


ONE_BITS_PLUS_1 = 0x3F800001
LIST_ROWS = 16


def _prefix_counts(mask_bf16, tri_ones, strict_lower):
    t = jnp.dot(mask_bf16, tri_ones, preferred_element_type=jnp.float32)
    offs = jnp.dot(strict_lower, t[:, LANES:].astype(jnp.bfloat16), preferred_element_type=jnp.float32)
    return t[:, :LANES] + offs, offs, t[:, LANES:]


def _route_kernel(aff_ref, pos_ref, offs_ref, idx_ref, gate_ref, *, cap):
    x = pltpu.bitcast(aff_ref[...], jnp.int32)
    nb = x.shape[0]

    def count_ge(thr):
        hit = (x >= thr).astype(jnp.int32)
        return jnp.sum(jnp.sum(hit, axis=0, keepdims=True), axis=1, keepdims=True)

    def bisect(_, bounds):
        lo, hi = bounds
        mid = lo + ((hi - lo) >> 1)
        ok = count_ge(mid) >= cap
        return jnp.where(ok, mid, lo), jnp.where(ok, hi, mid)

    lo0 = jnp.zeros((1, 1), jnp.int32)
    hi0 = jnp.full((1, 1), ONE_BITS_PLUS_1, jnp.int32)
    thr, _ = lax.fori_loop(0, 31, bisect, (lo0, hi0))
    above = x > thr
    tie = x == thr
    need = cap - count_ge(thr + 1)

    r_i = lax.broadcasted_iota(jnp.int32, (LANES, 2 * LANES), 0)
    c_i = lax.broadcasted_iota(jnp.int32, (LANES, 2 * LANES), 1)
    tri_ones = ((c_i >= LANES) | (r_i <= c_i)).astype(jnp.bfloat16)
    rr = lax.broadcasted_iota(jnp.int32, (nb, nb), 0)
    cc = lax.broadcasted_iota(jnp.int32, (nb, nb), 1)
    strict_lower = (cc < rr).astype(jnp.bfloat16)

    tie_f = tie.astype(jnp.float32)
    tie_incl, _, _ = _prefix_counts(tie_f.astype(jnp.bfloat16), tri_ones, strict_lower)
    sel = above | (tie & ((tie_incl - tie_f) < need.astype(jnp.float32)))
    sel_incl, offs, row_total = _prefix_counts(sel.astype(jnp.bfloat16), tri_ones, strict_lower)
    in_row = sel_incl - offs
    row_end = offs + row_total
    pos_ref[...] = jnp.where(sel, sel_incl.astype(jnp.int32) - 1, -1)
    offs_ref[...] = offs.astype(jnp.int32)

    aff_t = aff_ref[...].T
    g_hi = aff_t.astype(jnp.bfloat16)
    g_mid = (aff_t - g_hi.astype(jnp.float32)).astype(jnp.bfloat16)
    g_lo = (aff_t - g_hi.astype(jnp.float32) - g_mid.astype(jnp.float32)).astype(jnp.bfloat16)
    row_id = lax.broadcasted_iota(jnp.int32, (1, nb), 1)
    offs_t = offs.T[0:1, :].astype(jnp.int32)
    facts = jnp.concatenate([row_id >> 4, row_id & 15, offs_t >> 7, offs_t & (LANES - 1)], axis=0)
    facts = jnp.concatenate([facts.astype(jnp.float32), jnp.zeros((LIST_ROWS - 4, nb), jnp.float32)], axis=0)
    stack = jnp.concatenate([in_row.T.astype(jnp.bfloat16), facts.astype(jnp.bfloat16), g_hi, g_mid, g_lo], axis=0)
    sub = lax.broadcasted_iota(jnp.int32, (LANES, LANES), 0).astype(jnp.float32)

    def window_pair(wp, carry):
        ws = [wp * 2, wp * 2 + 1]
        slots = [(w * LANES + lax.broadcasted_iota(jnp.int32, (1, LANES), 1)).astype(jnp.float32) for w in ws]
        onehots = [((offs <= s_) & (s_ < row_end)).astype(jnp.bfloat16) for s_ in slots]
        gots = [jnp.dot(stack, oh, preferred_element_type=jnp.float32) for oh in onehots]
        for w, slot, got in zip(ws, slots, gots):
            emit(w, slot, got)
        return carry

    def emit(w, slot, got):
        counts = got[:LANES]
        fact = got[LANES:LANES + LIST_ROWS]
        gates = got[LANES + LIST_ROWS:LANES + LIST_ROWS + LANES] + got[2 * LANES + LIST_ROWS:3 * LANES + LIST_ROWS] \
            + got[3 * LANES + LIST_ROWS:]
        row = fact[0:1] * 16.0 + fact[1:2]
        local = slot - (fact[2:3] * float(LANES) + fact[3:4])
        lane = jnp.sum((counts <= local).astype(jnp.float32), axis=0, keepdims=True)
        idx = (row * float(LANES) + lane).astype(jnp.int32)
        gate = jnp.sum(jnp.where(sub == lane, gates, 0.0), axis=0, keepdims=True)
        idx_ref[w] = jnp.broadcast_to(idx, (8, LANES))
        gate_ref[w] = jnp.broadcast_to(gate, (8, LANES))

    lax.fori_loop(0, cap // (2 * LANES), window_pair, 0)


def _route(aff3, cap):
    e, nb, _ = aff3.shape
    n_win = cap // LANES
    spec = pl.BlockSpec((None, nb, LANES), lambda ei: (ei, 0, 0))
    lspec = pl.BlockSpec((None, n_win, 8, LANES), lambda ei: (ei, 0, 0, 0))
    shape = jax.ShapeDtypeStruct((e, nb, LANES), jnp.int32)
    pos, offs, idx, gate = pl.pallas_call(
        functools.partial(_route_kernel, cap=cap),
        grid=(e,),
        in_specs=[spec],
        out_specs=[spec, spec, lspec, lspec],
        out_shape=[shape, shape, jax.ShapeDtypeStruct((e, n_win, 8, LANES), jnp.int32),
                   jax.ShapeDtypeStruct((e, n_win, 8, LANES), jnp.float32)],
        compiler_params=_cparams(("parallel",)),
        name="route",
    )(aff3)
    return pos, offs, idx[:, :, 0, :].reshape(e, cap), gate[:, :, 0, :].reshape(e, cap)


COMBINE_TILE = 512
WINDOW = LANES
GROUP = 8
MAX_WINDOWS = N_EXPERTS * (COMBINE_TILE // WINDOW + 1)
NO_SLOT = -(1 << 20)


def _combine_kernel(first_ref, count_ref, x_ref, pos_ref, ye_hbm, o_ref, stage, sems, win_e, win_s):
    i = pl.program_id(0)

    def fetch(j, e, win):
        return pltpu.make_async_copy(ye_hbm.at[e, pl.ds(win * WINDOW, WINDOW), :], stage.at[j], sems.at[j])

    def issue(j, e, win, base):
        win_e[j] = e
        win_s[j] = base
        fetch(j, e, win).start()

    n_fetched = jnp.int32(0)
    for e in range(N_EXPERTS):
        first = first_ref[e, i]
        count = count_ref[e, i]

        def issue_one(k, j, e=e, first=first):
            issue(j, e, first + k, (first + k) * WINDOW)
            return j + 1

        n_fetched = lax.fori_loop(0, count, issue_one, n_fetched)
    n_groups = (n_fetched + GROUP - 1) // GROUP
    lax.fori_loop(n_fetched, n_groups * GROUP, lambda j, c: (issue(j, 0, 0, NO_SLOT), c)[1], 0)

    slot_iota = lax.broadcasted_iota(jnp.int32, (WINDOW, COMBINE_TILE), 0)

    def group_body(g, acc):
        onehots = []
        for u in range(GROUP):
            j = g * GROUP + u
            pos_row = pos_ref[win_e[j]]
            onehots.append((pos_row == win_s[j] + slot_iota).astype(jnp.bfloat16))
        onehot = jnp.concatenate(onehots, axis=0)
        for u in range(GROUP):
            fetch(g * GROUP + u, 0, 0).wait()
        rows = stage[pl.ds(g * GROUP, GROUP)].reshape(GROUP * WINDOW, D_MODEL)
        return acc + lax.dot_general(onehot, rows, (((0,), (0,)), ((), ())), preferred_element_type=jnp.float32)

    o_ref[...] = lax.fori_loop(0, n_groups, group_body, x_ref[...])


def _combine(xn, pos, offs, ye, cap):
    n = xn.shape[0]
    nt = n // COMBINE_TILE
    rows_per_tile = COMBINE_TILE // LANES
    start = offs[:, ::rows_per_tile, 0]
    stop = jnp.concatenate([start[:, 1:], jnp.full((N_EXPERTS, 1), cap, jnp.int32)], axis=1)
    first = start // WINDOW
    count = jnp.where(stop > start, (stop - 1) // WINDOW - first + 1, 0)
    pos_t = jnp.transpose(pos.reshape(N_EXPERTS, nt, COMBINE_TILE), (1, 0, 2)).reshape(nt, N_EXPERTS, 1, COMBINE_TILE)
    tok = pl.BlockSpec((COMBINE_TILE, D_MODEL), lambda i, f, c: (i, 0))
    return pl.pallas_call(
        _combine_kernel,
        grid_spec=pltpu.PrefetchScalarGridSpec(
            num_scalar_prefetch=2,
            grid=(nt,),
            in_specs=[tok,
                      pl.BlockSpec((None, N_EXPERTS, 1, COMBINE_TILE), lambda i, f, c: (i, 0, 0, 0)),
                      pl.BlockSpec(memory_space=pl.ANY)],
            out_specs=tok,
            scratch_shapes=[pltpu.VMEM((MAX_WINDOWS, WINDOW, D_MODEL), jnp.bfloat16),
                            pltpu.SemaphoreType.DMA((MAX_WINDOWS,)),
                            pltpu.SMEM((MAX_WINDOWS,), jnp.int32),
                            pltpu.SMEM((MAX_WINDOWS,), jnp.int32)],
        ),
        out_shape=jax.ShapeDtypeStruct((n, D_MODEL), jnp.float32),
        compiler_params=_cparams(("arbitrary",)),
        name="combine",
    )(first, count, xn, pos_t, ye)


def _pad_lora(w):
    z = jnp.zeros((N_DIR, N_DIR * w.shape[1], w.shape[2]), w.dtype)
    for d in range(N_DIR):
        z = z.at[d, d * w.shape[1]:(d + 1) * w.shape[1]].set(w[d])
    return z.astype(jnp.bfloat16)


def _layer_params(l, ln1_g, ln2_g, w_in, mu_prev, mu_next, q_norm_g, k_norm_g, w0, w2, a0, a2, g2, k_k, k_a,
                  r_k, lnx_g, lnx_b, w_out, router, w_gate, w_up, w_down):
    bf = lambda z: z.astype(jnp.bfloat16)
    row = lambda z: z.reshape(1, -1)
    return dict(
        ln1_g=ln1_g[l], ln2_g=row(ln2_g[l]), w_in=bf(w_in[l]), mu_prev=row(mu_prev[l]), mu_next=row(mu_next[l]),
        q_gain=row(jnp.tile(q_norm_g[l], ATT_SLOTS)), k_gain=row(jnp.tile(k_norm_g[l], ATT_SLOTS)),
        w0=w0[l], w2=_pad_lora(w2[l]), a0=a0[l], a2=_pad_lora(a2[l]), g2=bf(g2[l]),
        k_k=row(k_k[l]), k_a=row(k_a[l]), r_k=row(r_k[l]), lnx_g=row(lnx_g[l]), lnx_b=row(lnx_b[l]),
        w_out=bf(w_out[l]), router_t=router[l].T, w_gate=bf(w_gate[l]), w_up=bf(w_up[l]), w_down=bf(w_down[l]),
    )


def _encoder_layer(x, p, rope, bd):
    b, t, _ = x.shape
    n = b * t
    q, k, v, s = _in_proj(x, p['ln1_g'], p['w_in'], p['q_gain'], p['k_gain'], rope, bd)
    att_parts = [_band_attn(q[g], k[g], v[g]) for g in range(N_DIL)]
    lhs, add, bonus, gate = _rwkv_prep(
        s, p['mu_prev'], p['mu_next'], p['w0'], p['w2'], p['a0'], p['a2'], p['g2'], p['k_k'], p['k_a'], p['r_k'], bd)
    y = _rwkv_scan(lhs, add)
    xn, h2, aff_t = _out_proj(x, att_parts, y, bonus, gate, p['lnx_g'], p['lnx_b'], p['w_out'], p['ln2_g'],
                              p['router_t'], bd)
    cap = max(1, (EC_CAPACITY_FACTOR * n) // N_EXPERTS)
    pos, offs, idx, gate_e = _route(aff_t.reshape(N_EXPERTS, n // LANES, LANES), cap)
    xs = h2.reshape(n, D_MODEL)[idx]
    ye = _expert_ffn(xs, gate_e[..., None], p['w_gate'], p['w_up'], p['w_down'])
    return _combine(xn.reshape(n, D_MODEL), pos, offs, ye, cap).reshape(b, t, D_MODEL)


def kernel(x_prompt, x_sample, ln1_g, ln2_g, w_in, mu_prev, mu_next, q_norm_g, k_norm_g, w0, w2, a0, a2, g2,
           k_k, k_a, r_k, lnx_g, lnx_b, w_out, router, w_gate, w_up, w_down):
    depth = w_in.shape[0]
    bd = _head_sum_matrix()
    rope_p = _rope_tables(x_prompt.shape[1])
    rope_s = _rope_tables(x_sample.shape[1])
    y_prompt, y_sample = x_prompt, x_sample
    for l in range(depth):
        p = _layer_params(l, ln1_g, ln2_g, w_in, mu_prev, mu_next, q_norm_g, k_norm_g, w0, w2, a0, a2, g2, k_k,
                          k_a, r_k, lnx_g, lnx_b, w_out, router, w_gate, w_up, w_down)
        y_prompt = _encoder_layer(y_prompt, p, rope_p, bd)
        y_sample = _encoder_layer(y_sample, p, rope_s, bd)
    return (y_prompt, y_sample)
```

```python
import functools

import jax
import jax.numpy as jnp
import numpy as np
from jax import lax
from jax.experimental import pallas as pl
from jax.experimental.pallas import tpu as pltpu

D_MODEL = 1024
RW_HEADS = 4
HEAD_DIM = 64
RW_WIDTH = RW_HEADS * HEAD_DIM
DECAY_LORA = 64
AAA_LORA = 64
GATE_LORA = 128
N_DIR = 2
GN_EPS = 64e-5
ATT_SLOTS = 4
DILATIONS = (1, 4, 16)
BAND_HALF = 64
N_DIL = len(DILATIONS)
GROUP_WIDTH = ATT_SLOTS * HEAD_DIM
ATT_WIDTH = N_DIL * GROUP_WIDTH
ATT_COLS = 3 * ATT_WIDTH
SHIFT_COLS = 3 * RW_WIDTH + N_DIR * DECAY_LORA + N_DIR * AAA_LORA + GATE_LORA
IN_COLS = ATT_COLS + SHIFT_COLS
ROPE_THETA = 500000.0
ROPE_DIMS = HEAD_DIM // 4
ROPE_HALF = ROPE_DIMS // 2
N_EXPERTS = 16
EC_CAPACITY_FACTOR = 2
EXPERT_FF = 2048
NORM_EPS = 1e-6
NEG_INF = -1e30
LANES = 128

ROW_TILE = 512
ATT_Q_BLOCK = 128
CHUNK = 64
CHUNKS_PER_ITER = 4
FFN_ROW_TILE = 512
VMEM_LIMIT = 56 * 1024 * 1024

HIGHEST = lax.Precision.HIGHEST


def _cparams(sem):
    return pltpu.CompilerParams(dimension_semantics=sem, vmem_limit_bytes=VMEM_LIMIT)


def _head_sum_matrix():
    h = np.arange(GROUP_WIDTH) // HEAD_DIM
    return jnp.asarray((h[:, None] == h[None, :]).astype(np.float32), dtype=jnp.bfloat16)


def _head_sum(x, bd):
    hi = x.astype(jnp.bfloat16)
    lo = (x - hi.astype(jnp.float32)).astype(jnp.bfloat16)
    return (jnp.dot(hi, bd, preferred_element_type=jnp.float32)
            + jnp.dot(lo, bd, preferred_element_type=jnp.float32))


def _rope_tables(t):
    inv = jnp.power(jnp.float32(ROPE_THETA), -jnp.arange(ROPE_HALF, dtype=jnp.float32) * 2.0 / ROPE_DIMS)
    ang = jnp.arange(t, dtype=jnp.float32)[:, None] * inv[None, :]
    cos, sin = jnp.cos(ang), jnp.sin(ang)
    zeros = jnp.zeros((t, HEAD_DIM - ROPE_DIMS), jnp.float32)
    zero8 = jnp.zeros((t, ROPE_HALF), jnp.float32)
    c = jnp.concatenate([cos, cos, zeros + 1.0], -1)
    s1 = jnp.concatenate([zero8, sin, zeros], -1)
    s2 = jnp.concatenate([-sin, zero8, zeros], -1)
    tile = lambda z: jnp.tile(z, (1, ATT_SLOTS))
    return tile(c), tile(s1), tile(s2)


def _in_proj_kernel(x_ref, g_ref, w_ref, qg_ref, kg_ref, c_ref, s1_ref, s2_ref, bd_ref,
                    q0_ref, q1_ref, q2_ref, k0_ref, k1_ref, k2_ref, v0_ref, v1_ref, v2_ref, s_ref,
                    tmp_ref):
    x = x_ref[...]
    h = x * lax.rsqrt(jnp.mean(x * x, -1, keepdims=True) + NORM_EPS) * g_ref[...]
    hb = h.astype(jnp.bfloat16)
    bd = bd_ref[...]
    c, s1, s2 = c_ref[...], s1_ref[...], s2_ref[...]
    rows = x.shape[0]

    def deinterleave(val, out_ref, dil):
        if dil == 1:
            out_ref[0] = val.astype(out_ref.dtype)
            return
        for hf in range(GROUP_WIDTH // LANES):
            tmp_ref[hf] = val[:, hf * LANES:(hf + 1) * LANES]
        for r in range(dil):
            for hf in range(GROUP_WIDTH // LANES):
                out_ref[r, :, hf * LANES:(hf + 1) * LANES] = (
                    tmp_ref[hf, pl.ds(r, rows // dil, stride=dil), :].astype(out_ref.dtype))

    proj = jnp.dot(hb, w_ref[:, :ATT_COLS], preferred_element_type=jnp.float32)

    def normed_rope(col0, gain, scale):
        p = proj[:, col0:col0 + GROUP_WIDTH]
        ms = _head_sum(p * p, bd) * (1.0 / HEAD_DIM)
        n = p * lax.rsqrt(ms + NORM_EPS) * gain
        n = n * c + pltpu.roll(n, ROPE_HALF, 1) * s1 + pltpu.roll(n, GROUP_WIDTH - ROPE_HALF, 1) * s2
        return n * scale if scale != 1.0 else n

    q_refs = (q0_ref, q1_ref, q2_ref)
    k_refs = (k0_ref, k1_ref, k2_ref)
    v_refs = (v0_ref, v1_ref, v2_ref)
    for g, dil in enumerate(DILATIONS):
        deinterleave(normed_rope(g * GROUP_WIDTH, qg_ref[...], HEAD_DIM ** -0.5), q_refs[g], dil)
        deinterleave(normed_rope(ATT_WIDTH + g * GROUP_WIDTH, kg_ref[...], 1.0), k_refs[g], dil)
        col0 = 2 * ATT_WIDTH + g * GROUP_WIDTH
        deinterleave(proj[:, col0:col0 + GROUP_WIDTH], v_refs[g], dil)
    s_ref[...] = jnp.dot(hb, w_ref[:, ATT_COLS:], preferred_element_type=jnp.float32)


def _in_proj(x, ln1_g, w_in_bf16, q_gain, k_gain, rope, bd):
    b, t, _ = x.shape
    nt = t // ROW_TILE
    row2 = lambda bi, i: (0, 0)
    grp_shapes, grp_specs = [], []
    for _ in range(3):
        for dil in DILATIONS:
            grp_shapes.append(jax.ShapeDtypeStruct((b, dil, t // dil, GROUP_WIDTH), jnp.bfloat16))
            grp_specs.append(pl.BlockSpec((None, dil, ROW_TILE // dil, GROUP_WIDTH), lambda bi, i: (bi, 0, i, 0)))
    tab_spec = pl.BlockSpec((ROW_TILE, GROUP_WIDTH), lambda bi, i: (i, 0))
    outs = pl.pallas_call(
        _in_proj_kernel,
        grid=(b, nt),
        in_specs=[
            pl.BlockSpec((None, ROW_TILE, D_MODEL), lambda bi, i: (bi, i, 0)),
            pl.BlockSpec((1, D_MODEL), row2),
            pl.BlockSpec((D_MODEL, IN_COLS), row2),
            pl.BlockSpec((1, GROUP_WIDTH), row2),
            pl.BlockSpec((1, GROUP_WIDTH), row2),
            tab_spec, tab_spec, tab_spec,
            pl.BlockSpec((GROUP_WIDTH, GROUP_WIDTH), row2),
        ],
        out_specs=grp_specs + [pl.BlockSpec((None, ROW_TILE, SHIFT_COLS), lambda bi, i: (bi, i, 0))],
        out_shape=grp_shapes + [jax.ShapeDtypeStruct((b, t, SHIFT_COLS), jnp.float32)],
        scratch_shapes=[pltpu.VMEM((GROUP_WIDTH // LANES, ROW_TILE, LANES), jnp.float32)],
        compiler_params=_cparams(("parallel", "parallel")),
        name="in_proj",
    )(x, ln1_g.reshape(1, D_MODEL), w_in_bf16, q_gain, k_gain, *rope, bd)
    q = outs[0:3]
    k = outs[3:6]
    v = outs[6:9]
    return q, k, v, outs[9]


def _band_attn_kernel(q_ref, kp_ref, kc_ref, kn_ref, vp_ref, vc_ref, vn_ref, o_ref, lse_ref, *, seq_len, q_chunk):
    i = pl.program_id(1)
    kext = jnp.concatenate([kp_ref[...], kc_ref[...], kn_ref[...]], axis=0)
    vext = jnp.concatenate([vp_ref[...], vc_ref[...], vn_ref[...]], axis=0)
    kw = ATT_Q_BLOCK + 2 * BAND_HALF
    lane_head = lax.broadcasted_iota(jnp.int32, (ATT_Q_BLOCK, GROUP_WIDTH), 1) // HEAD_DIM
    qi = lax.broadcasted_iota(jnp.int32, (ATT_Q_BLOCK, kw), 0)
    kc = lax.broadcasted_iota(jnp.int32, (ATT_Q_BLOCK, kw), 1)
    diff = kc - BAND_HALF - qi
    band = (diff <= BAND_HALF) & (diff >= -BAND_HALF)
    for a in range(0, q_chunk, ATT_Q_BLOCK):
        kpos = i * q_chunk + (a - BAND_HALF) + kc
        valid = band & (kpos >= 0) & (kpos < seq_len)
        qb = q_ref[a:a + ATT_Q_BLOCK, :]
        kb = kext[a:a + kw]
        vb = vext[a:a + kw]
        heads = range(ATT_SLOTS)
        mine = [lane_head == h for h in heads]
        s = [lax.dot_general(jnp.where(mine[h], qb, jnp.zeros_like(qb)), kb, (((1,), (1,)), ((), ())),
                             preferred_element_type=jnp.float32) for h in heads]
        s = [jnp.where(valid, s_h, NEG_INF) for s_h in s]
        m = [jnp.max(s_h, -1, keepdims=True) for s_h in s]
        p = [jnp.exp(s_h - m_h) for s_h, m_h in zip(s, m)]
        den = [jnp.sum(p_h, -1, keepdims=True) for p_h in p]
        pv = [jnp.dot(p_h.astype(jnp.bfloat16), vb, preferred_element_type=jnp.float32) for p_h in p]
        acc = jnp.zeros((ATT_Q_BLOCK, GROUP_WIDTH), jnp.float32)
        lse = jnp.zeros((ATT_Q_BLOCK, GROUP_WIDTH), jnp.float32)
        for h in heads:
            acc = jnp.where(mine[h], pv[h] / den[h], acc)
            lse = jnp.where(mine[h], m[h] + jnp.log(den[h]), lse)
        o_ref[a:a + ATT_Q_BLOCK, :] = acc
        lse_ref[a:a + ATT_Q_BLOCK, :] = lse


def _band_attn(q, k, v):
    b, dil, seq_len, _ = q.shape
    ns = b * dil
    q, k, v = (z.reshape(ns, seq_len, GROUP_WIDTH) for z in (q, k, v))
    q_chunk = min(seq_len, 512)
    halo_per_chunk = q_chunk // BAND_HALF
    last_halo = seq_len // BAND_HALF - 1
    cur = pl.BlockSpec((None, q_chunk, GROUP_WIDTH), lambda s, i: (s, i, 0))
    prev = pl.BlockSpec((None, BAND_HALF, GROUP_WIDTH), lambda s, i: (s, jnp.maximum(i * halo_per_chunk - 1, 0), 0))
    nxt = pl.BlockSpec((None, BAND_HALF, GROUP_WIDTH),
                       lambda s, i: (s, jnp.minimum((i + 1) * halo_per_chunk, last_halo), 0))
    o, lse = pl.pallas_call(
        functools.partial(_band_attn_kernel, seq_len=seq_len, q_chunk=q_chunk),
        grid=(ns, seq_len // q_chunk),
        in_specs=[cur, prev, cur, nxt, prev, cur, nxt],
        out_specs=[cur, cur],
        out_shape=[jax.ShapeDtypeStruct((ns, seq_len, GROUP_WIDTH), jnp.float32)] * 2,
        compiler_params=_cparams(("parallel", "parallel")),
        name="band_attn",
    )(q, k, k, k, v, v, v)
    shape4 = (b, dil, seq_len, GROUP_WIDTH)
    return o.reshape(shape4), lse.reshape(shape4)


def _sigmoid(x):
    return 1.0 / (1.0 + jnp.exp(-x))


def _rwkv_prep_kernel(sp_ref, sc_ref, sn_ref, mup_ref, mun_ref, w0_ref, w2_ref, a0_ref, a2_ref, g2_ref,
                      kk_ref, ka_ref, rk_ref, bd_ref,
                      lhs_ref, add_ref, bonus_ref, gate_ref,
                      r_s, v_s, nkk_s, e_s, kd_s, beta_s):
    i = pl.program_id(1)
    last = pl.num_programs(1) - 1
    p = sc_ref[...]
    rows = p.shape[0]
    row = lax.broadcasted_iota(jnp.int32, p.shape, 0)
    halo_prev = sp_ref[7:8, :] * (i > 0).astype(jnp.float32)
    halo_next = sn_ref[0:1, :] * (i < last).astype(jnp.float32)
    p_prev = jnp.where(row == 0, halo_prev, pltpu.roll(p, 1, 0))
    p_next = jnp.where(row == rows - 1, halo_next, pltpu.roll(p, rows - 1, 0))
    u = p + mup_ref[...] * (p_prev - p) + mun_ref[...] * (p_next - p)

    bd = bd_ref[...]
    r = u[:, 0:RW_WIDTH]
    k = u[:, RW_WIDTH:2 * RW_WIDTH]
    v = u[:, 2 * RW_WIDTH:3 * RW_WIDTH]
    c0 = 3 * RW_WIDTH
    lw = jnp.tanh(u[:, c0:c0 + 128]).astype(jnp.bfloat16)
    la = u[:, c0 + 128:c0 + 256].astype(jnp.bfloat16)
    lg = _sigmoid(u[:, c0 + 256:c0 + 384]).astype(jnp.bfloat16)

    kk = k * kk_ref[...]
    kk = kk * lax.rsqrt(_head_sum(kk * kk, bd) + 1e-12)
    r_s[...] = r
    v_s[...] = v
    nkk_s[...] = -kk
    gate_ref[...] = jnp.dot(lg, g2_ref[...], preferred_element_type=jnp.float32)
    bonus = jnp.zeros_like(v)
    for d in range(N_DIR):
        z = w0_ref[d:d + 1, :] + jnp.dot(lw, w2_ref[d], preferred_element_type=jnp.float32)
        e_s[d] = _sigmoid(z) * float(np.exp(-0.5))
        a = _sigmoid(a0_ref[d:d + 1, :] + jnp.dot(la, a2_ref[d], preferred_element_type=jnp.float32))
        kd = k * (1.0 + (a - 1.0) * ka_ref[...])
        kd_s[d] = kd
        beta_s[d] = kk * a
        bonus = bonus + _head_sum(r * kd * rk_ref[...], bd) * v
    bonus_ref[...] = bonus

    idx_t = lax.broadcasted_iota(jnp.int32, (CHUNK, CHUNK), 0)
    idx_s = lax.broadcasted_iota(jnp.int32, (CHUNK, CHUNK), 1)
    eye = (idx_t == idx_s).astype(jnp.float32)
    bf = lambda z: z.astype(jnp.bfloat16)
    mm = lambda a_, b_: jnp.dot(bf(a_), bf(b_), preferred_element_type=jnp.float32)
    mm_nt = lambda a_, b_: lax.dot_general(bf(a_), bf(b_), (((1,), (1,)), ((), ())), preferred_element_type=jnp.float32)
    mm_tn = lambda a_, b_: lax.dot_general(bf(a_), bf(b_), (((0,), (0,)), ((), ())), preferred_element_type=jnp.float32)

    def chunk_body(cp, carry):
        units = []
        for c, d in [(cp * CHUNKS_PER_ITER + j, d_) for j in range(CHUNKS_PER_ITER) for d_ in range(N_DIR)]:
            rows_c = pl.ds(pl.multiple_of(c * CHUNK, CHUNK), CHUNK)
            r_c, v_c, nkk_c = r_s[rows_c, :], v_s[rows_c, :], nkk_s[rows_c, :]
            before = (idx_s < idx_t) if d == 0 else (idx_s > idx_t)
            before_eq = (idx_s <= idx_t) if d == 0 else (idx_s >= idx_t)
            last = CHUNK - 1 if d == 0 else 0
            e_c = e_s[d, rows_c, :]
            cl = jnp.dot(before_eq.astype(jnp.float32), e_c, precision=HIGHEST, preferred_element_type=jnp.float32)
            w_inc = jnp.exp(-cl)
            w_exc = jnp.exp(e_c - cl)
            w_inv = jnp.exp(cl)
            w_tot = w_inc[last:last + 1, :]
            a_t = nkk_c * w_exc
            r_t = r_c * w_inc
            b_h = beta_s[d, rows_c, :] * w_inv
            k_h = kd_s[d, rows_c, :] * w_inv
            b_t = b_h * w_tot
            k_t = k_h * w_tot
            for h in range(RW_HEADS):
                hs = slice(h * HEAD_DIM, (h + 1) * HEAD_DIM)
                units.append(dict(chunk=c, dir=d, before=before, before_eq=before_eq, a_t=a_t[:, hs], r_t=r_t[:, hs], b_h=b_h[:, hs],
                                  k_h=k_h[:, hs], b_t=b_t[:, hs], k_t=k_t[:, hs], w_tot=w_tot[:, hs], v=v_c[:, hs]))
        each = lambda fn: [fn(un) for un in units]
        g = each(lambda un: mm_nt(jnp.concatenate([un['a_t'], un['r_t']], axis=0),
                                  jnp.concatenate([un['b_h'], un['k_h']], axis=0)))
        for un, g_u in zip(units, g):
            un['m_ab'] = jnp.where(un['before'], g_u[:CHUNK, :CHUNK], 0.0)
            un['m_ak'] = jnp.where(un['before'], g_u[:CHUNK, CHUNK:], 0.0)
            un['n_rb'] = jnp.where(un['before_eq'], g_u[CHUNK:, :CHUNK], 0.0)
            un['n_rk'] = jnp.where(un['before_eq'], g_u[CHUNK:, CHUNK:], 0.0)
        t_inv = each(lambda un: eye + un['m_ab'])
        pw = each(lambda un: un['m_ab'])
        for _ in range(CHUNK.bit_length() - 2):
            pw = [mm(p_, p_) for p_ in pw]
            t_inv = [t_ + mm(t_, p_) for t_, p_ in zip(t_inv, pw)]
        mv = each(lambda un: mm(un['m_ak'], un['v']))
        tx = [mm(t_, jnp.concatenate([un['a_t'], mv_], axis=1)) for t_, un, mv_ in zip(t_inv, units, mv)]
        zero = jnp.zeros((CHUNK, HEAD_DIM), jnp.float32)
        z = [jnp.concatenate([tx_, jnp.concatenate([zero, un['v']], axis=1)], axis=0)
             for un, tx_ in zip(units, tx)]
        nx = [mm(jnp.concatenate([un['n_rb'], un['n_rk']], axis=1), z_) for un, z_ in zip(units, z)]
        bx = [mm_tn(jnp.concatenate([un['b_t'], un['k_t']], axis=0), z_) for un, z_ in zip(units, z)]
        lhs_parts, add_parts = [], []
        for un, nx_, bx_ in zip(units, nx, bx):
            ra = un['r_t'] + nx_[:, :HEAD_DIM]
            y0 = nx_[:, HEAD_DIM:]
            phi = eye * un['w_tot'] + bx_[:, :HEAD_DIM]
            psi = bx_[:, HEAD_DIM:]
            lhs_parts.append(jnp.concatenate([ra, phi], axis=0))
            add_parts.append(jnp.concatenate([y0, psi], axis=0))
        for g0 in range(0, len(units), RW_HEADS):
            c, d = units[g0]['chunk'], units[g0]['dir']
            lhs_ref[d, c] = jnp.concatenate(lhs_parts[g0:g0 + RW_HEADS], axis=1)
            add_ref[d, c] = jnp.concatenate(add_parts[g0:g0 + RW_HEADS], axis=1)
        return carry

    lax.fori_loop(0, rows // (CHUNK * CHUNKS_PER_ITER), chunk_body, 0)


def _rwkv_prep(s, mu_prev, mu_next, w0, w2p, a0, a2p, g2, k_k, k_a, r_k, bd):
    b, t, _ = s.shape
    nt = t // ROW_TILE
    halo_per_tile = ROW_TILE // 8
    last_halo = t // 8 - 1
    row2 = lambda bi, i: (0, 0)
    row3 = lambda bi, i: (0, 0, 0)
    vec = lambda n: pl.BlockSpec((1, n), row2)
    chunks_per_tile = ROW_TILE // CHUNK
    cspec = pl.BlockSpec((N_DIR, None, chunks_per_tile, 2 * CHUNK, RW_WIDTH), lambda bi, i: (0, bi, i, 0, 0))
    nspec = pl.BlockSpec((None, ROW_TILE, RW_WIDTH), lambda bi, i: (bi, i, 0))
    cshape = jax.ShapeDtypeStruct((N_DIR, b, t // CHUNK, 2 * CHUNK, RW_WIDTH), jnp.float32)
    nshape = jax.ShapeDtypeStruct((b, t, RW_WIDTH), jnp.float32)
    tile_f32 = pltpu.VMEM((ROW_TILE, RW_WIDTH), jnp.float32)
    tile2_f32 = pltpu.VMEM((N_DIR, ROW_TILE, RW_WIDTH), jnp.float32)
    return pl.pallas_call(
        _rwkv_prep_kernel,
        grid=(b, nt),
        in_specs=[
            pl.BlockSpec((None, 8, SHIFT_COLS), lambda bi, i: (bi, jnp.maximum(i * halo_per_tile - 1, 0), 0)),
            pl.BlockSpec((None, ROW_TILE, SHIFT_COLS), lambda bi, i: (bi, i, 0)),
            pl.BlockSpec((None, 8, SHIFT_COLS), lambda bi, i: (bi, jnp.minimum((i + 1) * halo_per_tile, last_halo), 0)),
            vec(SHIFT_COLS), vec(SHIFT_COLS),
            pl.BlockSpec((N_DIR, RW_WIDTH), row2),
            pl.BlockSpec((N_DIR, 128, RW_WIDTH), row3),
            pl.BlockSpec((N_DIR, RW_WIDTH), row2),
            pl.BlockSpec((N_DIR, 128, RW_WIDTH), row3),
            pl.BlockSpec((GATE_LORA, RW_WIDTH), row2),
            vec(RW_WIDTH), vec(RW_WIDTH), vec(RW_WIDTH),
            pl.BlockSpec((GROUP_WIDTH, GROUP_WIDTH), row2),
        ],
        out_specs=[cspec, cspec, nspec, nspec],
        out_shape=[cshape, cshape, nshape, nshape],
        scratch_shapes=[tile_f32, tile_f32, tile_f32, tile2_f32, tile2_f32, tile2_f32],
        compiler_params=_cparams(("parallel", "parallel")),
        name="rwkv_prep",
    )(s, s, s, mu_prev, mu_next, w0, w2p, a0, a2p, g2, k_k, k_a, r_k, bd)


def _split_bf16(x):
    hi = x.astype(jnp.bfloat16)
    return hi, (x - hi.astype(jnp.float32)).astype(jnp.bfloat16)


def _rwkv_scan_kernel(lhs_f, add_f, lhs_b, add_b, y_f, y_b, state_ref):
    n_chunks = lhs_f.shape[0]

    @pl.when(pl.program_id(1) == 0)
    def _():
        state_ref[...] = jnp.zeros_like(state_ref)

    dot = lambda a_, b_: jnp.dot(a_, b_, preferred_element_type=jnp.float32)

    def chunk_body(cc, carry):
        chains = []
        for d, (lhs_ref, add_ref, c) in enumerate(((lhs_f, add_f, cc), (lhs_b, add_b, n_chunks - 1 - cc))):
            for h in range(RW_HEADS):
                hs = slice(h * HEAD_DIM, (h + 1) * HEAD_DIM)
                chains.append((d, h, c, hs, lhs_ref, add_ref))
        lsplit = [_split_bf16(lhs_ref[c, :, hs]) for (_, _, c, hs, lhs_ref, _) in chains]
        ssplit = [_split_bf16(state_ref[d, h]) for (d, h, *_) in chains]
        res = [dot(l_hi, s_hi) + dot(l_lo, s_hi) + dot(l_hi, s_lo) for (l_hi, l_lo), (s_hi, s_lo) in zip(lsplit, ssplit)]
        res = [r_ + add_ref[c, :, hs] for r_, (_, _, c, hs, _, add_ref) in zip(res, chains)]
        for r_, (d, h, *_) in zip(res, chains):
            state_ref[d, h] = r_[CHUNK:]
        for d, (y_ref, c) in enumerate(((y_f, cc), (y_b, n_chunks - 1 - cc))):
            rows = pl.ds(pl.multiple_of(c * CHUNK, CHUNK), CHUNK)
            y_ref[rows, :] = jnp.concatenate([r_[:CHUNK] for r_ in res[d * RW_HEADS:(d + 1) * RW_HEADS]], axis=1)
        return carry

    lax.fori_loop(0, n_chunks, chunk_body, 0)


def _rwkv_scan(lhs, add):
    _, b, nchunk, _, _ = lhs.shape
    per_step = ROW_TILE // CHUNK
    nc = nchunk // per_step
    fwd = pl.BlockSpec((None, None, per_step, 2 * CHUNK, RW_WIDTH), lambda bi, c: (0, bi, c, 0, 0))
    bwd = pl.BlockSpec((None, None, per_step, 2 * CHUNK, RW_WIDTH), lambda bi, c: (1, bi, nc - 1 - c, 0, 0))
    y_shape = jax.ShapeDtypeStruct((b, nchunk * CHUNK, RW_WIDTH), jnp.float32)
    return pl.pallas_call(
        _rwkv_scan_kernel,
        grid=(b, nc),
        in_specs=[fwd, fwd, bwd, bwd],
        out_specs=[pl.BlockSpec((None, ROW_TILE, RW_WIDTH), lambda bi, c: (bi, c, 0)),
                   pl.BlockSpec((None, ROW_TILE, RW_WIDTH), lambda bi, c: (bi, nc - 1 - c, 0))],
        out_shape=[y_shape, y_shape],
        scratch_shapes=[pltpu.VMEM((N_DIR, RW_HEADS, HEAD_DIM, HEAD_DIM), jnp.float32)],
        compiler_params=_cparams(("parallel", "arbitrary")),
        name="rwkv_scan",
    )(lhs, add, lhs, add)


def _out_proj_kernel(x_ref, o0_ref, l0_ref, o1_ref, l1_ref, o2_ref, l2_ref, yf_ref, yb_ref, bonus_ref, gate_ref,
                     lnxg_ref, lnxb_ref, wout_ref, ln2_ref, router_ref, bd_ref,
                     xn_ref, h2_ref, aff_ref, so1_ref, sl1_ref, so2_ref, sl2_ref):
    rows = x_ref.shape[0]

    def interleave(src_ref, dst_ref, dil):
        halves = range(GROUP_WIDTH // LANES)
        for r in range(dil):
            for hf in halves:
                dst_ref[hf, pl.ds(r, rows // dil, stride=dil), :] = src_ref[r, :, hf * LANES:(hf + 1) * LANES]
        return jnp.concatenate([dst_ref[hf] for hf in halves], axis=-1)

    o1 = interleave(o1_ref, so1_ref, DILATIONS[1])
    l1 = interleave(l1_ref, sl1_ref, DILATIONS[1])
    o2 = interleave(o2_ref, so2_ref, DILATIONS[2])
    l2 = interleave(l2_ref, sl2_ref, DILATIONS[2])
    l0 = l0_ref[0]
    m = jnp.maximum(jnp.maximum(l0, l1), l2)
    e0, e1, e2 = jnp.exp(l0 - m), jnp.exp(l1 - m), jnp.exp(l2 - m)
    att = (e0 * o0_ref[0] + e1 * o1 + e2 * o2) / (e0 + e1 + e2)

    bd = bd_ref[...]
    y = yf_ref[...] + yb_ref[...]
    mu = _head_sum(y, bd) * (1.0 / HEAD_DIM)
    dlt = y - mu
    var = _head_sum(dlt * dlt, bd) * (1.0 / HEAD_DIM)
    yn = dlt * lax.rsqrt(var + GN_EPS) * lnxg_ref[...] + lnxb_ref[...]
    rw = (yn + bonus_ref[...]) * gate_ref[...]

    mixed = (jnp.dot(att.astype(jnp.bfloat16), wout_ref[0:GROUP_WIDTH, :], preferred_element_type=jnp.float32)
             + jnp.dot(rw.astype(jnp.bfloat16), wout_ref[GROUP_WIDTH:, :], preferred_element_type=jnp.float32))
    xn = x_ref[...] + mixed
    xn_ref[...] = xn
    h2 = xn * lax.rsqrt(jnp.mean(xn * xn, -1, keepdims=True) + NORM_EPS) * ln2_ref[...]
    h2_ref[...] = h2.astype(h2_ref.dtype)
    logits = lax.dot_general(router_ref[...], h2, (((1,), (1,)), ((), ())),
                             precision=HIGHEST, preferred_element_type=jnp.float32)
    ex = jnp.exp(logits - jnp.max(logits, 0, keepdims=True))
    aff_ref[...] = ex / jnp.sum(ex, 0, keepdims=True)


def _out_proj(x, att_parts, y, bonus, gate, lnx_g, lnx_b, w_out_bf16, ln2_g, router_t, bd):
    b, t, _ = x.shape
    nt = t // ROW_TILE
    row2 = lambda bi, i: (0, 0)
    tok = lambda w: pl.BlockSpec((None, ROW_TILE, w), lambda bi, i: (bi, i, 0))
    grp = lambda dil: pl.BlockSpec((None, dil, ROW_TILE // dil, GROUP_WIDTH), lambda bi, i: (bi, 0, i, 0))
    vec = lambda n: pl.BlockSpec((1, n), row2)
    grp_specs, grp_args = [], []
    for (o, lse), dil in zip(att_parts, DILATIONS):
        grp_specs += [grp(dil), grp(dil)]
        grp_args += [o, lse]
    return pl.pallas_call(
        _out_proj_kernel,
        grid=(b, nt),
        in_specs=[tok(D_MODEL)] + grp_specs + [
            tok(RW_WIDTH), tok(RW_WIDTH), tok(RW_WIDTH), tok(RW_WIDTH), vec(RW_WIDTH), vec(RW_WIDTH),
            pl.BlockSpec((GROUP_WIDTH + RW_WIDTH, D_MODEL), row2),
            vec(D_MODEL),
            pl.BlockSpec((N_EXPERTS, D_MODEL), row2),
            pl.BlockSpec((GROUP_WIDTH, GROUP_WIDTH), row2),
        ],
        out_specs=[tok(D_MODEL), tok(D_MODEL),
                   pl.BlockSpec((N_EXPERTS, ROW_TILE), lambda bi, i: (0, bi * nt + i))],
        out_shape=[jax.ShapeDtypeStruct((b, t, D_MODEL), jnp.float32),
                   jax.ShapeDtypeStruct((b, t, D_MODEL), jnp.bfloat16),
                   jax.ShapeDtypeStruct((N_EXPERTS, b * t), jnp.float32)],
        scratch_shapes=[pltpu.VMEM((GROUP_WIDTH // LANES, ROW_TILE, LANES), jnp.float32)] * 4,
        compiler_params=_cparams(("parallel", "parallel")),
        name="out_proj",
    )(x, *grp_args, *y, bonus, gate, lnx_g, lnx_b, w_out_bf16, ln2_g, router_t, bd)


FF_CHUNK = 512


def _expert_ffn_kernel(x_ref, gate_ref, wg_ref, wu_ref, wd_ref, o_ref):
    x = x_ref[...]
    acc = jnp.zeros(o_ref.shape, jnp.float32)
    for f0 in range(0, EXPERT_FF, FF_CHUNK):
        hg = jnp.dot(x, wg_ref[:, f0:f0 + FF_CHUNK], preferred_element_type=jnp.float32)
        hu = jnp.dot(x, wu_ref[:, f0:f0 + FF_CHUNK], preferred_element_type=jnp.float32)
        hid = (hg * _sigmoid(hg) * hu).astype(jnp.bfloat16)
        acc = acc + jnp.dot(hid, wd_ref[f0:f0 + FF_CHUNK, :], preferred_element_type=jnp.float32)
    o_ref[...] = (acc * gate_ref[...]).astype(o_ref.dtype)


def _expert_ffn(xs, gate, wg, wu, wd):
    e, cap, _ = xs.shape
    tile = min(FFN_ROW_TILE, cap)
    tok = lambda w: pl.BlockSpec((None, tile, w), lambda ei, i: (ei, i, 0))
    wspec = lambda r, c: pl.BlockSpec((None, r, c), lambda ei, i: (ei, 0, 0))
    return pl.pallas_call(
        _expert_ffn_kernel,
        grid=(e, cap // tile),
        in_specs=[tok(D_MODEL), tok(1), wspec(D_MODEL, EXPERT_FF), wspec(D_MODEL, EXPERT_FF), wspec(EXPERT_FF, D_MODEL)],
        out_specs=tok(D_MODEL),
        out_shape=jax.ShapeDtypeStruct((e, cap, D_MODEL), jnp.bfloat16),
        compiler_params=_cparams(("parallel", "arbitrary")),
        name="expert_ffn",
    )(xs, gate, wg, wu, wd)


ONE_BITS_PLUS_1 = 0x3F800001
LIST_ROWS = 16


def _prefix_counts(mask_bf16, tri_ones, strict_lower):
    t = jnp.dot(mask_bf16, tri_ones, preferred_element_type=jnp.float32)
    offs = jnp.dot(strict_lower, t[:, LANES:].astype(jnp.bfloat16), preferred_element_type=jnp.float32)
    return t[:, :LANES] + offs, offs, t[:, LANES:]


def _route_kernel(aff_ref, pos_ref, offs_ref, idx_ref, gate_ref, *, cap):
    x = pltpu.bitcast(aff_ref[...], jnp.int32)
    nb = x.shape[0]

    def count_ge(thr):
        hit = (x >= thr).astype(jnp.int32)
        return jnp.sum(jnp.sum(hit, axis=0, keepdims=True), axis=1, keepdims=True)

    def bisect(_, bounds):
        lo, hi = bounds
        mid = lo + ((hi - lo) >> 1)
        ok = count_ge(mid) >= cap
        return jnp.where(ok, mid, lo), jnp.where(ok, hi, mid)

    lo0 = jnp.zeros((1, 1), jnp.int32)
    hi0 = jnp.full((1, 1), ONE_BITS_PLUS_1, jnp.int32)
    thr, _ = lax.fori_loop(0, 31, bisect, (lo0, hi0))
    above = x > thr
    tie = x == thr
    need = cap - count_ge(thr + 1)

    r_i = lax.broadcasted_iota(jnp.int32, (LANES, 2 * LANES), 0)
    c_i = lax.broadcasted_iota(jnp.int32, (LANES, 2 * LANES), 1)
    tri_ones = ((c_i >= LANES) | (r_i <= c_i)).astype(jnp.bfloat16)
    rr = lax.broadcasted_iota(jnp.int32, (nb, nb), 0)
    cc = lax.broadcasted_iota(jnp.int32, (nb, nb), 1)
    strict_lower = (cc < rr).astype(jnp.bfloat16)

    tie_f = tie.astype(jnp.float32)
    tie_incl, _, _ = _prefix_counts(tie_f.astype(jnp.bfloat16), tri_ones, strict_lower)
    sel = above | (tie & ((tie_incl - tie_f) < need.astype(jnp.float32)))
    sel_incl, offs, row_total = _prefix_counts(sel.astype(jnp.bfloat16), tri_ones, strict_lower)
    in_row = sel_incl - offs
    row_end = offs + row_total
    pos_ref[...] = jnp.where(sel, sel_incl.astype(jnp.int32) - 1, -1)
    offs_ref[...] = offs.astype(jnp.int32)

    aff_t = aff_ref[...].T
    g_hi = aff_t.astype(jnp.bfloat16)
    g_mid = (aff_t - g_hi.astype(jnp.float32)).astype(jnp.bfloat16)
    g_lo = (aff_t - g_hi.astype(jnp.float32) - g_mid.astype(jnp.float32)).astype(jnp.bfloat16)
    row_id = lax.broadcasted_iota(jnp.int32, (1, nb), 1)
    offs_t = offs.T[0:1, :].astype(jnp.int32)
    facts = jnp.concatenate([row_id >> 4, row_id & 15, offs_t >> 7, offs_t & (LANES - 1)], axis=0)
    facts = jnp.concatenate([facts.astype(jnp.float32), jnp.zeros((LIST_ROWS - 4, nb), jnp.float32)], axis=0)
    stack = jnp.concatenate([in_row.T.astype(jnp.bfloat16), facts.astype(jnp.bfloat16), g_hi, g_mid, g_lo], axis=0)
    sub = lax.broadcasted_iota(jnp.int32, (LANES, LANES), 0).astype(jnp.float32)

    def window_pair(wp, carry):
        ws = [wp * 2, wp * 2 + 1]
        slots = [(w * LANES + lax.broadcasted_iota(jnp.int32, (1, LANES), 1)).astype(jnp.float32) for w in ws]
        onehots = [((offs <= s_) & (s_ < row_end)).astype(jnp.bfloat16) for s_ in slots]
        gots = [jnp.dot(stack, oh, preferred_element_type=jnp.float32) for oh in onehots]
        for w, slot, got in zip(ws, slots, gots):
            emit(w, slot, got)
        return carry

    def emit(w, slot, got):
        counts = got[:LANES]
        fact = got[LANES:LANES + LIST_ROWS]
        gates = got[LANES + LIST_ROWS:LANES + LIST_ROWS + LANES] + got[2 * LANES + LIST_ROWS:3 * LANES + LIST_ROWS] \
            + got[3 * LANES + LIST_ROWS:]
        row = fact[0:1] * 16.0 + fact[1:2]
        local = slot - (fact[2:3] * float(LANES) + fact[3:4])
        lane = jnp.sum((counts <= local).astype(jnp.float32), axis=0, keepdims=True)
        idx = (row * float(LANES) + lane).astype(jnp.int32)
        gate = jnp.sum(jnp.where(sub == lane, gates, 0.0), axis=0, keepdims=True)
        idx_ref[w] = jnp.broadcast_to(idx, (8, LANES))
        gate_ref[w] = jnp.broadcast_to(gate, (8, LANES))

    lax.fori_loop(0, cap // (2 * LANES), window_pair, 0)


def _route(aff3, cap):
    e, nb, _ = aff3.shape
    n_win = cap // LANES
    spec = pl.BlockSpec((None, nb, LANES), lambda ei: (ei, 0, 0))
    lspec = pl.BlockSpec((None, n_win, 8, LANES), lambda ei: (ei, 0, 0, 0))
    shape = jax.ShapeDtypeStruct((e, nb, LANES), jnp.int32)
    pos, offs, idx, gate = pl.pallas_call(
        functools.partial(_route_kernel, cap=cap),
        grid=(e,),
        in_specs=[spec],
        out_specs=[spec, spec, lspec, lspec],
        out_shape=[shape, shape, jax.ShapeDtypeStruct((e, n_win, 8, LANES), jnp.int32),
                   jax.ShapeDtypeStruct((e, n_win, 8, LANES), jnp.float32)],
        compiler_params=_cparams(("parallel",)),
        name="route",
    )(aff3)
    return pos, offs, idx[:, :, 0, :].reshape(e, cap), gate[:, :, 0, :].reshape(e, cap)


COMBINE_TILE = 512
WINDOW = 64
GROUP = 16
MAX_WINDOWS = N_EXPERTS * (COMBINE_TILE // WINDOW + 1)
NO_SLOT = -(1 << 20)


def _combine_kernel(first_ref, count_ref, x_ref, pos_ref, ye_hbm, o_ref, stage, sems, win_e, win_s, n_win):
    i = pl.program_id(0)
    half = i % 2

    def fetch(h, j, e, win):
        return pltpu.make_async_copy(ye_hbm.at[e, pl.ds(win * WINDOW, WINDOW), :], stage.at[h, j], sems.at[h, j])

    def request_tile(tile, h):
        n_req = jnp.int32(0)
        for e in range(N_EXPERTS):
            first = first_ref[e, tile]

            def request_one(k, j, e=e, first=first):
                win_e[h, j] = e
                win_s[h, j] = (first + k) * WINDOW
                fetch(h, j, e, first + k).start()
                return j + 1

            n_req = lax.fori_loop(0, count_ref[e, tile], request_one, n_req)
        n_win[h] = n_req

    @pl.when(i == 0)
    def _():
        stage[...] = jnp.zeros_like(stage)
        request_tile(0, 0)

    @pl.when(i + 1 < pl.num_programs(0))
    def _():
        request_tile(i + 1, 1 - half)

    n_here = n_win[half]
    slot_iota = lax.broadcasted_iota(jnp.int32, (WINDOW, COMBINE_TILE), 0)

    def group_body(g, acc):
        onehots = []
        for u in range(GROUP):
            j = g * GROUP + u
            live = j < n_here
            pos_row = pos_ref[jnp.where(live, win_e[half, j], 0)]
            base = jnp.where(live, win_s[half, j], NO_SLOT)
            onehots.append((pos_row == base + slot_iota).astype(jnp.bfloat16))
        onehot = jnp.concatenate(onehots, axis=0)
        for u in range(GROUP):
            j = g * GROUP + u

            @pl.when(j < n_here)
            def _(j=j):
                fetch(half, j, 0, 0).wait()

        rows = stage[half, pl.ds(g * GROUP, GROUP)].reshape(GROUP * WINDOW, D_MODEL)
        return acc + lax.dot_general(onehot, rows, (((0,), (0,)), ((), ())), preferred_element_type=jnp.float32)

    o_ref[...] = lax.fori_loop(0, (n_here + GROUP - 1) // GROUP, group_body, x_ref[...])


def _combine(xn, pos, offs, ye, cap):
    n = xn.shape[0]
    nt = n // COMBINE_TILE
    rows_per_tile = COMBINE_TILE // LANES
    start = offs[:, ::rows_per_tile, 0]
    stop = jnp.concatenate([start[:, 1:], jnp.full((N_EXPERTS, 1), cap, jnp.int32)], axis=1)
    first = start // WINDOW
    count = jnp.where(stop > start, (stop - 1) // WINDOW - first + 1, 0)
    pos_t = jnp.transpose(pos.reshape(N_EXPERTS, nt, COMBINE_TILE), (1, 0, 2)).reshape(nt, N_EXPERTS, 1, COMBINE_TILE)
    tok = pl.BlockSpec((COMBINE_TILE, D_MODEL), lambda i, f, c: (i, 0))
    return pl.pallas_call(
        _combine_kernel,
        grid_spec=pltpu.PrefetchScalarGridSpec(
            num_scalar_prefetch=2,
            grid=(nt,),
            in_specs=[tok,
                      pl.BlockSpec((None, N_EXPERTS, 1, COMBINE_TILE), lambda i, f, c: (i, 0, 0, 0)),
                      pl.BlockSpec(memory_space=pl.ANY)],
            out_specs=tok,
            scratch_shapes=[pltpu.VMEM((2, MAX_WINDOWS, WINDOW, D_MODEL), jnp.bfloat16),
                            pltpu.SemaphoreType.DMA((2, MAX_WINDOWS)),
                            pltpu.SMEM((2, MAX_WINDOWS), jnp.int32),
                            pltpu.SMEM((2, MAX_WINDOWS), jnp.int32),
                            pltpu.SMEM((2,), jnp.int32)],
        ),
        out_shape=jax.ShapeDtypeStruct((n, D_MODEL), jnp.float32),
        compiler_params=_cparams(("arbitrary",)),
        name="combine",
    )(first, count, xn, pos_t, ye)


def _pad_lora(w):
    z = jnp.zeros((N_DIR, N_DIR * w.shape[1], w.shape[2]), w.dtype)
    for d in range(N_DIR):
        z = z.at[d, d * w.shape[1]:(d + 1) * w.shape[1]].set(w[d])
    return z.astype(jnp.bfloat16)


def _layer_params(l, ln1_g, ln2_g, w_in, mu_prev, mu_next, q_norm_g, k_norm_g, w0, w2, a0, a2, g2, k_k, k_a,
                  r_k, lnx_g, lnx_b, w_out, router, w_gate, w_up, w_down):
    bf = lambda z: z.astype(jnp.bfloat16)
    row = lambda z: z.reshape(1, -1)
    return dict(
        ln1_g=ln1_g[l], ln2_g=row(ln2_g[l]), w_in=bf(w_in[l]), mu_prev=row(mu_prev[l]), mu_next=row(mu_next[l]),
        q_gain=row(jnp.tile(q_norm_g[l], ATT_SLOTS)), k_gain=row(jnp.tile(k_norm_g[l], ATT_SLOTS)),
        w0=w0[l], w2=_pad_lora(w2[l]), a0=a0[l], a2=_pad_lora(a2[l]), g2=bf(g2[l]),
        k_k=row(k_k[l]), k_a=row(k_a[l]), r_k=row(r_k[l]), lnx_g=row(lnx_g[l]), lnx_b=row(lnx_b[l]),
        w_out=bf(w_out[l]), router_t=router[l].T, w_gate=bf(w_gate[l]), w_up=bf(w_up[l]), w_down=bf(w_down[l]),
    )


def _encoder_layer(x, p, rope, bd):
    b, t, _ = x.shape
    n = b * t
    q, k, v, s = _in_proj(x, p['ln1_g'], p['w_in'], p['q_gain'], p['k_gain'], rope, bd)
    att_parts = [_band_attn(q[g], k[g], v[g]) for g in range(N_DIL)]
    lhs, add, bonus, gate = _rwkv_prep(
        s, p['mu_prev'], p['mu_next'], p['w0'], p['w2'], p['a0'], p['a2'], p['g2'], p['k_k'], p['k_a'], p['r_k'], bd)
    y = _rwkv_scan(lhs, add)
    xn, h2, aff_t = _out_proj(x, att_parts, y, bonus, gate, p['lnx_g'], p['lnx_b'], p['w_out'], p['ln2_g'],
                              p['router_t'], bd)
    cap = max(1, (EC_CAPACITY_FACTOR * n) // N_EXPERTS)
    pos, offs, idx, gate_e = _route(aff_t.reshape(N_EXPERTS, n // LANES, LANES), cap)
    xs = h2.reshape(n, D_MODEL)[idx]
    ye = _expert_ffn(xs, gate_e[..., None], p['w_gate'], p['w_up'], p['w_down'])
    return _combine(xn.reshape(n, D_MODEL), pos, offs, ye, cap).reshape(b, t, D_MODEL)


def kernel(x_prompt, x_sample, ln1_g, ln2_g, w_in, mu_prev, mu_next, q_norm_g, k_norm_g, w0, w2, a0, a2, g2,
           k_k, k_a, r_k, lnx_g, lnx_b, w_out, router, w_gate, w_up, w_down):
    depth = w_in.shape[0]
    bd = _head_sum_matrix()
    rope_p = _rope_tables(x_prompt.shape[1])
    rope_s = _rope_tables(x_sample.shape[1])
    y_prompt, y_sample = x_prompt, x_sample
    for l in range(depth):
        p = _layer_params(l, ln1_g, ln2_g, w_in, mu_prev, mu_next, q_norm_g, k_norm_g, w0, w2, a0, a2, g2, k_k,
                          k_a, r_k, lnx_g, lnx_b, w_out, router, w_gate, w_up, w_down)
        y_prompt = _encoder_layer(y_prompt, p, rope_p, bd)
        y_sample = _encoder_layer(y_sample, p, rope_s, bd)
    return (y_prompt, y_sample)
```

```python
import functools

import jax
import jax.numpy as jnp
import numpy as np
from jax import lax
from jax.experimental import pallas as pl
from jax.experimental.pallas import tpu as pltpu

D_MODEL = 1024
RW_HEADS = 4
HEAD_DIM = 64
RW_WIDTH = RW_HEADS * HEAD_DIM
DECAY_LORA = 64
AAA_LORA = 64
GATE_LORA = 128
N_DIR = 2
GN_EPS = 64e-5
ATT_SLOTS = 4
DILATIONS = (1, 4, 16)
BAND_HALF = 64
N_DIL = len(DILATIONS)
GROUP_WIDTH = ATT_SLOTS * HEAD_DIM
ATT_WIDTH = N_DIL * GROUP_WIDTH
ATT_COLS = 3 * ATT_WIDTH
SHIFT_COLS = 3 * RW_WIDTH + N_DIR * DECAY_LORA + N_DIR * AAA_LORA + GATE_LORA
IN_COLS = ATT_COLS + SHIFT_COLS
ROPE_THETA = 500000.0
ROPE_DIMS = HEAD_DIM // 4
ROPE_HALF = ROPE_DIMS // 2
N_EXPERTS = 16
EC_CAPACITY_FACTOR = 2
EXPERT_FF = 2048
NORM_EPS = 1e-6
NEG_INF = -1e30
LANES = 128

ROW_TILE = 512
ATT_Q_BLOCK = 128
CHUNK = 64
CHUNKS_PER_ITER = 4
FFN_ROW_TILE = 512
VMEM_LIMIT = 56 * 1024 * 1024

HIGHEST = lax.Precision.HIGHEST


def _cparams(sem):
    return pltpu.CompilerParams(dimension_semantics=sem, vmem_limit_bytes=VMEM_LIMIT)


def _head_sum_matrix():
    h = np.arange(GROUP_WIDTH) // HEAD_DIM
    return jnp.asarray((h[:, None] == h[None, :]).astype(np.float32), dtype=jnp.bfloat16)


def _head_sum(x, bd):
    hi = x.astype(jnp.bfloat16)
    lo = (x - hi.astype(jnp.float32)).astype(jnp.bfloat16)
    return (jnp.dot(hi, bd, preferred_element_type=jnp.float32)
            + jnp.dot(lo, bd, preferred_element_type=jnp.float32))


def _rope_tables(t):
    inv = jnp.power(jnp.float32(ROPE_THETA), -jnp.arange(ROPE_HALF, dtype=jnp.float32) * 2.0 / ROPE_DIMS)
    ang = jnp.arange(t, dtype=jnp.float32)[:, None] * inv[None, :]
    cos, sin = jnp.cos(ang), jnp.sin(ang)
    zeros = jnp.zeros((t, HEAD_DIM - ROPE_DIMS), jnp.float32)
    zero8 = jnp.zeros((t, ROPE_HALF), jnp.float32)
    c = jnp.concatenate([cos, cos, zeros + 1.0], -1)
    s1 = jnp.concatenate([zero8, sin, zeros], -1)
    s2 = jnp.concatenate([-sin, zero8, zeros], -1)
    tile = lambda z: jnp.tile(z, (1, ATT_SLOTS))
    return tile(c), tile(s1), tile(s2)


def _in_proj_kernel(x_ref, g_ref, w_ref, qg_ref, kg_ref, c_ref, s1_ref, s2_ref, bd_ref,
                    q0_ref, q1_ref, q2_ref, k0_ref, k1_ref, k2_ref, v0_ref, v1_ref, v2_ref, s_ref,
                    tmp_ref):
    x = x_ref[...]
    h = x * lax.rsqrt(jnp.mean(x * x, -1, keepdims=True) + NORM_EPS) * g_ref[...]
    hb = h.astype(jnp.bfloat16)
    bd = bd_ref[...]
    c, s1, s2 = c_ref[...], s1_ref[...], s2_ref[...]
    rows = x.shape[0]

    def deinterleave(val, out_ref, dil):
        if dil == 1:
            out_ref[0] = val.astype(out_ref.dtype)
            return
        for hf in range(GROUP_WIDTH // LANES):
            tmp_ref[hf] = val[:, hf * LANES:(hf + 1) * LANES]
        for r in range(dil):
            for hf in range(GROUP_WIDTH // LANES):
                out_ref[r, :, hf * LANES:(hf + 1) * LANES] = (
                    tmp_ref[hf, pl.ds(r, rows // dil, stride=dil), :].astype(out_ref.dtype))

    proj = jnp.dot(hb, w_ref[:, :ATT_COLS], preferred_element_type=jnp.float32)

    def normed_rope(col0, gain, scale):
        p = proj[:, col0:col0 + GROUP_WIDTH]
        ms = _head_sum(p * p, bd) * (1.0 / HEAD_DIM)
        n = p * lax.rsqrt(ms + NORM_EPS) * gain
        n = n * c + pltpu.roll(n, ROPE_HALF, 1) * s1 + pltpu.roll(n, GROUP_WIDTH - ROPE_HALF, 1) * s2
        return n * scale if scale != 1.0 else n

    q_refs = (q0_ref, q1_ref, q2_ref)
    k_refs = (k0_ref, k1_ref, k2_ref)
    v_refs = (v0_ref, v1_ref, v2_ref)
    for g, dil in enumerate(DILATIONS):
        deinterleave(normed_rope(g * GROUP_WIDTH, qg_ref[...], HEAD_DIM ** -0.5), q_refs[g], dil)
        deinterleave(normed_rope(ATT_WIDTH + g * GROUP_WIDTH, kg_ref[...], 1.0), k_refs[g], dil)
        col0 = 2 * ATT_WIDTH + g * GROUP_WIDTH
        deinterleave(proj[:, col0:col0 + GROUP_WIDTH], v_refs[g], dil)
    s_ref[...] = jnp.dot(hb, w_ref[:, ATT_COLS:], preferred_element_type=jnp.float32)


def _in_proj(x, ln1_g, w_in_bf16, q_gain, k_gain, rope, bd):
    b, t, _ = x.shape
    nt = t // ROW_TILE
    row2 = lambda bi, i: (0, 0)
    grp_shapes, grp_specs = [], []
    for _ in range(3):
        for dil in DILATIONS:
            grp_shapes.append(jax.ShapeDtypeStruct((b, dil, t // dil, GROUP_WIDTH), jnp.bfloat16))
            grp_specs.append(pl.BlockSpec((None, dil, ROW_TILE // dil, GROUP_WIDTH), lambda bi, i: (bi, 0, i, 0)))
    tab_spec = pl.BlockSpec((ROW_TILE, GROUP_WIDTH), lambda bi, i: (i, 0))
    outs = pl.pallas_call(
        _in_proj_kernel,
        grid=(b, nt),
        in_specs=[
            pl.BlockSpec((None, ROW_TILE, D_MODEL), lambda bi, i: (bi, i, 0)),
            pl.BlockSpec((1, D_MODEL), row2),
            pl.BlockSpec((D_MODEL, IN_COLS), row2),
            pl.BlockSpec((1, GROUP_WIDTH), row2),
            pl.BlockSpec((1, GROUP_WIDTH), row2),
            tab_spec, tab_spec, tab_spec,
            pl.BlockSpec((GROUP_WIDTH, GROUP_WIDTH), row2),
        ],
        out_specs=grp_specs + [pl.BlockSpec((None, ROW_TILE, SHIFT_COLS), lambda bi, i: (bi, i, 0))],
        out_shape=grp_shapes + [jax.ShapeDtypeStruct((b, t, SHIFT_COLS), jnp.float32)],
        scratch_shapes=[pltpu.VMEM((GROUP_WIDTH // LANES, ROW_TILE, LANES), jnp.float32)],
        compiler_params=_cparams(("parallel", "parallel")),
        name="in_proj",
    )(x, ln1_g.reshape(1, D_MODEL), w_in_bf16, q_gain, k_gain, *rope, bd)
    q = outs[0:3]
    k = outs[3:6]
    v = outs[6:9]
    return q, k, v, outs[9]


def _band_attn_kernel(q_ref, kp_ref, kc_ref, kn_ref, vp_ref, vc_ref, vn_ref, o_ref, lse_ref, *, seq_len, q_chunk):
    i = pl.program_id(1)
    kext = jnp.concatenate([kp_ref[...], kc_ref[...], kn_ref[...]], axis=0)
    vext = jnp.concatenate([vp_ref[...], vc_ref[...], vn_ref[...]], axis=0)
    kw = ATT_Q_BLOCK + 2 * BAND_HALF
    lane_head = lax.broadcasted_iota(jnp.int32, (ATT_Q_BLOCK, GROUP_WIDTH), 1) // HEAD_DIM
    qi = lax.broadcasted_iota(jnp.int32, (ATT_Q_BLOCK, kw), 0)
    kc = lax.broadcasted_iota(jnp.int32, (ATT_Q_BLOCK, kw), 1)
    diff = kc - BAND_HALF - qi
    band = (diff <= BAND_HALF) & (diff >= -BAND_HALF)
    for a in range(0, q_chunk, ATT_Q_BLOCK):
        kpos = i * q_chunk + (a - BAND_HALF) + kc
        valid = band & (kpos >= 0) & (kpos < seq_len)
        qb = q_ref[a:a + ATT_Q_BLOCK, :]
        kb = kext[a:a + kw]
        vb = vext[a:a + kw]
        heads = range(ATT_SLOTS)
        mine = [lane_head == h for h in heads]
        s = [lax.dot_general(jnp.where(mine[h], qb, jnp.zeros_like(qb)), kb, (((1,), (1,)), ((), ())),
                             preferred_element_type=jnp.float32) for h in heads]
        s = [jnp.where(valid, s_h, NEG_INF) for s_h in s]
        m = [jnp.max(s_h, -1, keepdims=True) for s_h in s]
        p = [jnp.exp(s_h - m_h) for s_h, m_h in zip(s, m)]
        den = [jnp.sum(p_h, -1, keepdims=True) for p_h in p]
        pv = [jnp.dot(p_h.astype(jnp.bfloat16), vb, preferred_element_type=jnp.float32) for p_h in p]
        acc = jnp.zeros((ATT_Q_BLOCK, GROUP_WIDTH), jnp.float32)
        lse = jnp.zeros((ATT_Q_BLOCK, GROUP_WIDTH), jnp.float32)
        for h in heads:
            acc = jnp.where(mine[h], pv[h] / den[h], acc)
            lse = jnp.where(mine[h], m[h] + jnp.log(den[h]), lse)
        o_ref[a:a + ATT_Q_BLOCK, :] = acc
        lse_ref[a:a + ATT_Q_BLOCK, :] = lse


def _band_attn(q, k, v):
    b, dil, seq_len, _ = q.shape
    ns = b * dil
    q, k, v = (z.reshape(ns, seq_len, GROUP_WIDTH) for z in (q, k, v))
    q_chunk = min(seq_len, 512)
    halo_per_chunk = q_chunk // BAND_HALF
    last_halo = seq_len // BAND_HALF - 1
    cur = pl.BlockSpec((None, q_chunk, GROUP_WIDTH), lambda s, i: (s, i, 0))
    prev = pl.BlockSpec((None, BAND_HALF, GROUP_WIDTH), lambda s, i: (s, jnp.maximum(i * halo_per_chunk - 1, 0), 0))
    nxt = pl.BlockSpec((None, BAND_HALF, GROUP_WIDTH),
                       lambda s, i: (s, jnp.minimum((i + 1) * halo_per_chunk, last_halo), 0))
    o, lse = pl.pallas_call(
        functools.partial(_band_attn_kernel, seq_len=seq_len, q_chunk=q_chunk),
        grid=(ns, seq_len // q_chunk),
        in_specs=[cur, prev, cur, nxt, prev, cur, nxt],
        out_specs=[cur, cur],
        out_shape=[jax.ShapeDtypeStruct((ns, seq_len, GROUP_WIDTH), jnp.float32)] * 2,
        compiler_params=_cparams(("parallel", "parallel")),
        name="band_attn",
    )(q, k, k, k, v, v, v)
    shape4 = (b, dil, seq_len, GROUP_WIDTH)
    return o.reshape(shape4), lse.reshape(shape4)


def _sigmoid(x):
    return 1.0 / (1.0 + jnp.exp(-x))


def _rwkv_prep_kernel(sp_ref, sc_ref, sn_ref, mup_ref, mun_ref, w0_ref, w2_ref, a0_ref, a2_ref, g2_ref,
                      kk_ref, ka_ref, rk_ref, bd_ref,
                      lhs_ref, add_ref, bonus_ref, gate_ref,
                      r_s, v_s, nkk_s, e_s, kd_s, beta_s):
    i = pl.program_id(1)
    last = pl.num_programs(1) - 1
    p = sc_ref[...]
    rows = p.shape[0]
    row = lax.broadcasted_iota(jnp.int32, p.shape, 0)
    halo_prev = sp_ref[7:8, :] * (i > 0).astype(jnp.float32)
    halo_next = sn_ref[0:1, :] * (i < last).astype(jnp.float32)
    p_prev = jnp.where(row == 0, halo_prev, pltpu.roll(p, 1, 0))
    p_next = jnp.where(row == rows - 1, halo_next, pltpu.roll(p, rows - 1, 0))
    u = p + mup_ref[...] * (p_prev - p) + mun_ref[...] * (p_next - p)

    bd = bd_ref[...]
    r = u[:, 0:RW_WIDTH]
    k = u[:, RW_WIDTH:2 * RW_WIDTH]
    v = u[:, 2 * RW_WIDTH:3 * RW_WIDTH]
    c0 = 3 * RW_WIDTH
    lw = jnp.tanh(u[:, c0:c0 + 128]).astype(jnp.bfloat16)
    la = u[:, c0 + 128:c0 + 256].astype(jnp.bfloat16)
    lg = _sigmoid(u[:, c0 + 256:c0 + 384]).astype(jnp.bfloat16)

    kk = k * kk_ref[...]
    kk = kk * lax.rsqrt(_head_sum(kk * kk, bd) + 1e-12)
    r_s[...] = r
    v_s[...] = v
    nkk_s[...] = -kk
    gate_ref[...] = jnp.dot(lg, g2_ref[...], preferred_element_type=jnp.float32)
    bonus = jnp.zeros_like(v)
    for d in range(N_DIR):
        z = w0_ref[d:d + 1, :] + jnp.dot(lw, w2_ref[d], preferred_element_type=jnp.float32)
        e_s[d] = _sigmoid(z) * float(np.exp(-0.5))
        a = _sigmoid(a0_ref[d:d + 1, :] + jnp.dot(la, a2_ref[d], preferred_element_type=jnp.float32))
        kd = k * (1.0 + (a - 1.0) * ka_ref[...])
        kd_s[d] = kd
        beta_s[d] = kk * a
        bonus = bonus + _head_sum(r * kd * rk_ref[...], bd) * v
    bonus_ref[...] = bonus

    idx_t = lax.broadcasted_iota(jnp.int32, (CHUNK, CHUNK), 0)
    idx_s = lax.broadcasted_iota(jnp.int32, (CHUNK, CHUNK), 1)
    eye = (idx_t == idx_s).astype(jnp.float32)
    bf = lambda z: z.astype(jnp.bfloat16)
    mm = lambda a_, b_: jnp.dot(bf(a_), bf(b_), preferred_element_type=jnp.float32)
    mm_nt = lambda a_, b_: lax.dot_general(bf(a_), bf(b_), (((1,), (1,)), ((), ())), preferred_element_type=jnp.float32)
    mm_tn = lambda a_, b_: lax.dot_general(bf(a_), bf(b_), (((0,), (0,)), ((), ())), preferred_element_type=jnp.float32)

    def chunk_body(cp, carry):
        units = []
        for c, d in [(cp * CHUNKS_PER_ITER + j, d_) for j in range(CHUNKS_PER_ITER) for d_ in range(N_DIR)]:
            rows_c = pl.ds(pl.multiple_of(c * CHUNK, CHUNK), CHUNK)
            r_c, v_c, nkk_c = r_s[rows_c, :], v_s[rows_c, :], nkk_s[rows_c, :]
            before = (idx_s < idx_t) if d == 0 else (idx_s > idx_t)
            before_eq = (idx_s <= idx_t) if d == 0 else (idx_s >= idx_t)
            last = CHUNK - 1 if d == 0 else 0
            e_c = e_s[d, rows_c, :]
            cl = jnp.dot(before_eq.astype(jnp.float32), e_c, precision=HIGHEST, preferred_element_type=jnp.float32)
            w_inc = jnp.exp(-cl)
            w_exc = jnp.exp(e_c - cl)
            w_inv = jnp.exp(cl)
            w_tot = w_inc[last:last + 1, :]
            a_t = nkk_c * w_exc
            r_t = r_c * w_inc
            b_h = beta_s[d, rows_c, :] * w_inv
            k_h = kd_s[d, rows_c, :] * w_inv
            b_t = b_h * w_tot
            k_t = k_h * w_tot
            for h in range(RW_HEADS):
                hs = slice(h * HEAD_DIM, (h + 1) * HEAD_DIM)
                units.append(dict(chunk=c, dir=d, before=before, before_eq=before_eq, a_t=a_t[:, hs], r_t=r_t[:, hs], b_h=b_h[:, hs],
                                  k_h=k_h[:, hs], b_t=b_t[:, hs], k_t=k_t[:, hs], w_tot=w_tot[:, hs], v=v_c[:, hs]))
        each = lambda fn: [fn(un) for un in units]
        g = each(lambda un: mm_nt(jnp.concatenate([un['a_t'], un['r_t']], axis=0),
                                  jnp.concatenate([un['b_h'], un['k_h']], axis=0)))
        for un, g_u in zip(units, g):
            un['m_ab'] = jnp.where(un['before'], g_u[:CHUNK, :CHUNK], 0.0)
            un['m_ak'] = jnp.where(un['before'], g_u[:CHUNK, CHUNK:], 0.0)
            un['n_rb'] = jnp.where(un['before_eq'], g_u[CHUNK:, :CHUNK], 0.0)
            un['n_rk'] = jnp.where(un['before_eq'], g_u[CHUNK:, CHUNK:], 0.0)
        t_inv = each(lambda un: eye + un['m_ab'])
        pw = each(lambda un: un['m_ab'])
        for _ in range(CHUNK.bit_length() - 2):
            pw = [mm(p_, p_) for p_ in pw]
            t_inv = [t_ + mm(t_, p_) for t_, p_ in zip(t_inv, pw)]
        mv = each(lambda un: mm(un['m_ak'], un['v']))
        tx = [mm(t_, jnp.concatenate([un['a_t'], mv_], axis=1)) for t_, un, mv_ in zip(t_inv, units, mv)]
        zero = jnp.zeros((CHUNK, HEAD_DIM), jnp.float32)
        z = [jnp.concatenate([tx_, jnp.concatenate([zero, un['v']], axis=1)], axis=0)
             for un, tx_ in zip(units, tx)]
        nx = [mm(jnp.concatenate([un['n_rb'], un['n_rk']], axis=1), z_) for un, z_ in zip(units, z)]
        bx = [mm_tn(jnp.concatenate([un['b_t'], un['k_t']], axis=0), z_) for un, z_ in zip(units, z)]
        lhs_parts, add_parts = [], []
        for un, nx_, bx_ in zip(units, nx, bx):
            ra = un['r_t'] + nx_[:, :HEAD_DIM]
            y0 = nx_[:, HEAD_DIM:]
            phi = eye * un['w_tot'] + bx_[:, :HEAD_DIM]
            psi = bx_[:, HEAD_DIM:]
            lhs_parts.append(jnp.concatenate([ra, phi], axis=0))
            add_parts.append(jnp.concatenate([y0, psi], axis=0))
        for g0 in range(0, len(units), RW_HEADS):
            c, d = units[g0]['chunk'], units[g0]['dir']
            lhs_ref[d, c] = jnp.concatenate(lhs_parts[g0:g0 + RW_HEADS], axis=1).astype(lhs_ref.dtype)
            add_ref[d, c] = jnp.concatenate(add_parts[g0:g0 + RW_HEADS], axis=1)
        return carry

    lax.fori_loop(0, rows // (CHUNK * CHUNKS_PER_ITER), chunk_body, 0)


def _rwkv_prep(s, mu_prev, mu_next, w0, w2p, a0, a2p, g2, k_k, k_a, r_k, bd):
    b, t, _ = s.shape
    nt = t // ROW_TILE
    halo_per_tile = ROW_TILE // 8
    last_halo = t // 8 - 1
    row2 = lambda bi, i: (0, 0)
    row3 = lambda bi, i: (0, 0, 0)
    vec = lambda n: pl.BlockSpec((1, n), row2)
    chunks_per_tile = ROW_TILE // CHUNK
    cspec = pl.BlockSpec((N_DIR, None, chunks_per_tile, 2 * CHUNK, RW_WIDTH), lambda bi, i: (0, bi, i, 0, 0))
    nspec = pl.BlockSpec((None, ROW_TILE, RW_WIDTH), lambda bi, i: (bi, i, 0))
    cshape = jax.ShapeDtypeStruct((N_DIR, b, t // CHUNK, 2 * CHUNK, RW_WIDTH), jnp.float32)
    nshape = jax.ShapeDtypeStruct((b, t, RW_WIDTH), jnp.float32)
    tile_f32 = pltpu.VMEM((ROW_TILE, RW_WIDTH), jnp.float32)
    tile2_f32 = pltpu.VMEM((N_DIR, ROW_TILE, RW_WIDTH), jnp.float32)
    return pl.pallas_call(
        _rwkv_prep_kernel,
        grid=(b, nt),
        in_specs=[
            pl.BlockSpec((None, 8, SHIFT_COLS), lambda bi, i: (bi, jnp.maximum(i * halo_per_tile - 1, 0), 0)),
            pl.BlockSpec((None, ROW_TILE, SHIFT_COLS), lambda bi, i: (bi, i, 0)),
            pl.BlockSpec((None, 8, SHIFT_COLS), lambda bi, i: (bi, jnp.minimum((i + 1) * halo_per_tile, last_halo), 0)),
            vec(SHIFT_COLS), vec(SHIFT_COLS),
            pl.BlockSpec((N_DIR, RW_WIDTH), row2),
            pl.BlockSpec((N_DIR, 128, RW_WIDTH), row3),
            pl.BlockSpec((N_DIR, RW_WIDTH), row2),
            pl.BlockSpec((N_DIR, 128, RW_WIDTH), row3),
            pl.BlockSpec((GATE_LORA, RW_WIDTH), row2),
            vec(RW_WIDTH), vec(RW_WIDTH), vec(RW_WIDTH),
            pl.BlockSpec((GROUP_WIDTH, GROUP_WIDTH), row2),
        ],
        out_specs=[cspec, cspec, nspec, nspec],
        out_shape=[jax.ShapeDtypeStruct(cshape.shape, jnp.bfloat16), cshape, nshape, nshape],
        scratch_shapes=[tile_f32, tile_f32, tile_f32, tile2_f32, tile2_f32, tile2_f32],
        compiler_params=_cparams(("parallel", "parallel")),
        name="rwkv_prep",
    )(s, s, s, mu_prev, mu_next, w0, w2p, a0, a2p, g2, k_k, k_a, r_k, bd)


def _split_bf16(x):
    hi = x.astype(jnp.bfloat16)
    return hi, (x - hi.astype(jnp.float32)).astype(jnp.bfloat16)


def _rwkv_scan_kernel(lhs_f, add_f, lhs_b, add_b, y_f, y_b, state_ref):
    n_chunks = lhs_f.shape[0]

    @pl.when(pl.program_id(1) == 0)
    def _():
        state_ref[...] = jnp.zeros_like(state_ref)

    dot = lambda a_, b_: jnp.dot(a_, b_, preferred_element_type=jnp.float32)

    def chunk_body(cc, carry):
        chains = []
        for d, (lhs_ref, add_ref, c) in enumerate(((lhs_f, add_f, cc), (lhs_b, add_b, n_chunks - 1 - cc))):
            for h in range(RW_HEADS):
                hs = slice(h * HEAD_DIM, (h + 1) * HEAD_DIM)
                chains.append((d, h, c, hs, lhs_ref, add_ref))
        lhs = [lhs_ref[c, :, hs] for (_, _, c, hs, lhs_ref, _) in chains]
        ssplit = [_split_bf16(state_ref[d, h]) for (d, h, *_) in chains]
        res = [dot(l_, s_hi) + dot(l_, s_lo) for l_, (s_hi, s_lo) in zip(lhs, ssplit)]
        res = [r_ + add_ref[c, :, hs] for r_, (_, _, c, hs, _, add_ref) in zip(res, chains)]
        for r_, (d, h, *_) in zip(res, chains):
            state_ref[d, h] = r_[CHUNK:]
        for d, (y_ref, c) in enumerate(((y_f, cc), (y_b, n_chunks - 1 - cc))):
            rows = pl.ds(pl.multiple_of(c * CHUNK, CHUNK), CHUNK)
            y_ref[rows, :] = jnp.concatenate([r_[:CHUNK] for r_ in res[d * RW_HEADS:(d + 1) * RW_HEADS]], axis=1)
        return carry

    lax.fori_loop(0, n_chunks, chunk_body, 0)


def _rwkv_scan(lhs, add):
    _, b, nchunk, _, _ = lhs.shape
    per_step = ROW_TILE // CHUNK
    nc = nchunk // per_step
    fwd = pl.BlockSpec((None, None, per_step, 2 * CHUNK, RW_WIDTH), lambda bi, c: (0, bi, c, 0, 0))
    bwd = pl.BlockSpec((None, None, per_step, 2 * CHUNK, RW_WIDTH), lambda bi, c: (1, bi, nc - 1 - c, 0, 0))
    y_shape = jax.ShapeDtypeStruct((b, nchunk * CHUNK, RW_WIDTH), jnp.float32)
    return pl.pallas_call(
        _rwkv_scan_kernel,
        grid=(b, nc),
        in_specs=[fwd, fwd, bwd, bwd],
        out_specs=[pl.BlockSpec((None, ROW_TILE, RW_WIDTH), lambda bi, c: (bi, c, 0)),
                   pl.BlockSpec((None, ROW_TILE, RW_WIDTH), lambda bi, c: (bi, nc - 1 - c, 0))],
        out_shape=[y_shape, y_shape],
        scratch_shapes=[pltpu.VMEM((N_DIR, RW_HEADS, HEAD_DIM, HEAD_DIM), jnp.float32)],
        compiler_params=_cparams(("parallel", "arbitrary")),
        name="rwkv_scan",
    )(lhs, add, lhs, add)


def _out_proj_kernel(x_ref, o0_ref, l0_ref, o1_ref, l1_ref, o2_ref, l2_ref, yf_ref, yb_ref, bonus_ref, gate_ref,
                     lnxg_ref, lnxb_ref, wout_ref, ln2_ref, router_ref, bd_ref,
                     xn_ref, h2_ref, aff_ref, so1_ref, sl1_ref, so2_ref, sl2_ref):
    rows = x_ref.shape[0]

    def interleave(src_ref, dst_ref, dil):
        halves = range(GROUP_WIDTH // LANES)
        for r in range(dil):
            for hf in halves:
                dst_ref[hf, pl.ds(r, rows // dil, stride=dil), :] = src_ref[r, :, hf * LANES:(hf + 1) * LANES]
        return jnp.concatenate([dst_ref[hf] for hf in halves], axis=-1)

    o1 = interleave(o1_ref, so1_ref, DILATIONS[1])
    l1 = interleave(l1_ref, sl1_ref, DILATIONS[1])
    o2 = interleave(o2_ref, so2_ref, DILATIONS[2])
    l2 = interleave(l2_ref, sl2_ref, DILATIONS[2])
    l0 = l0_ref[0]
    m = jnp.maximum(jnp.maximum(l0, l1), l2)
    e0, e1, e2 = jnp.exp(l0 - m), jnp.exp(l1 - m), jnp.exp(l2 - m)
    att = (e0 * o0_ref[0] + e1 * o1 + e2 * o2) / (e0 + e1 + e2)

    bd = bd_ref[...]
    y = yf_ref[...] + yb_ref[...]
    mu = _head_sum(y, bd) * (1.0 / HEAD_DIM)
    dlt = y - mu
    var = _head_sum(dlt * dlt, bd) * (1.0 / HEAD_DIM)
    yn = dlt * lax.rsqrt(var + GN_EPS) * lnxg_ref[...] + lnxb_ref[...]
    rw = (yn + bonus_ref[...]) * gate_ref[...]

    mixed = (jnp.dot(att.astype(jnp.bfloat16), wout_ref[0:GROUP_WIDTH, :], preferred_element_type=jnp.float32)
             + jnp.dot(rw.astype(jnp.bfloat16), wout_ref[GROUP_WIDTH:, :], preferred_element_type=jnp.float32))
    xn = x_ref[...] + mixed
    xn_ref[...] = xn
    h2 = xn * lax.rsqrt(jnp.mean(xn * xn, -1, keepdims=True) + NORM_EPS) * ln2_ref[...]
    h2_ref[...] = h2.astype(h2_ref.dtype)
    logits = lax.dot_general(router_ref[...], h2, (((1,), (1,)), ((), ())),
                             precision=HIGHEST, preferred_element_type=jnp.float32)
    ex = jnp.exp(logits - jnp.max(logits, 0, keepdims=True))
    aff_ref[...] = ex / jnp.sum(ex, 0, keepdims=True)


def _out_proj(x, att_parts, y, bonus, gate, lnx_g, lnx_b, w_out_bf16, ln2_g, router_t, bd):
    b, t, _ = x.shape
    nt = t // ROW_TILE
    row2 = lambda bi, i: (0, 0)
    tok = lambda w: pl.BlockSpec((None, ROW_TILE, w), lambda bi, i: (bi, i, 0))
    grp = lambda dil: pl.BlockSpec((None, dil, ROW_TILE // dil, GROUP_WIDTH), lambda bi, i: (bi, 0, i, 0))
    vec = lambda n: pl.BlockSpec((1, n), row2)
    grp_specs, grp_args = [], []
    for (o, lse), dil in zip(att_parts, DILATIONS):
        grp_specs += [grp(dil), grp(dil)]
        grp_args += [o, lse]
    return pl.pallas_call(
        _out_proj_kernel,
        grid=(b, nt),
        in_specs=[tok(D_MODEL)] + grp_specs + [
            tok(RW_WIDTH), tok(RW_WIDTH), tok(RW_WIDTH), tok(RW_WIDTH), vec(RW_WIDTH), vec(RW_WIDTH),
            pl.BlockSpec((GROUP_WIDTH + RW_WIDTH, D_MODEL), row2),
            vec(D_MODEL),
            pl.BlockSpec((N_EXPERTS, D_MODEL), row2),
            pl.BlockSpec((GROUP_WIDTH, GROUP_WIDTH), row2),
        ],
        out_specs=[tok(D_MODEL), tok(D_MODEL),
                   pl.BlockSpec((N_EXPERTS, ROW_TILE), lambda bi, i: (0, bi * nt + i))],
        out_shape=[jax.ShapeDtypeStruct((b, t, D_MODEL), jnp.float32),
                   jax.ShapeDtypeStruct((b, t, D_MODEL), jnp.bfloat16),
                   jax.ShapeDtypeStruct((N_EXPERTS, b * t), jnp.float32)],
        scratch_shapes=[pltpu.VMEM((GROUP_WIDTH // LANES, ROW_TILE, LANES), jnp.float32)] * 4,
        compiler_params=_cparams(("parallel", "parallel")),
        name="out_proj",
    )(x, *grp_args, *y, bonus, gate, lnx_g, lnx_b, w_out_bf16, ln2_g, router_t, bd)


FF_CHUNK = 512


def _expert_ffn_kernel(x_ref, gate_ref, wg_ref, wu_ref, wd_ref, o_ref):
    x = x_ref[...]
    acc = jnp.zeros(o_ref.shape, jnp.float32)
    for f0 in range(0, EXPERT_FF, FF_CHUNK):
        hg = jnp.dot(x, wg_ref[:, f0:f0 + FF_CHUNK], preferred_element_type=jnp.float32)
        hu = jnp.dot(x, wu_ref[:, f0:f0 + FF_CHUNK], preferred_element_type=jnp.float32)
        hid = (hg * _sigmoid(hg) * hu).astype(jnp.bfloat16)
        acc = acc + jnp.dot(hid, wd_ref[f0:f0 + FF_CHUNK, :], preferred_element_type=jnp.float32)
    o_ref[...] = (acc * gate_ref[...]).astype(o_ref.dtype)


def _expert_ffn(xs, gate, wg, wu, wd):
    e, cap, _ = xs.shape
    tile = min(FFN_ROW_TILE, cap)
    tok = lambda w: pl.BlockSpec((None, tile, w), lambda ei, i: (ei, i, 0))
    wspec = lambda r, c: pl.BlockSpec((None, r, c), lambda ei, i: (ei, 0, 0))
    return pl.pallas_call(
        _expert_ffn_kernel,
        grid=(e, cap // tile),
        in_specs=[tok(D_MODEL), tok(1), wspec(D_MODEL, EXPERT_FF), wspec(D_MODEL, EXPERT_FF), wspec(EXPERT_FF, D_MODEL)],
        out_specs=tok(D_MODEL),
        out_shape=jax.ShapeDtypeStruct((e, cap, D_MODEL), jnp.bfloat16),
        compiler_params=_cparams(("parallel", "arbitrary")),
        name="expert_ffn",
    )(xs, gate, wg, wu, wd)


ONE_BITS_PLUS_1 = 0x3F800001
LIST_ROWS = 16


def _prefix_counts(mask_bf16, tri_ones, strict_lower):
    t = jnp.dot(mask_bf16, tri_ones, preferred_element_type=jnp.float32)
    offs = jnp.dot(strict_lower, t[:, LANES:].astype(jnp.bfloat16), preferred_element_type=jnp.float32)
    return t[:, :LANES] + offs, offs, t[:, LANES:]


def _route_kernel(aff_ref, pos_ref, offs_ref, idx_ref, gate_ref, *, cap):
    x = pltpu.bitcast(aff_ref[...], jnp.int32)
    nb = x.shape[0]

    def count_ge(thr):
        hit = (x >= thr).astype(jnp.int32)
        return jnp.sum(jnp.sum(hit, axis=0, keepdims=True), axis=1, keepdims=True)

    def bisect(_, bounds):
        lo, hi = bounds
        mid = lo + ((hi - lo) >> 1)
        ok = count_ge(mid) >= cap
        return jnp.where(ok, mid, lo), jnp.where(ok, hi, mid)

    lo0 = jnp.zeros((1, 1), jnp.int32)
    hi0 = jnp.full((1, 1), ONE_BITS_PLUS_1, jnp.int32)
    thr, _ = lax.fori_loop(0, 31, bisect, (lo0, hi0))
    above = x > thr
    tie = x == thr
    need = cap - count_ge(thr + 1)

    r_i = lax.broadcasted_iota(jnp.int32, (LANES, 2 * LANES), 0)
    c_i = lax.broadcasted_iota(jnp.int32, (LANES, 2 * LANES), 1)
    tri_ones = ((c_i >= LANES) | (r_i <= c_i)).astype(jnp.bfloat16)
    rr = lax.broadcasted_iota(jnp.int32, (nb, nb), 0)
    cc = lax.broadcasted_iota(jnp.int32, (nb, nb), 1)
    strict_lower = (cc < rr).astype(jnp.bfloat16)

    tie_f = tie.astype(jnp.float32)
    tie_incl, _, _ = _prefix_counts(tie_f.astype(jnp.bfloat16), tri_ones, strict_lower)
    sel = above | (tie & ((tie_incl - tie_f) < need.astype(jnp.float32)))
    sel_incl, offs, row_total = _prefix_counts(sel.astype(jnp.bfloat16), tri_ones, strict_lower)
    in_row = sel_incl - offs
    row_end = offs + row_total
    pos_ref[...] = jnp.where(sel, sel_incl.astype(jnp.int32) - 1, -1)
    offs_ref[...] = offs.astype(jnp.int32)

    aff_t = aff_ref[...].T
    g_hi = aff_t.astype(jnp.bfloat16)
    g_mid = (aff_t - g_hi.astype(jnp.float32)).astype(jnp.bfloat16)
    g_lo = (aff_t - g_hi.astype(jnp.float32) - g_mid.astype(jnp.float32)).astype(jnp.bfloat16)
    row_id = lax.broadcasted_iota(jnp.int32, (1, nb), 1)
    offs_t = offs.T[0:1, :].astype(jnp.int32)
    facts = jnp.concatenate([row_id >> 4, row_id & 15, offs_t >> 7, offs_t & (LANES - 1)], axis=0)
    facts = jnp.concatenate([facts.astype(jnp.float32), jnp.zeros((LIST_ROWS - 4, nb), jnp.float32)], axis=0)
    stack = jnp.concatenate([in_row.T.astype(jnp.bfloat16), facts.astype(jnp.bfloat16), g_hi, g_mid, g_lo], axis=0)
    sub = lax.broadcasted_iota(jnp.int32, (LANES, LANES), 0).astype(jnp.float32)

    def window_pair(wp, carry):
        ws = [wp * 2, wp * 2 + 1]
        slots = [(w * LANES + lax.broadcasted_iota(jnp.int32, (1, LANES), 1)).astype(jnp.float32) for w in ws]
        onehots = [((offs <= s_) & (s_ < row_end)).astype(jnp.bfloat16) for s_ in slots]
        gots = [jnp.dot(stack, oh, preferred_element_type=jnp.float32) for oh in onehots]
        for w, slot, got in zip(ws, slots, gots):
            emit(w, slot, got)
        return carry

    def emit(w, slot, got):
        counts = got[:LANES]
        fact = got[LANES:LANES + LIST_ROWS]
        gates = got[LANES + LIST_ROWS:LANES + LIST_ROWS + LANES] + got[2 * LANES + LIST_ROWS:3 * LANES + LIST_ROWS] \
            + got[3 * LANES + LIST_ROWS:]
        row = fact[0:1] * 16.0 + fact[1:2]
        local = slot - (fact[2:3] * float(LANES) + fact[3:4])
        lane = jnp.sum((counts <= local).astype(jnp.float32), axis=0, keepdims=True)
        idx = (row * float(LANES) + lane).astype(jnp.int32)
        gate = jnp.sum(jnp.where(sub == lane, gates, 0.0), axis=0, keepdims=True)
        idx_ref[w] = jnp.broadcast_to(idx, (8, LANES))
        gate_ref[w] = jnp.broadcast_to(gate, (8, LANES))

    lax.fori_loop(0, cap // (2 * LANES), window_pair, 0)


def _route(aff3, cap):
    e, nb, _ = aff3.shape
    n_win = cap // LANES
    spec = pl.BlockSpec((None, nb, LANES), lambda ei: (ei, 0, 0))
    lspec = pl.BlockSpec((None, n_win, 8, LANES), lambda ei: (ei, 0, 0, 0))
    shape = jax.ShapeDtypeStruct((e, nb, LANES), jnp.int32)
    pos, offs, idx, gate = pl.pallas_call(
        functools.partial(_route_kernel, cap=cap),
        grid=(e,),
        in_specs=[spec],
        out_specs=[spec, spec, lspec, lspec],
        out_shape=[shape, shape, jax.ShapeDtypeStruct((e, n_win, 8, LANES), jnp.int32),
                   jax.ShapeDtypeStruct((e, n_win, 8, LANES), jnp.float32)],
        compiler_params=_cparams(("parallel",)),
        name="route",
    )(aff3)
    return pos, offs, idx[:, :, 0, :].reshape(e, cap), gate[:, :, 0, :].reshape(e, cap)


COMBINE_TILE = 512
WINDOW = 64
GROUP = 16
MAX_WINDOWS = N_EXPERTS * (COMBINE_TILE // WINDOW + 1)
NO_SLOT = -(1 << 20)


def _combine_kernel(first_ref, count_ref, x_ref, pos_ref, ye_hbm, o_ref, stage, sems, win_e, win_s, n_win):
    i = pl.program_id(0)
    half = i % 2

    def fetch(h, j, e, win):
        return pltpu.make_async_copy(ye_hbm.at[e, pl.ds(win * WINDOW, WINDOW), :], stage.at[h, j], sems.at[h, j])

    def request_tile(tile, h):
        n_req = jnp.int32(0)
        for e in range(N_EXPERTS):
            first = first_ref[e, tile]

            def request_one(k, j, e=e, first=first):
                win_e[h, j] = e
                win_s[h, j] = (first + k) * WINDOW
                fetch(h, j, e, first + k).start()
                return j + 1

            n_req = lax.fori_loop(0, count_ref[e, tile], request_one, n_req)
        n_win[h] = n_req

    @pl.when(i == 0)
    def _():
        stage[...] = jnp.zeros_like(stage)
        request_tile(0, 0)

    @pl.when(i + 1 < pl.num_programs(0))
    def _():
        request_tile(i + 1, 1 - half)

    n_here = n_win[half]
    slot_iota = lax.broadcasted_iota(jnp.int32, (WINDOW, COMBINE_TILE), 0)

    def group_body(g, acc):
        onehots = []
        for u in range(GROUP):
            j = g * GROUP + u
            live = j < n_here
            pos_row = pos_ref[jnp.where(live, win_e[half, j], 0)]
            base = jnp.where(live, win_s[half, j], NO_SLOT)
            onehots.append((pos_row == base + slot_iota).astype(jnp.bfloat16))
        onehot = jnp.concatenate(onehots, axis=0)
        for u in range(GROUP):
            j = g * GROUP + u

            @pl.when(j < n_here)
            def _(j=j):
                fetch(half, j, 0, 0).wait()

        rows = stage[half, pl.ds(g * GROUP, GROUP)].reshape(GROUP * WINDOW, D_MODEL)
        return acc + lax.dot_general(onehot, rows, (((0,), (0,)), ((), ())), preferred_element_type=jnp.float32)

    o_ref[...] = lax.fori_loop(0, (n_here + GROUP - 1) // GROUP, group_body, x_ref[...])


def _combine(xn, pos, offs, ye, cap):
    n = xn.shape[0]
    nt = n // COMBINE_TILE
    rows_per_tile = COMBINE_TILE // LANES
    start = offs[:, ::rows_per_tile, 0]
    stop = jnp.concatenate([start[:, 1:], jnp.full((N_EXPERTS, 1), cap, jnp.int32)], axis=1)
    first = start // WINDOW
    count = jnp.where(stop > start, (stop - 1) // WINDOW - first + 1, 0)
    pos_t = jnp.transpose(pos.reshape(N_EXPERTS, nt, COMBINE_TILE), (1, 0, 2)).reshape(nt, N_EXPERTS, 1, COMBINE_TILE)
    tok = pl.BlockSpec((COMBINE_TILE, D_MODEL), lambda i, f, c: (i, 0))
    return pl.pallas_call(
        _combine_kernel,
        grid_spec=pltpu.PrefetchScalarGridSpec(
            num_scalar_prefetch=2,
            grid=(nt,),
            in_specs=[tok,
                      pl.BlockSpec((None, N_EXPERTS, 1, COMBINE_TILE), lambda i, f, c: (i, 0, 0, 0)),
                      pl.BlockSpec(memory_space=pl.ANY)],
            out_specs=tok,
            scratch_shapes=[pltpu.VMEM((2, MAX_WINDOWS, WINDOW, D_MODEL), jnp.bfloat16),
                            pltpu.SemaphoreType.DMA((2, MAX_WINDOWS)),
                            pltpu.SMEM((2, MAX_WINDOWS), jnp.int32),
                            pltpu.SMEM((2, MAX_WINDOWS), jnp.int32),
                            pltpu.SMEM((2,), jnp.int32)],
        ),
        out_shape=jax.ShapeDtypeStruct((n, D_MODEL), jnp.float32),
        compiler_params=_cparams(("arbitrary",)),
        name="combine",
    )(first, count, xn, pos_t, ye)


def _pad_lora(w):
    z = jnp.zeros((N_DIR, N_DIR * w.shape[1], w.shape[2]), w.dtype)
    for d in range(N_DIR):
        z = z.at[d, d * w.shape[1]:(d + 1) * w.shape[1]].set(w[d])
    return z.astype(jnp.bfloat16)


def _layer_params(l, ln1_g, ln2_g, w_in, mu_prev, mu_next, q_norm_g, k_norm_g, w0, w2, a0, a2, g2, k_k, k_a,
                  r_k, lnx_g, lnx_b, w_out, router, w_gate, w_up, w_down):
    bf = lambda z: z.astype(jnp.bfloat16)
    row = lambda z: z.reshape(1, -1)
    return dict(
        ln1_g=ln1_g[l], ln2_g=row(ln2_g[l]), w_in=bf(w_in[l]), mu_prev=row(mu_prev[l]), mu_next=row(mu_next[l]),
        q_gain=row(jnp.tile(q_norm_g[l], ATT_SLOTS)), k_gain=row(jnp.tile(k_norm_g[l], ATT_SLOTS)),
        w0=w0[l], w2=_pad_lora(w2[l]), a0=a0[l], a2=_pad_lora(a2[l]), g2=bf(g2[l]),
        k_k=row(k_k[l]), k_a=row(k_a[l]), r_k=row(r_k[l]), lnx_g=row(lnx_g[l]), lnx_b=row(lnx_b[l]),
        w_out=bf(w_out[l]), router_t=router[l].T, w_gate=bf(w_gate[l]), w_up=bf(w_up[l]), w_down=bf(w_down[l]),
    )


def _encoder_layer(x, p, rope, bd):
    b, t, _ = x.shape
    n = b * t
    q, k, v, s = _in_proj(x, p['ln1_g'], p['w_in'], p['q_gain'], p['k_gain'], rope, bd)
    att_parts = [_band_attn(q[g], k[g], v[g]) for g in range(N_DIL)]
    lhs, add, bonus, gate = _rwkv_prep(
        s, p['mu_prev'], p['mu_next'], p['w0'], p['w2'], p['a0'], p['a2'], p['g2'], p['k_k'], p['k_a'], p['r_k'], bd)
    y = _rwkv_scan(lhs, add)
    xn, h2, aff_t = _out_proj(x, att_parts, y, bonus, gate, p['lnx_g'], p['lnx_b'], p['w_out'], p['ln2_g'],
                              p['router_t'], bd)
    cap = max(1, (EC_CAPACITY_FACTOR * n) // N_EXPERTS)
    pos, offs, idx, gate_e = _route(aff_t.reshape(N_EXPERTS, n // LANES, LANES), cap)
    xs = h2.reshape(n, D_MODEL)[idx]
    ye = _expert_ffn(xs, gate_e[..., None], p['w_gate'], p['w_up'], p['w_down'])
    return _combine(xn.reshape(n, D_MODEL), pos, offs, ye, cap).reshape(b, t, D_MODEL)


def kernel(x_prompt, x_sample, ln1_g, ln2_g, w_in, mu_prev, mu_next, q_norm_g, k_norm_g, w0, w2, a0, a2, g2,
           k_k, k_a, r_k, lnx_g, lnx_b, w_out, router, w_gate, w_up, w_down):
    depth = w_in.shape[0]
    bd = _head_sum_matrix()
    rope_p = _rope_tables(x_prompt.shape[1])
    rope_s = _rope_tables(x_sample.shape[1])
    y_prompt, y_sample = x_prompt, x_sample
    for l in range(depth):
        p = _layer_params(l, ln1_g, ln2_g, w_in, mu_prev, mu_next, q_norm_g, k_norm_g, w0, w2, a0, a2, g2, k_k,
                          k_a, r_k, lnx_g, lnx_b, w_out, router, w_gate, w_up, w_down)
        y_prompt = _encoder_layer(y_prompt, p, rope_p, bd)
        y_sample = _encoder_layer(y_sample, p, rope_s, bd)
    return (y_prompt, y_sample)
```

```python
import functools

import jax
import jax.numpy as jnp
import numpy as np
from jax import lax
from jax.experimental import pallas as pl
from jax.experimental.pallas import tpu as pltpu

D_MODEL = 1024
RW_HEADS = 4
HEAD_DIM = 64
RW_WIDTH = RW_HEADS * HEAD_DIM
DECAY_LORA = 64
AAA_LORA = 64
GATE_LORA = 128
N_DIR = 2
GN_EPS = 64e-5
ATT_SLOTS = 4
DILATIONS = (1, 4, 16)
BAND_HALF = 64
N_DIL = len(DILATIONS)
GROUP_WIDTH = ATT_SLOTS * HEAD_DIM
ATT_WIDTH = N_DIL * GROUP_WIDTH
ATT_COLS = 3 * ATT_WIDTH
SHIFT_COLS = 3 * RW_WIDTH + N_DIR * DECAY_LORA + N_DIR * AAA_LORA + GATE_LORA
IN_COLS = ATT_COLS + SHIFT_COLS
ROPE_THETA = 500000.0
ROPE_DIMS = HEAD_DIM // 4
ROPE_HALF = ROPE_DIMS // 2
N_EXPERTS = 16
EC_CAPACITY_FACTOR = 2
EXPERT_FF = 2048
NORM_EPS = 1e-6
NEG_INF = -1e30
LANES = 128

ROW_TILE = 512
ATT_Q_BLOCK = 128
CHUNK = 64
CHUNKS_PER_ITER = 4
INVERSE_SCALE = 16.0
RISK_LIMIT = 4.0
SCAN_CHUNK = 128
SCAN_UNROLL = 8
FFN_ROW_TILE = 512
VMEM_LIMIT = 56 * 1024 * 1024

HIGHEST = lax.Precision.HIGHEST


def _cparams(sem):
    return pltpu.CompilerParams(dimension_semantics=sem, vmem_limit_bytes=VMEM_LIMIT)


def _head_sum_matrix():
    h = np.arange(GROUP_WIDTH) // HEAD_DIM
    return jnp.asarray((h[:, None] == h[None, :]).astype(np.float32), dtype=jnp.bfloat16)


def _head_sum(x, bd):
    hi = x.astype(jnp.bfloat16)
    lo = (x - hi.astype(jnp.float32)).astype(jnp.bfloat16)
    return (jnp.dot(hi, bd, preferred_element_type=jnp.float32)
            + jnp.dot(lo, bd, preferred_element_type=jnp.float32))


def _rope_tables(t):
    inv = jnp.power(jnp.float32(ROPE_THETA), -jnp.arange(ROPE_HALF, dtype=jnp.float32) * 2.0 / ROPE_DIMS)
    ang = jnp.arange(t, dtype=jnp.float32)[:, None] * inv[None, :]
    cos, sin = jnp.cos(ang), jnp.sin(ang)
    zeros = jnp.zeros((t, HEAD_DIM - ROPE_DIMS), jnp.float32)
    zero8 = jnp.zeros((t, ROPE_HALF), jnp.float32)
    c = jnp.concatenate([cos, cos, zeros + 1.0], -1)
    s1 = jnp.concatenate([zero8, sin, zeros], -1)
    s2 = jnp.concatenate([-sin, zero8, zeros], -1)
    tile = lambda z: jnp.tile(z, (1, ATT_SLOTS))
    return tile(c), tile(s1), tile(s2)


def _in_proj_kernel(x_ref, g_ref, w_ref, qg_ref, kg_ref, c_ref, s1_ref, s2_ref, bd_ref,
                    q0_ref, q1_ref, q2_ref, k0_ref, k1_ref, k2_ref, v0_ref, v1_ref, v2_ref, s_ref,
                    tmp_ref):
    x = x_ref[...]
    h = x * lax.rsqrt(jnp.mean(x * x, -1, keepdims=True) + NORM_EPS) * g_ref[...]
    hb = h.astype(jnp.bfloat16)
    bd = bd_ref[...]
    c, s1, s2 = c_ref[...], s1_ref[...], s2_ref[...]
    rows = x.shape[0]

    def deinterleave(val, out_ref, dil):
        if dil == 1:
            out_ref[0] = val.astype(out_ref.dtype)
            return
        for hf in range(GROUP_WIDTH // LANES):
            tmp_ref[hf] = val[:, hf * LANES:(hf + 1) * LANES]
        for r in range(dil):
            for hf in range(GROUP_WIDTH // LANES):
                out_ref[r, :, hf * LANES:(hf + 1) * LANES] = (
                    tmp_ref[hf, pl.ds(r, rows // dil, stride=dil), :].astype(out_ref.dtype))

    proj = jnp.dot(hb, w_ref[:, :ATT_COLS], preferred_element_type=jnp.float32)

    def normed_rope(col0, gain, scale):
        p = proj[:, col0:col0 + GROUP_WIDTH]
        ms = _head_sum(p * p, bd) * (1.0 / HEAD_DIM)
        n = p * lax.rsqrt(ms + NORM_EPS) * gain
        n = n * c + pltpu.roll(n, ROPE_HALF, 1) * s1 + pltpu.roll(n, GROUP_WIDTH - ROPE_HALF, 1) * s2
        return n * scale if scale != 1.0 else n

    q_refs = (q0_ref, q1_ref, q2_ref)
    k_refs = (k0_ref, k1_ref, k2_ref)
    v_refs = (v0_ref, v1_ref, v2_ref)
    for g, dil in enumerate(DILATIONS):
        deinterleave(normed_rope(g * GROUP_WIDTH, qg_ref[...], HEAD_DIM ** -0.5), q_refs[g], dil)
        deinterleave(normed_rope(ATT_WIDTH + g * GROUP_WIDTH, kg_ref[...], 1.0), k_refs[g], dil)
        col0 = 2 * ATT_WIDTH + g * GROUP_WIDTH
        deinterleave(proj[:, col0:col0 + GROUP_WIDTH], v_refs[g], dil)
    s_ref[...] = jnp.dot(hb, w_ref[:, ATT_COLS:], preferred_element_type=jnp.float32)


def _in_proj(x, ln1_g, w_in_bf16, q_gain, k_gain, rope, bd):
    b, t, _ = x.shape
    nt = t // ROW_TILE
    row2 = lambda bi, i: (0, 0)
    grp_shapes, grp_specs = [], []
    for _ in range(3):
        for dil in DILATIONS:
            grp_shapes.append(jax.ShapeDtypeStruct((b, dil, t // dil, GROUP_WIDTH), jnp.bfloat16))
            grp_specs.append(pl.BlockSpec((None, dil, ROW_TILE // dil, GROUP_WIDTH), lambda bi, i: (bi, 0, i, 0)))
    tab_spec = pl.BlockSpec((ROW_TILE, GROUP_WIDTH), lambda bi, i: (i, 0))
    outs = pl.pallas_call(
        _in_proj_kernel,
        grid=(b, nt),
        in_specs=[
            pl.BlockSpec((None, ROW_TILE, D_MODEL), lambda bi, i: (bi, i, 0)),
            pl.BlockSpec((1, D_MODEL), row2),
            pl.BlockSpec((D_MODEL, IN_COLS), row2),
            pl.BlockSpec((1, GROUP_WIDTH), row2),
            pl.BlockSpec((1, GROUP_WIDTH), row2),
            tab_spec, tab_spec, tab_spec,
            pl.BlockSpec((GROUP_WIDTH, GROUP_WIDTH), row2),
        ],
        out_specs=grp_specs + [pl.BlockSpec((None, ROW_TILE, SHIFT_COLS), lambda bi, i: (bi, i, 0))],
        out_shape=grp_shapes + [jax.ShapeDtypeStruct((b, t, SHIFT_COLS), jnp.float32)],
        scratch_shapes=[pltpu.VMEM((GROUP_WIDTH // LANES, ROW_TILE, LANES), jnp.float32)],
        compiler_params=_cparams(("parallel", "parallel")),
        name="in_proj",
    )(x, ln1_g.reshape(1, D_MODEL), w_in_bf16, q_gain, k_gain, *rope, bd)
    q = outs[0:3]
    k = outs[3:6]
    v = outs[6:9]
    return q, k, v, outs[9]


def _band_attn_kernel(q_ref, kp_ref, kc_ref, kn_ref, vp_ref, vc_ref, vn_ref, o_ref, lse_ref, *, seq_len, q_chunk):
    i = pl.program_id(1)
    kext = jnp.concatenate([kp_ref[...], kc_ref[...], kn_ref[...]], axis=0)
    vext = jnp.concatenate([vp_ref[...], vc_ref[...], vn_ref[...]], axis=0)
    kw = ATT_Q_BLOCK + 2 * BAND_HALF
    lane_head = lax.broadcasted_iota(jnp.int32, (ATT_Q_BLOCK, GROUP_WIDTH), 1) // HEAD_DIM
    qi = lax.broadcasted_iota(jnp.int32, (ATT_Q_BLOCK, kw), 0)
    kc = lax.broadcasted_iota(jnp.int32, (ATT_Q_BLOCK, kw), 1)
    diff = kc - BAND_HALF - qi
    band = (diff <= BAND_HALF) & (diff >= -BAND_HALF)
    for a in range(0, q_chunk, ATT_Q_BLOCK):
        kpos = i * q_chunk + (a - BAND_HALF) + kc
        valid = band & (kpos >= 0) & (kpos < seq_len)
        qb = q_ref[a:a + ATT_Q_BLOCK, :]
        kb = kext[a:a + kw]
        vb = vext[a:a + kw]
        heads = range(ATT_SLOTS)
        mine = [lane_head == h for h in heads]
        s = [lax.dot_general(jnp.where(mine[h], qb, jnp.zeros_like(qb)), kb, (((1,), (1,)), ((), ())),
                             preferred_element_type=jnp.float32) for h in heads]
        s = [jnp.where(valid, s_h, NEG_INF) for s_h in s]
        m = [jnp.max(s_h, -1, keepdims=True) for s_h in s]
        p = [jnp.exp(s_h - m_h) for s_h, m_h in zip(s, m)]
        den = [jnp.sum(p_h, -1, keepdims=True) for p_h in p]
        pv = [jnp.dot(p_h.astype(jnp.bfloat16), vb, preferred_element_type=jnp.float32) for p_h in p]
        acc = jnp.zeros((ATT_Q_BLOCK, GROUP_WIDTH), jnp.float32)
        lse = jnp.zeros((ATT_Q_BLOCK, GROUP_WIDTH), jnp.float32)
        for h in heads:
            acc = jnp.where(mine[h], pv[h] / den[h], acc)
            lse = jnp.where(mine[h], m[h] + jnp.log(den[h]), lse)
        o_ref[a:a + ATT_Q_BLOCK, :] = acc
        lse_ref[a:a + ATT_Q_BLOCK, :] = lse


def _band_attn(q, k, v):
    b, dil, seq_len, _ = q.shape
    ns = b * dil
    q, k, v = (z.reshape(ns, seq_len, GROUP_WIDTH) for z in (q, k, v))
    q_chunk = min(seq_len, 512)
    halo_per_chunk = q_chunk // BAND_HALF
    last_halo = seq_len // BAND_HALF - 1
    cur = pl.BlockSpec((None, q_chunk, GROUP_WIDTH), lambda s, i: (s, i, 0))
    prev = pl.BlockSpec((None, BAND_HALF, GROUP_WIDTH), lambda s, i: (s, jnp.maximum(i * halo_per_chunk - 1, 0), 0))
    nxt = pl.BlockSpec((None, BAND_HALF, GROUP_WIDTH),
                       lambda s, i: (s, jnp.minimum((i + 1) * halo_per_chunk, last_halo), 0))
    o, lse = pl.pallas_call(
        functools.partial(_band_attn_kernel, seq_len=seq_len, q_chunk=q_chunk),
        grid=(ns, seq_len // q_chunk),
        in_specs=[cur, prev, cur, nxt, prev, cur, nxt],
        out_specs=[cur, cur],
        out_shape=[jax.ShapeDtypeStruct((ns, seq_len, GROUP_WIDTH), jnp.float32)] * 2,
        compiler_params=_cparams(("parallel", "parallel")),
        name="band_attn",
    )(q, k, k, k, v, v, v)
    shape4 = (b, dil, seq_len, GROUP_WIDTH)
    return o.reshape(shape4), lse.reshape(shape4)


def _sigmoid(x):
    return 1.0 / (1.0 + jnp.exp(-x))


def _rwkv_token_terms(sp_ref, sc_ref, sn_ref, mup_ref, mun_ref, w0_ref, w2_ref, a0_ref, a2_ref, kk_ref, ka_ref, bd):
    i = pl.program_id(1)
    last = pl.num_programs(1) - 1
    p = sc_ref[...]
    rows = p.shape[0]
    row = lax.broadcasted_iota(jnp.int32, p.shape, 0)
    halo_prev = sp_ref[7:8, :] * (i > 0).astype(jnp.float32)
    halo_next = sn_ref[0:1, :] * (i < last).astype(jnp.float32)
    p_prev = jnp.where(row == 0, halo_prev, pltpu.roll(p, 1, 0))
    p_next = jnp.where(row == rows - 1, halo_next, pltpu.roll(p, rows - 1, 0))
    u = p + mup_ref[...] * (p_prev - p) + mun_ref[...] * (p_next - p)

    r = u[:, 0:RW_WIDTH]
    k = u[:, RW_WIDTH:2 * RW_WIDTH]
    v = u[:, 2 * RW_WIDTH:3 * RW_WIDTH]
    c0 = 3 * RW_WIDTH
    lw = jnp.tanh(u[:, c0:c0 + 128]).astype(jnp.bfloat16)
    la = u[:, c0 + 128:c0 + 256].astype(jnp.bfloat16)
    lg = _sigmoid(u[:, c0 + 256:c0 + 384]).astype(jnp.bfloat16)
    kk = k * kk_ref[...]
    kk = kk * lax.rsqrt(_head_sum(kk * kk, bd) + 1e-12)
    per_dir = []
    for d in range(N_DIR):
        z = w0_ref[d:d + 1, :] + jnp.dot(lw, w2_ref[d], preferred_element_type=jnp.float32)
        e = _sigmoid(z) * float(np.exp(-0.5))
        a = _sigmoid(a0_ref[d:d + 1, :] + jnp.dot(la, a2_ref[d], preferred_element_type=jnp.float32))
        kd = k * (1.0 + (a - 1.0) * ka_ref[...])
        per_dir.append((e, a, kd))
    return r, v, kk, lg, per_dir


def _rwkv_prep_kernel(sp_ref, sc_ref, sn_ref, mup_ref, mun_ref, w0_ref, w2_ref, a0_ref, a2_ref, g2_ref,
                      kk_ref, ka_ref, rk_ref, bd_ref,
                      lhs_ref, add_ref, bonus_ref, gate_ref, risk_ref,
                      r_s, v_s, nkk_s, e_s, kd_s, beta_s):
    bd = bd_ref[...]
    r, v, kk, lg, per_dir = _rwkv_token_terms(sp_ref, sc_ref, sn_ref, mup_ref, mun_ref, w0_ref, w2_ref, a0_ref,
                                              a2_ref, kk_ref, ka_ref, bd)
    rows = r.shape[0]
    r_s[...] = r
    v_s[...] = v
    nkk_s[...] = -kk
    gate_ref[...] = jnp.dot(lg, g2_ref[...], preferred_element_type=jnp.float32)
    bonus = jnp.zeros_like(v)
    for d, (e, a, kd) in enumerate(per_dir):
        e_s[d] = e
        kd_s[d] = kd
        beta_s[d] = kk * a
        bonus = bonus + _head_sum(r * kd * rk_ref[...], bd) * v
    bonus_ref[...] = bonus

    idx_t = lax.broadcasted_iota(jnp.int32, (CHUNK, CHUNK), 0)
    idx_s = lax.broadcasted_iota(jnp.int32, (CHUNK, CHUNK), 1)
    eye = (idx_t == idx_s).astype(jnp.float32)
    bf = lambda z: z.astype(jnp.bfloat16)
    mm = lambda a_, b_: jnp.dot(bf(a_), bf(b_), preferred_element_type=jnp.float32)
    mm_nt = lambda a_, b_: lax.dot_general(bf(a_), bf(b_), (((1,), (1,)), ((), ())), preferred_element_type=jnp.float32)
    mm_tn = lambda a_, b_: lax.dot_general(bf(a_), bf(b_), (((0,), (0,)), ((), ())), preferred_element_type=jnp.float32)

    def chunk_body(cp, carry):
        units = []
        for c, d in [(cp * CHUNKS_PER_ITER + j, d_) for j in range(CHUNKS_PER_ITER) for d_ in range(N_DIR)]:
            rows_c = pl.ds(pl.multiple_of(c * CHUNK, CHUNK), CHUNK)
            r_c, v_c, nkk_c = r_s[rows_c, :], v_s[rows_c, :], nkk_s[rows_c, :]
            before = (idx_s < idx_t) if d == 0 else (idx_s > idx_t)
            before_eq = (idx_s <= idx_t) if d == 0 else (idx_s >= idx_t)
            last = CHUNK - 1 if d == 0 else 0
            e_c = e_s[d, rows_c, :]
            cl = jnp.dot(before_eq.astype(jnp.float32), e_c, precision=HIGHEST, preferred_element_type=jnp.float32)
            w_inc = jnp.exp(-cl)
            w_exc = jnp.exp(e_c - cl)
            w_inv = jnp.exp(cl)
            w_tot = w_inc[last:last + 1, :]
            a_t = nkk_c * w_exc
            r_t = r_c * w_inc
            b_h = beta_s[d, rows_c, :] * w_inv
            k_h = kd_s[d, rows_c, :] * w_inv
            b_t = b_h * w_tot
            k_t = k_h * w_tot
            for h in range(RW_HEADS):
                hs = slice(h * HEAD_DIM, (h + 1) * HEAD_DIM)
                units.append(dict(chunk=c, dir=d, before=before, before_eq=before_eq, a_t=a_t[:, hs], r_t=r_t[:, hs], b_h=b_h[:, hs],
                                  k_h=k_h[:, hs], b_t=b_t[:, hs], k_t=k_t[:, hs], w_tot=w_tot[:, hs], v=v_c[:, hs]))
        each = lambda fn: [fn(un) for un in units]
        g = each(lambda un: mm_nt(jnp.concatenate([un['a_t'], un['r_t']], axis=0),
                                  jnp.concatenate([un['b_h'], un['k_h']], axis=0)))
        for un, g_u in zip(units, g):
            un['m_ab'] = jnp.where(un['before'], g_u[:CHUNK, :CHUNK], 0.0)
            un['m_ak'] = jnp.where(un['before'], g_u[:CHUNK, CHUNK:], 0.0)
            un['n_rb'] = jnp.where(un['before_eq'], g_u[CHUNK:, :CHUNK], 0.0)
            un['n_rk'] = jnp.where(un['before_eq'], g_u[CHUNK:, CHUNK:], 0.0)
        t_inv = each(lambda un: eye + un['m_ab'])
        pw = each(lambda un: un['m_ab'])
        risk = carry
        for _ in range(CHUNK.bit_length() - 2):
            pw = [mm(p_, p_) for p_ in pw]
            t_inv = [t_ + mm(t_, p_) for t_, p_ in zip(t_inv, pw)]
            for p_ in pw:
                risk = jnp.maximum(risk, jnp.abs(p_))
        for t_ in t_inv:
            risk = jnp.maximum(risk, jnp.abs(t_) * (1.0 / INVERSE_SCALE))
        mv = each(lambda un: mm(un['m_ak'], un['v']))
        tx = [mm(t_, jnp.concatenate([un['a_t'], mv_], axis=1)) for t_, un, mv_ in zip(t_inv, units, mv)]
        zero = jnp.zeros((CHUNK, HEAD_DIM), jnp.float32)
        z = [jnp.concatenate([tx_, jnp.concatenate([zero, un['v']], axis=1)], axis=0)
             for un, tx_ in zip(units, tx)]
        nx = [mm(jnp.concatenate([un['n_rb'], un['n_rk']], axis=1), z_) for un, z_ in zip(units, z)]
        bx = [mm_tn(jnp.concatenate([un['b_t'], un['k_t']], axis=0), z_) for un, z_ in zip(units, z)]
        lhs_parts, add_parts = [], []
        for un, nx_, bx_ in zip(units, nx, bx):
            ra = un['r_t'] + nx_[:, :HEAD_DIM]
            y0 = nx_[:, HEAD_DIM:]
            phi = eye * un['w_tot'] + bx_[:, :HEAD_DIM]
            psi = bx_[:, HEAD_DIM:]
            lhs_parts.append(jnp.concatenate([ra, phi], axis=0))
            add_parts.append(jnp.concatenate([y0, psi], axis=0))
        for g0 in range(0, len(units), RW_HEADS):
            c, d = units[g0]['chunk'], units[g0]['dir']
            lhs_ref[d, c] = jnp.concatenate(lhs_parts[g0:g0 + RW_HEADS], axis=1).astype(lhs_ref.dtype)
            add_ref[d, c] = jnp.concatenate(add_parts[g0:g0 + RW_HEADS], axis=1)
        return risk

    risk = lax.fori_loop(0, rows // (CHUNK * CHUNKS_PER_ITER), chunk_body, jnp.zeros((CHUNK, CHUNK), jnp.float32))
    risk_ref[...] = jnp.broadcast_to(jnp.max(jnp.max(risk, axis=0, keepdims=True), axis=1, keepdims=True), risk_ref.shape)


def _rwkv_prep(s, mu_prev, mu_next, w0, w2p, a0, a2p, g2, k_k, k_a, r_k, bd):
    b, t, _ = s.shape
    nt = t // ROW_TILE
    halo_per_tile = ROW_TILE // 8
    last_halo = t // 8 - 1
    row2 = lambda bi, i: (0, 0)
    row3 = lambda bi, i: (0, 0, 0)
    vec = lambda n: pl.BlockSpec((1, n), row2)
    chunks_per_tile = ROW_TILE // CHUNK
    cspec = pl.BlockSpec((N_DIR, None, chunks_per_tile, 2 * CHUNK, RW_WIDTH), lambda bi, i: (0, bi, i, 0, 0))
    nspec = pl.BlockSpec((None, ROW_TILE, RW_WIDTH), lambda bi, i: (bi, i, 0))
    cshape = jax.ShapeDtypeStruct((N_DIR, b, t // CHUNK, 2 * CHUNK, RW_WIDTH), jnp.float32)
    nshape = jax.ShapeDtypeStruct((b, t, RW_WIDTH), jnp.float32)
    tile_f32 = pltpu.VMEM((ROW_TILE, RW_WIDTH), jnp.float32)
    tile2_f32 = pltpu.VMEM((N_DIR, ROW_TILE, RW_WIDTH), jnp.float32)
    return pl.pallas_call(
        _rwkv_prep_kernel,
        grid=(b, nt),
        in_specs=[
            pl.BlockSpec((None, 8, SHIFT_COLS), lambda bi, i: (bi, jnp.maximum(i * halo_per_tile - 1, 0), 0)),
            pl.BlockSpec((None, ROW_TILE, SHIFT_COLS), lambda bi, i: (bi, i, 0)),
            pl.BlockSpec((None, 8, SHIFT_COLS), lambda bi, i: (bi, jnp.minimum((i + 1) * halo_per_tile, last_halo), 0)),
            vec(SHIFT_COLS), vec(SHIFT_COLS),
            pl.BlockSpec((N_DIR, RW_WIDTH), row2),
            pl.BlockSpec((N_DIR, 128, RW_WIDTH), row3),
            pl.BlockSpec((N_DIR, RW_WIDTH), row2),
            pl.BlockSpec((N_DIR, 128, RW_WIDTH), row3),
            pl.BlockSpec((GATE_LORA, RW_WIDTH), row2),
            vec(RW_WIDTH), vec(RW_WIDTH), vec(RW_WIDTH),
            pl.BlockSpec((GROUP_WIDTH, GROUP_WIDTH), row2),
        ],
        out_specs=[cspec, cspec, nspec, nspec, pl.BlockSpec((None, None, 8, LANES), lambda bi, i: (bi, i, 0, 0))],
        out_shape=[jax.ShapeDtypeStruct(cshape.shape, jnp.bfloat16), cshape, nshape, nshape,
                   jax.ShapeDtypeStruct((b, nt, 8, LANES), jnp.float32)],
        scratch_shapes=[tile_f32, tile_f32, tile_f32, tile2_f32, tile2_f32, tile2_f32],
        compiler_params=_cparams(("parallel", "parallel")),
        name="rwkv_prep",
    )(s, s, s, mu_prev, mu_next, w0, w2p, a0, a2p, g2, k_k, k_a, r_k, bd)


def _split_bf16(x):
    hi = x.astype(jnp.bfloat16)
    return hi, (x - hi.astype(jnp.float32)).astype(jnp.bfloat16)


def _rwkv_scan_kernel(lhs_f, add_f, lhs_b, add_b, y_f, y_b, state_ref):
    n_chunks = lhs_f.shape[0]

    @pl.when(pl.program_id(1) == 0)
    def _():
        state_ref[...] = jnp.zeros_like(state_ref)

    dot = lambda a_, b_: jnp.dot(a_, b_, preferred_element_type=jnp.float32)

    def chunk_body(cc, carry):
        chains = []
        for d, (lhs_ref, add_ref, c) in enumerate(((lhs_f, add_f, cc), (lhs_b, add_b, n_chunks - 1 - cc))):
            for h in range(RW_HEADS):
                hs = slice(h * HEAD_DIM, (h + 1) * HEAD_DIM)
                chains.append((d, h, c, hs, lhs_ref, add_ref))
        lhs = [lhs_ref[c, :, hs] for (_, _, c, hs, lhs_ref, _) in chains]
        ssplit = [_split_bf16(state_ref[d, h]) for (d, h, *_) in chains]
        res = [dot(l_, s_hi) + dot(l_, s_lo) for l_, (s_hi, s_lo) in zip(lhs, ssplit)]
        res = [r_ + add_ref[c, :, hs] for r_, (_, _, c, hs, _, add_ref) in zip(res, chains)]
        for r_, (d, h, *_) in zip(res, chains):
            state_ref[d, h] = r_[CHUNK:]
        for d, (y_ref, c) in enumerate(((y_f, cc), (y_b, n_chunks - 1 - cc))):
            rows = pl.ds(pl.multiple_of(c * CHUNK, CHUNK), CHUNK)
            y_ref[rows, :] = jnp.concatenate([r_[:CHUNK] for r_ in res[d * RW_HEADS:(d + 1) * RW_HEADS]], axis=1)
        return carry

    lax.fori_loop(0, n_chunks, chunk_body, 0)


def _rwkv_scan(lhs, add):
    _, b, nchunk, _, _ = lhs.shape
    per_step = ROW_TILE // CHUNK
    nc = nchunk // per_step
    fwd = pl.BlockSpec((None, None, per_step, 2 * CHUNK, RW_WIDTH), lambda bi, c: (0, bi, c, 0, 0))
    bwd = pl.BlockSpec((None, None, per_step, 2 * CHUNK, RW_WIDTH), lambda bi, c: (1, bi, nc - 1 - c, 0, 0))
    y_shape = jax.ShapeDtypeStruct((b, nchunk * CHUNK, RW_WIDTH), jnp.float32)
    return pl.pallas_call(
        _rwkv_scan_kernel,
        grid=(b, nc),
        in_specs=[fwd, fwd, bwd, bwd],
        out_specs=[pl.BlockSpec((None, ROW_TILE, RW_WIDTH), lambda bi, c: (bi, c, 0)),
                   pl.BlockSpec((None, ROW_TILE, RW_WIDTH), lambda bi, c: (bi, nc - 1 - c, 0))],
        out_shape=[y_shape, y_shape],
        scratch_shapes=[pltpu.VMEM((N_DIR, RW_HEADS, HEAD_DIM, HEAD_DIM), jnp.float32)],
        compiler_params=_cparams(("parallel", "arbitrary")),
        name="rwkv_scan",
    )(lhs, add, lhs, add)


def _rwkv_steps_prep_kernel(sp_ref, sc_ref, sn_ref, mup_ref, mun_ref, w0_ref, w2_ref, a0_ref, a2_ref,
                            kk_ref, ka_ref, bd_ref, nkkT_ref, rT_ref, wT_ref, bT_ref, kdT_ref, v_ref):
    r, v, kk, _, per_dir = _rwkv_token_terms(sp_ref, sc_ref, sn_ref, mup_ref, mun_ref, w0_ref, w2_ref, a0_ref,
                                             a2_ref, kk_ref, ka_ref, bd_ref[...])
    nkkT_ref[...] = (-kk).T
    rT_ref[...] = r.T
    v_ref[...] = v
    for d, (e, a, kd) in enumerate(per_dir):
        wT_ref[d] = jnp.exp(-e).T
        kdT_ref[d] = kd.T
        bT_ref[d] = (kk * a).T


def _rwkv_steps_kernel(nkkT_ref, rT_ref, wT_ref, bT_ref, kdT_ref, v_ref, y_ref, state_ref):
    d = pl.program_id(0)

    @pl.when(pl.program_id(2) == 0)
    def _():
        state_ref[...] = jnp.zeros_like(state_ref)

    col_refs = (nkkT_ref, wT_ref, bT_ref, kdT_ref, rT_ref)

    def run(reverse):
        def block(blk, states):
            base = (SCAN_CHUNK - SCAN_UNROLL * (blk + 1)) if reverse else SCAN_UNROLL * blk
            shift = (SCAN_CHUNK - base) % SCAN_CHUNK
            tiles = [pltpu.roll(ref[...], shift, 1) for ref in col_refs]
            rows = pl.ds(pl.multiple_of(base, SCAN_UNROLL), SCAN_UNROLL)
            v_rows = v_ref[rows, :]
            states = list(states)
            y_rows = [None] * SCAN_UNROLL
            order = range(SCAN_UNROLL - 1, -1, -1) if reverse else range(SCAN_UNROLL)
            for u in order:
                y_heads = []
                for h in range(RW_HEADS):
                    hs = slice(h * HEAD_DIM, (h + 1) * HEAD_DIM)
                    col = lambda z: jnp.broadcast_to(z[hs, u:u + 1], (HEAD_DIM, HEAD_DIM))
                    nkk, w, bb, kd, r = (col(z) for z in tiles)
                    st = states[h]
                    sa = jnp.sum(st * nkk, axis=0, keepdims=True)
                    st = st * w + bb * sa + kd * v_rows[u:u + 1, hs]
                    y_heads.append(jnp.sum(st * r, axis=0, keepdims=True))
                    states[h] = st
                y_rows[u] = jnp.concatenate(y_heads, axis=1)
            y_ref[rows, :] = jnp.concatenate(y_rows, axis=0)
            return tuple(states)

        init = tuple(state_ref[h] for h in range(RW_HEADS))
        final = lax.fori_loop(0, SCAN_CHUNK // SCAN_UNROLL, block, init)
        for h in range(RW_HEADS):
            state_ref[h] = final[h]

    @pl.when(d == 0)
    def _():
        run(False)

    @pl.when(d == 1)
    def _():
        run(True)


def _rwkv_steps(s, mu_prev, mu_next, w0, w2p, a0, a2p, k_k, k_a, bd):
    b, t, _ = s.shape
    nt = t // ROW_TILE
    halo_per_tile = ROW_TILE // 8
    last_halo = t // 8 - 1
    row2 = lambda bi, i: (0, 0)
    row3 = lambda bi, i: (0, 0, 0)
    vec = lambda n: pl.BlockSpec((1, n), row2)
    tspec = pl.BlockSpec((None, RW_WIDTH, ROW_TILE), lambda bi, i: (bi, 0, i))
    tspec2 = pl.BlockSpec((N_DIR, None, RW_WIDTH, ROW_TILE), lambda bi, i: (0, bi, 0, i))
    tshape = jax.ShapeDtypeStruct((b, RW_WIDTH, t), jnp.float32)
    tshape2 = jax.ShapeDtypeStruct((N_DIR, b, RW_WIDTH, t), jnp.float32)
    nkkT, rT, wT, bT, kdT, v = pl.pallas_call(
        _rwkv_steps_prep_kernel,
        grid=(b, nt),
        in_specs=[
            pl.BlockSpec((None, 8, SHIFT_COLS), lambda bi, i: (bi, jnp.maximum(i * halo_per_tile - 1, 0), 0)),
            pl.BlockSpec((None, ROW_TILE, SHIFT_COLS), lambda bi, i: (bi, i, 0)),
            pl.BlockSpec((None, 8, SHIFT_COLS), lambda bi, i: (bi, jnp.minimum((i + 1) * halo_per_tile, last_halo), 0)),
            vec(SHIFT_COLS), vec(SHIFT_COLS),
            pl.BlockSpec((N_DIR, RW_WIDTH), row2),
            pl.BlockSpec((N_DIR, 128, RW_WIDTH), row3),
            pl.BlockSpec((N_DIR, RW_WIDTH), row2),
            pl.BlockSpec((N_DIR, 128, RW_WIDTH), row3),
            vec(RW_WIDTH), vec(RW_WIDTH),
            pl.BlockSpec((GROUP_WIDTH, GROUP_WIDTH), row2),
        ],
        out_specs=[tspec, tspec, tspec2, tspec2, tspec2,
                   pl.BlockSpec((None, ROW_TILE, RW_WIDTH), lambda bi, i: (bi, i, 0))],
        out_shape=[tshape, tshape, tshape2, tshape2, tshape2, jax.ShapeDtypeStruct((b, t, RW_WIDTH), jnp.float32)],
        compiler_params=_cparams(("parallel", "parallel")),
        name="rwkv_steps_prep",
    )(s, s, s, mu_prev, mu_next, w0, w2p, a0, a2p, k_k, k_a, bd)
    nc = t // SCAN_CHUNK
    chunk = lambda d, c: c + d * (nc - 1 - 2 * c)
    shared = pl.BlockSpec((None, RW_WIDTH, SCAN_CHUNK), lambda d, bi, c: (bi, 0, chunk(d, c)))
    per_dir = pl.BlockSpec((None, None, RW_WIDTH, SCAN_CHUNK), lambda d, bi, c: (d, bi, 0, chunk(d, c)))
    y = pl.pallas_call(
        _rwkv_steps_kernel,
        grid=(N_DIR, b, nc),
        in_specs=[shared, shared, per_dir, per_dir, per_dir,
                  pl.BlockSpec((None, SCAN_CHUNK, RW_WIDTH), lambda d, bi, c: (bi, chunk(d, c), 0))],
        out_specs=pl.BlockSpec((None, None, SCAN_CHUNK, RW_WIDTH), lambda d, bi, c: (d, bi, chunk(d, c), 0)),
        out_shape=jax.ShapeDtypeStruct((N_DIR, b, t, RW_WIDTH), jnp.float32),
        scratch_shapes=[pltpu.VMEM((RW_HEADS, HEAD_DIM, HEAD_DIM), jnp.float32)],
        compiler_params=_cparams(("parallel", "parallel", "arbitrary")),
        name="rwkv_steps",
    )(nkkT, rT, wT, bT, kdT, v)
    return y[0], y[1]


def _out_proj_kernel(x_ref, o0_ref, l0_ref, o1_ref, l1_ref, o2_ref, l2_ref, yf_ref, yb_ref, bonus_ref, gate_ref,
                     lnxg_ref, lnxb_ref, wout_ref, ln2_ref, router_ref, bd_ref,
                     xn_ref, h2_ref, aff_ref, so1_ref, sl1_ref, so2_ref, sl2_ref):
    rows = x_ref.shape[0]

    def interleave(src_ref, dst_ref, dil):
        halves = range(GROUP_WIDTH // LANES)
        for r in range(dil):
            for hf in halves:
                dst_ref[hf, pl.ds(r, rows // dil, stride=dil), :] = src_ref[r, :, hf * LANES:(hf + 1) * LANES]
        return jnp.concatenate([dst_ref[hf] for hf in halves], axis=-1)

    o1 = interleave(o1_ref, so1_ref, DILATIONS[1])
    l1 = interleave(l1_ref, sl1_ref, DILATIONS[1])
    o2 = interleave(o2_ref, so2_ref, DILATIONS[2])
    l2 = interleave(l2_ref, sl2_ref, DILATIONS[2])
    l0 = l0_ref[0]
    m = jnp.maximum(jnp.maximum(l0, l1), l2)
    e0, e1, e2 = jnp.exp(l0 - m), jnp.exp(l1 - m), jnp.exp(l2 - m)
    att = (e0 * o0_ref[0] + e1 * o1 + e2 * o2) / (e0 + e1 + e2)

    bd = bd_ref[...]
    y = yf_ref[...] + yb_ref[...]
    mu = _head_sum(y, bd) * (1.0 / HEAD_DIM)
    dlt = y - mu
    var = _head_sum(dlt * dlt, bd) * (1.0 / HEAD_DIM)
    yn = dlt * lax.rsqrt(var + GN_EPS) * lnxg_ref[...] + lnxb_ref[...]
    rw = (yn + bonus_ref[...]) * gate_ref[...]

    mixed = (jnp.dot(att.astype(jnp.bfloat16), wout_ref[0:GROUP_WIDTH, :], preferred_element_type=jnp.float32)
             + jnp.dot(rw.astype(jnp.bfloat16), wout_ref[GROUP_WIDTH:, :], preferred_element_type=jnp.float32))
    xn = x_ref[...] + mixed
    xn_ref[...] = xn
    h2 = xn * lax.rsqrt(jnp.mean(xn * xn, -1, keepdims=True) + NORM_EPS) * ln2_ref[...]
    h2_ref[...] = h2.astype(h2_ref.dtype)
    logits = lax.dot_general(router_ref[...], h2, (((1,), (1,)), ((), ())),
                             precision=HIGHEST, preferred_element_type=jnp.float32)
    ex = jnp.exp(logits - jnp.max(logits, 0, keepdims=True))
    aff_ref[...] = ex / jnp.sum(ex, 0, keepdims=True)


def _out_proj(x, att_parts, y, bonus, gate, lnx_g, lnx_b, w_out_bf16, ln2_g, router_t, bd):
    b, t, _ = x.shape
    nt = t // ROW_TILE
    row2 = lambda bi, i: (0, 0)
    tok = lambda w: pl.BlockSpec((None, ROW_TILE, w), lambda bi, i: (bi, i, 0))
    grp = lambda dil: pl.BlockSpec((None, dil, ROW_TILE // dil, GROUP_WIDTH), lambda bi, i: (bi, 0, i, 0))
    vec = lambda n: pl.BlockSpec((1, n), row2)
    grp_specs, grp_args = [], []
    for (o, lse), dil in zip(att_parts, DILATIONS):
        grp_specs += [grp(dil), grp(dil)]
        grp_args += [o, lse]
    return pl.pallas_call(
        _out_proj_kernel,
        grid=(b, nt),
        in_specs=[tok(D_MODEL)] + grp_specs + [
            tok(RW_WIDTH), tok(RW_WIDTH), tok(RW_WIDTH), tok(RW_WIDTH), vec(RW_WIDTH), vec(RW_WIDTH),
            pl.BlockSpec((GROUP_WIDTH + RW_WIDTH, D_MODEL), row2),
            vec(D_MODEL),
            pl.BlockSpec((N_EXPERTS, D_MODEL), row2),
            pl.BlockSpec((GROUP_WIDTH, GROUP_WIDTH), row2),
        ],
        out_specs=[tok(D_MODEL), tok(D_MODEL),
                   pl.BlockSpec((N_EXPERTS, ROW_TILE), lambda bi, i: (0, bi * nt + i))],
        out_shape=[jax.ShapeDtypeStruct((b, t, D_MODEL), jnp.float32),
                   jax.ShapeDtypeStruct((b, t, D_MODEL), jnp.bfloat16),
                   jax.ShapeDtypeStruct((N_EXPERTS, b * t), jnp.float32)],
        scratch_shapes=[pltpu.VMEM((GROUP_WIDTH // LANES, ROW_TILE, LANES), jnp.float32)] * 4,
        compiler_params=_cparams(("parallel", "parallel")),
        name="out_proj",
    )(x, *grp_args, *y, bonus, gate, lnx_g, lnx_b, w_out_bf16, ln2_g, router_t, bd)


FF_CHUNK = 512


def _expert_ffn_kernel(x_ref, gate_ref, wg_ref, wu_ref, wd_ref, o_ref):
    x = x_ref[...]
    acc = jnp.zeros(o_ref.shape, jnp.float32)
    for f0 in range(0, EXPERT_FF, FF_CHUNK):
        hg = jnp.dot(x, wg_ref[:, f0:f0 + FF_CHUNK], preferred_element_type=jnp.float32)
        hu = jnp.dot(x, wu_ref[:, f0:f0 + FF_CHUNK], preferred_element_type=jnp.float32)
        hid = (hg * _sigmoid(hg) * hu).astype(jnp.bfloat16)
        acc = acc + jnp.dot(hid, wd_ref[f0:f0 + FF_CHUNK, :], preferred_element_type=jnp.float32)
    o_ref[...] = (acc * gate_ref[...]).astype(o_ref.dtype)


def _expert_ffn(xs, gate, wg, wu, wd):
    e, cap, _ = xs.shape
    tile = min(FFN_ROW_TILE, cap)
    tok = lambda w: pl.BlockSpec((None, tile, w), lambda ei, i: (ei, i, 0))
    wspec = lambda r, c: pl.BlockSpec((None, r, c), lambda ei, i: (ei, 0, 0))
    return pl.pallas_call(
        _expert_ffn_kernel,
        grid=(e, cap // tile),
        in_specs=[tok(D_MODEL), tok(1), wspec(D_MODEL, EXPERT_FF), wspec(D_MODEL, EXPERT_FF), wspec(EXPERT_FF, D_MODEL)],
        out_specs=tok(D_MODEL),
        out_shape=jax.ShapeDtypeStruct((e, cap, D_MODEL), jnp.bfloat16),
        compiler_params=_cparams(("parallel", "arbitrary")),
        name="expert_ffn",
    )(xs, gate, wg, wu, wd)


ONE_BITS_PLUS_1 = 0x3F800001
LIST_ROWS = 16


def _prefix_counts(mask_bf16, tri_ones, strict_lower):
    t = jnp.dot(mask_bf16, tri_ones, preferred_element_type=jnp.float32)
    offs = jnp.dot(strict_lower, t[:, LANES:].astype(jnp.bfloat16), preferred_element_type=jnp.float32)
    return t[:, :LANES] + offs, offs, t[:, LANES:]


def _route_kernel(aff_ref, pos_ref, offs_ref, idx_ref, gate_ref, *, cap):
    x = pltpu.bitcast(aff_ref[...], jnp.int32)
    nb = x.shape[0]

    def count_ge(thr):
        hit = (x >= thr).astype(jnp.int32)
        return jnp.sum(jnp.sum(hit, axis=0, keepdims=True), axis=1, keepdims=True)

    def bisect(_, bounds):
        lo, hi = bounds
        mid = lo + ((hi - lo) >> 1)
        ok = count_ge(mid) >= cap
        return jnp.where(ok, mid, lo), jnp.where(ok, hi, mid)

    lo0 = jnp.zeros((1, 1), jnp.int32)
    hi0 = jnp.full((1, 1), ONE_BITS_PLUS_1, jnp.int32)
    thr, _ = lax.fori_loop(0, 31, bisect, (lo0, hi0))
    above = x > thr
    tie = x == thr
    need = cap - count_ge(thr + 1)

    r_i = lax.broadcasted_iota(jnp.int32, (LANES, 2 * LANES), 0)
    c_i = lax.broadcasted_iota(jnp.int32, (LANES, 2 * LANES), 1)
    tri_ones = ((c_i >= LANES) | (r_i <= c_i)).astype(jnp.bfloat16)
    rr = lax.broadcasted_iota(jnp.int32, (nb, nb), 0)
    cc = lax.broadcasted_iota(jnp.int32, (nb, nb), 1)
    strict_lower = (cc < rr).astype(jnp.bfloat16)

    tie_f = tie.astype(jnp.float32)
    tie_incl, _, _ = _prefix_counts(tie_f.astype(jnp.bfloat16), tri_ones, strict_lower)
    sel = above | (tie & ((tie_incl - tie_f) < need.astype(jnp.float32)))
    sel_incl, offs, row_total = _prefix_counts(sel.astype(jnp.bfloat16), tri_ones, strict_lower)
    in_row = sel_incl - offs
    row_end = offs + row_total
    pos_ref[...] = jnp.where(sel, sel_incl.astype(jnp.int32) - 1, -1)
    offs_ref[...] = offs.astype(jnp.int32)

    aff_t = aff_ref[...].T
    g_hi = aff_t.astype(jnp.bfloat16)
    g_mid = (aff_t - g_hi.astype(jnp.float32)).astype(jnp.bfloat16)
    g_lo = (aff_t - g_hi.astype(jnp.float32) - g_mid.astype(jnp.float32)).astype(jnp.bfloat16)
    row_id = lax.broadcasted_iota(jnp.int32, (1, nb), 1)
    offs_t = offs.T[0:1, :].astype(jnp.int32)
    facts = jnp.concatenate([row_id >> 4, row_id & 15, offs_t >> 7, offs_t & (LANES - 1)], axis=0)
    facts = jnp.concatenate([facts.astype(jnp.float32), jnp.zeros((LIST_ROWS - 4, nb), jnp.float32)], axis=0)
    stack = jnp.concatenate([in_row.T.astype(jnp.bfloat16), facts.astype(jnp.bfloat16), g_hi, g_mid, g_lo], axis=0)
    sub = lax.broadcasted_iota(jnp.int32, (LANES, LANES), 0).astype(jnp.float32)

    def window_pair(wp, carry):
        ws = [wp * 2, wp * 2 + 1]
        slots = [(w * LANES + lax.broadcasted_iota(jnp.int32, (1, LANES), 1)).astype(jnp.float32) for w in ws]
        onehots = [((offs <= s_) & (s_ < row_end)).astype(jnp.bfloat16) for s_ in slots]
        gots = [jnp.dot(stack, oh, preferred_element_type=jnp.float32) for oh in onehots]
        for w, slot, got in zip(ws, slots, gots):
            emit(w, slot, got)
        return carry

    def emit(w, slot, got):
        counts = got[:LANES]
        fact = got[LANES:LANES + LIST_ROWS]
        gates = got[LANES + LIST_ROWS:LANES + LIST_ROWS + LANES] + got[2 * LANES + LIST_ROWS:3 * LANES + LIST_ROWS] \
            + got[3 * LANES + LIST_ROWS:]
        row = fact[0:1] * 16.0 + fact[1:2]
        local = slot - (fact[2:3] * float(LANES) + fact[3:4])
        lane = jnp.sum((counts <= local).astype(jnp.float32), axis=0, keepdims=True)
        idx = (row * float(LANES) + lane).astype(jnp.int32)
        gate = jnp.sum(jnp.where(sub == lane, gates, 0.0), axis=0, keepdims=True)
        idx_ref[w] = jnp.broadcast_to(idx, (8, LANES))
        gate_ref[w] = jnp.broadcast_to(gate, (8, LANES))

    lax.fori_loop(0, cap // (2 * LANES), window_pair, 0)


def _route(aff3, cap):
    e, nb, _ = aff3.shape
    n_win = cap // LANES
    spec = pl.BlockSpec((None, nb, LANES), lambda ei: (ei, 0, 0))
    lspec = pl.BlockSpec((None, n_win, 8, LANES), lambda ei: (ei, 0, 0, 0))
    shape = jax.ShapeDtypeStruct((e, nb, LANES), jnp.int32)
    pos, offs, idx, gate = pl.pallas_call(
        functools.partial(_route_kernel, cap=cap),
        grid=(e,),
        in_specs=[spec],
        out_specs=[spec, spec, lspec, lspec],
        out_shape=[shape, shape, jax.ShapeDtypeStruct((e, n_win, 8, LANES), jnp.int32),
                   jax.ShapeDtypeStruct((e, n_win, 8, LANES), jnp.float32)],
        compiler_params=_cparams(("parallel",)),
        name="route",
    )(aff3)
    return pos, offs, idx[:, :, 0, :].reshape(e, cap), gate[:, :, 0, :].reshape(e, cap)


COMBINE_TILE = 512
WINDOW = 64
GROUP = 16
MAX_WINDOWS = N_EXPERTS * (COMBINE_TILE // WINDOW + 1)
NO_SLOT = -(1 << 20)


def _combine_kernel(first_ref, count_ref, x_ref, pos_ref, ye_hbm, o_ref, stage, sems, win_e, win_s, n_win):
    i = pl.program_id(0)
    half = i % 2

    def fetch(h, j, e, win):
        return pltpu.make_async_copy(ye_hbm.at[e, pl.ds(win * WINDOW, WINDOW), :], stage.at[h, j], sems.at[h, j])

    def request_tile(tile, h):
        n_req = jnp.int32(0)
        for e in range(N_EXPERTS):
            first = first_ref[e, tile]

            def request_one(k, j, e=e, first=first):
                win_e[h, j] = e
                win_s[h, j] = (first + k) * WINDOW
                fetch(h, j, e, first + k).start()
                return j + 1

            n_req = lax.fori_loop(0, count_ref[e, tile], request_one, n_req)
        n_win[h] = n_req

    @pl.when(i == 0)
    def _():
        stage[...] = jnp.zeros_like(stage)
        request_tile(0, 0)

    @pl.when(i + 1 < pl.num_programs(0))
    def _():
        request_tile(i + 1, 1 - half)

    n_here = n_win[half]
    slot_iota = lax.broadcasted_iota(jnp.int32, (WINDOW, COMBINE_TILE), 0)

    def group_body(g, acc):
        onehots = []
        for u in range(GROUP):
            j = g * GROUP + u
            live = j < n_here
            pos_row = pos_ref[jnp.where(live, win_e[half, j], 0)]
            base = jnp.where(live, win_s[half, j], NO_SLOT)
            onehots.append((pos_row == base + slot_iota).astype(jnp.bfloat16))
        onehot = jnp.concatenate(onehots, axis=0)
        for u in range(GROUP):
            j = g * GROUP + u

            @pl.when(j < n_here)
            def _(j=j):
                fetch(half, j, 0, 0).wait()

        rows = stage[half, pl.ds(g * GROUP, GROUP)].reshape(GROUP * WINDOW, D_MODEL)
        return acc + lax.dot_general(onehot, rows, (((0,), (0,)), ((), ())), preferred_element_type=jnp.float32)

    o_ref[...] = lax.fori_loop(0, (n_here + GROUP - 1) // GROUP, group_body, x_ref[...])


def _combine(xn, pos, offs, ye, cap):
    n = xn.shape[0]
    nt = n // COMBINE_TILE
    rows_per_tile = COMBINE_TILE // LANES
    start = offs[:, ::rows_per_tile, 0]
    stop = jnp.concatenate([start[:, 1:], jnp.full((N_EXPERTS, 1), cap, jnp.int32)], axis=1)
    first = start // WINDOW
    count = jnp.where(stop > start, (stop - 1) // WINDOW - first + 1, 0)
    pos_t = jnp.transpose(pos.reshape(N_EXPERTS, nt, COMBINE_TILE), (1, 0, 2)).reshape(nt, N_EXPERTS, 1, COMBINE_TILE)
    tok = pl.BlockSpec((COMBINE_TILE, D_MODEL), lambda i, f, c: (i, 0))
    return pl.pallas_call(
        _combine_kernel,
        grid_spec=pltpu.PrefetchScalarGridSpec(
            num_scalar_prefetch=2,
            grid=(nt,),
            in_specs=[tok,
                      pl.BlockSpec((None, N_EXPERTS, 1, COMBINE_TILE), lambda i, f, c: (i, 0, 0, 0)),
                      pl.BlockSpec(memory_space=pl.ANY)],
            out_specs=tok,
            scratch_shapes=[pltpu.VMEM((2, MAX_WINDOWS, WINDOW, D_MODEL), jnp.bfloat16),
                            pltpu.SemaphoreType.DMA((2, MAX_WINDOWS)),
                            pltpu.SMEM((2, MAX_WINDOWS), jnp.int32),
                            pltpu.SMEM((2, MAX_WINDOWS), jnp.int32),
                            pltpu.SMEM((2,), jnp.int32)],
        ),
        out_shape=jax.ShapeDtypeStruct((n, D_MODEL), jnp.float32),
        compiler_params=_cparams(("arbitrary",)),
        name="combine",
    )(first, count, xn, pos_t, ye)


def _pad_lora(w):
    z = jnp.zeros((N_DIR, N_DIR * w.shape[1], w.shape[2]), w.dtype)
    for d in range(N_DIR):
        z = z.at[d, d * w.shape[1]:(d + 1) * w.shape[1]].set(w[d])
    return z.astype(jnp.bfloat16)


def _layer_params(l, ln1_g, ln2_g, w_in, mu_prev, mu_next, q_norm_g, k_norm_g, w0, w2, a0, a2, g2, k_k, k_a,
                  r_k, lnx_g, lnx_b, w_out, router, w_gate, w_up, w_down):
    bf = lambda z: z.astype(jnp.bfloat16)
    row = lambda z: z.reshape(1, -1)
    return dict(
        ln1_g=ln1_g[l], ln2_g=row(ln2_g[l]), w_in=bf(w_in[l]), mu_prev=row(mu_prev[l]), mu_next=row(mu_next[l]),
        q_gain=row(jnp.tile(q_norm_g[l], ATT_SLOTS)), k_gain=row(jnp.tile(k_norm_g[l], ATT_SLOTS)),
        w0=w0[l], w2=_pad_lora(w2[l]), a0=a0[l], a2=_pad_lora(a2[l]), g2=bf(g2[l]),
        k_k=row(k_k[l]), k_a=row(k_a[l]), r_k=row(r_k[l]), lnx_g=row(lnx_g[l]), lnx_b=row(lnx_b[l]),
        w_out=bf(w_out[l]), router_t=router[l].T, w_gate=bf(w_gate[l]), w_up=bf(w_up[l]), w_down=bf(w_down[l]),
    )


def _encoder_layer(x, p, rope, bd):
    b, t, _ = x.shape
    n = b * t
    q, k, v, s = _in_proj(x, p['ln1_g'], p['w_in'], p['q_gain'], p['k_gain'], rope, bd)
    att_parts = [_band_attn(q[g], k[g], v[g]) for g in range(N_DIL)]
    lhs, add, bonus, gate, risk = _rwkv_prep(
        s, p['mu_prev'], p['mu_next'], p['w0'], p['w2'], p['a0'], p['a2'], p['g2'], p['k_k'], p['k_a'], p['r_k'], bd)
    chunked_ok = jnp.max(risk) <= RISK_LIMIT
    y = lax.cond(chunked_ok,
                 lambda: tuple(_rwkv_scan(lhs, add)),
                 lambda: _rwkv_steps(s, p['mu_prev'], p['mu_next'], p['w0'], p['w2'], p['a0'], p['a2'], p['k_k'],
                                     p['k_a'], bd))
    xn, h2, aff_t = _out_proj(x, att_parts, y, bonus, gate, p['lnx_g'], p['lnx_b'], p['w_out'], p['ln2_g'],
                              p['router_t'], bd)
    cap = max(1, (EC_CAPACITY_FACTOR * n) // N_EXPERTS)
    pos, offs, idx, gate_e = _route(aff_t.reshape(N_EXPERTS, n // LANES, LANES), cap)
    xs = h2.reshape(n, D_MODEL)[idx]
    ye = _expert_ffn(xs, gate_e[..., None], p['w_gate'], p['w_up'], p['w_down'])
    return _combine(xn.reshape(n, D_MODEL), pos, offs, ye, cap).reshape(b, t, D_MODEL)


def kernel(x_prompt, x_sample, ln1_g, ln2_g, w_in, mu_prev, mu_next, q_norm_g, k_norm_g, w0, w2, a0, a2, g2,
           k_k, k_a, r_k, lnx_g, lnx_b, w_out, router, w_gate, w_up, w_down):
    depth = w_in.shape[0]
    bd = _head_sum_matrix()
    rope_p = _rope_tables(x_prompt.shape[1])
    rope_s = _rope_tables(x_sample.shape[1])
    y_prompt, y_sample = x_prompt, x_sample
    for l in range(depth):
        p = _layer_params(l, ln1_g, ln2_g, w_in, mu_prev, mu_next, q_norm_g, k_norm_g, w0, w2, a0, a2, g2, k_k,
                          k_a, r_k, lnx_g, lnx_b, w_out, router, w_gate, w_up, w_down)
        y_prompt = _encoder_layer(y_prompt, p, rope_p, bd)
        y_sample = _encoder_layer(y_sample, p, rope_s, bd)
    return (y_prompt, y_sample)
```

```python
import functools

import jax
import jax.numpy as jnp
import numpy as np
from jax import lax
from jax.experimental import pallas as pl
from jax.experimental.pallas import tpu as pltpu

D_MODEL = 1024
RW_HEADS = 4
HEAD_DIM = 64
RW_WIDTH = RW_HEADS * HEAD_DIM
DECAY_LORA = 64
AAA_LORA = 64
GATE_LORA = 128
N_DIR = 2
GN_EPS = 64e-5
ATT_SLOTS = 4
DILATIONS = (1, 4, 16)
BAND_HALF = 64
N_DIL = len(DILATIONS)
GROUP_WIDTH = ATT_SLOTS * HEAD_DIM
ATT_WIDTH = N_DIL * GROUP_WIDTH
ATT_COLS = 3 * ATT_WIDTH
SHIFT_COLS = 3 * RW_WIDTH + N_DIR * DECAY_LORA + N_DIR * AAA_LORA + GATE_LORA
IN_COLS = ATT_COLS + SHIFT_COLS
ROPE_THETA = 500000.0
ROPE_DIMS = HEAD_DIM // 4
ROPE_HALF = ROPE_DIMS // 2
N_EXPERTS = 16
EC_CAPACITY_FACTOR = 2
EXPERT_FF = 2048
NORM_EPS = 1e-6
NEG_INF = -1e30
LANES = 128

ROW_TILE = 512
ATT_Q_CHUNK = 512
ATT_Q_BLOCK = 128
ATT_BLOCKS_PER_PASS = 2
CHUNK = 64
CHUNKS_PER_ITER = 4
INVERSE_SCALE = 16.0
RISK_LIMIT = 4.0
SCAN_CHUNK = 128
SCAN_UNROLL = 8
FFN_ROW_TILE = 512
VMEM_LIMIT = 56 * 1024 * 1024

HIGHEST = lax.Precision.HIGHEST


def _cparams(sem):
    return pltpu.CompilerParams(dimension_semantics=sem, vmem_limit_bytes=VMEM_LIMIT)


def _head_sum_matrix():
    h = np.arange(GROUP_WIDTH) // HEAD_DIM
    return jnp.asarray((h[:, None] == h[None, :]).astype(np.float32), dtype=jnp.bfloat16)


def _head_sum(x, bd):
    hi = x.astype(jnp.bfloat16)
    lo = (x - hi.astype(jnp.float32)).astype(jnp.bfloat16)
    return (jnp.dot(hi, bd, preferred_element_type=jnp.float32)
            + jnp.dot(lo, bd, preferred_element_type=jnp.float32))


def _rope_tables(t):
    inv = jnp.power(jnp.float32(ROPE_THETA), -jnp.arange(ROPE_HALF, dtype=jnp.float32) * 2.0 / ROPE_DIMS)
    ang = jnp.arange(t, dtype=jnp.float32)[:, None] * inv[None, :]
    cos, sin = jnp.cos(ang), jnp.sin(ang)
    zeros = jnp.zeros((t, HEAD_DIM - ROPE_DIMS), jnp.float32)
    zero8 = jnp.zeros((t, ROPE_HALF), jnp.float32)
    c = jnp.concatenate([cos, cos, zeros + 1.0], -1)
    s1 = jnp.concatenate([zero8, sin, zeros], -1)
    s2 = jnp.concatenate([-sin, zero8, zeros], -1)
    tile = lambda z: jnp.tile(z, (1, ATT_SLOTS))
    return tile(c), tile(s1), tile(s2)


def _in_proj_kernel(x_ref, g_ref, w_ref, qg_ref, kg_ref, c_ref, s1_ref, s2_ref, bd_ref,
                    q0_ref, q1_ref, q2_ref, k0_ref, k1_ref, k2_ref, v0_ref, v1_ref, v2_ref, s_ref,
                    tmp_ref):
    x = x_ref[...]
    h = x * lax.rsqrt(jnp.mean(x * x, -1, keepdims=True) + NORM_EPS) * g_ref[...]
    hb = h.astype(jnp.bfloat16)
    bd = bd_ref[...]
    c, s1, s2 = c_ref[...], s1_ref[...], s2_ref[...]
    rows = x.shape[0]

    def deinterleave(val, out_ref, dil):
        if dil == 1:
            out_ref[0] = val.astype(out_ref.dtype)
            return
        for hf in range(GROUP_WIDTH // LANES):
            tmp_ref[hf] = val[:, hf * LANES:(hf + 1) * LANES]
        for r in range(dil):
            for hf in range(GROUP_WIDTH // LANES):
                out_ref[r, :, hf * LANES:(hf + 1) * LANES] = (
                    tmp_ref[hf, pl.ds(r, rows // dil, stride=dil), :].astype(out_ref.dtype))

    proj = jnp.dot(hb, w_ref[:, :ATT_COLS], preferred_element_type=jnp.float32)

    def normed_rope(col0, gain, scale):
        p = proj[:, col0:col0 + GROUP_WIDTH]
        ms = _head_sum(p * p, bd) * (1.0 / HEAD_DIM)
        n = p * lax.rsqrt(ms + NORM_EPS) * gain
        n = n * c + pltpu.roll(n, ROPE_HALF, 1) * s1 + pltpu.roll(n, GROUP_WIDTH - ROPE_HALF, 1) * s2
        return n * scale if scale != 1.0 else n

    q_refs = (q0_ref, q1_ref, q2_ref)
    k_refs = (k0_ref, k1_ref, k2_ref)
    v_refs = (v0_ref, v1_ref, v2_ref)
    for g, dil in enumerate(DILATIONS):
        deinterleave(normed_rope(g * GROUP_WIDTH, qg_ref[...], HEAD_DIM ** -0.5), q_refs[g], dil)
        deinterleave(normed_rope(ATT_WIDTH + g * GROUP_WIDTH, kg_ref[...], 1.0), k_refs[g], dil)
        col0 = 2 * ATT_WIDTH + g * GROUP_WIDTH
        deinterleave(proj[:, col0:col0 + GROUP_WIDTH], v_refs[g], dil)
    s_ref[...] = jnp.dot(hb, w_ref[:, ATT_COLS:], preferred_element_type=jnp.float32)


def _in_proj(x, ln1_g, w_in_bf16, q_gain, k_gain, rope, bd):
    b, t, _ = x.shape
    nt = t // ROW_TILE
    row2 = lambda bi, i: (0, 0)
    grp_shapes, grp_specs = [], []
    for _ in range(3):
        for dil in DILATIONS:
            grp_shapes.append(jax.ShapeDtypeStruct((b, dil, t // dil, GROUP_WIDTH), jnp.bfloat16))
            grp_specs.append(pl.BlockSpec((None, dil, ROW_TILE // dil, GROUP_WIDTH), lambda bi, i: (bi, 0, i, 0)))
    tab_spec = pl.BlockSpec((ROW_TILE, GROUP_WIDTH), lambda bi, i: (i, 0))
    outs = pl.pallas_call(
        _in_proj_kernel,
        grid=(b, nt),
        in_specs=[
            pl.BlockSpec((None, ROW_TILE, D_MODEL), lambda bi, i: (bi, i, 0)),
            pl.BlockSpec((1, D_MODEL), row2),
            pl.BlockSpec((D_MODEL, IN_COLS), row2),
            pl.BlockSpec((1, GROUP_WIDTH), row2),
            pl.BlockSpec((1, GROUP_WIDTH), row2),
            tab_spec, tab_spec, tab_spec,
            pl.BlockSpec((GROUP_WIDTH, GROUP_WIDTH), row2),
        ],
        out_specs=grp_specs + [pl.BlockSpec((None, ROW_TILE, SHIFT_COLS), lambda bi, i: (bi, i, 0))],
        out_shape=grp_shapes + [jax.ShapeDtypeStruct((b, t, SHIFT_COLS), jnp.float32)],
        scratch_shapes=[pltpu.VMEM((GROUP_WIDTH // LANES, ROW_TILE, LANES), jnp.float32)],
        compiler_params=_cparams(("parallel", "parallel")),
        name="in_proj",
    )(x, ln1_g.reshape(1, D_MODEL), w_in_bf16, q_gain, k_gain, *rope, bd)
    q = outs[0:3]
    k = outs[3:6]
    v = outs[6:9]
    return q, k, v, outs[9]


def _band_attn_kernel(q_ref, kp_ref, kc_ref, kn_ref, vp_ref, vc_ref, vn_ref, o_ref, lse_ref, *, seq_len, q_chunk):
    i = pl.program_id(1)
    seqs = q_ref.shape[0]
    kext = [jnp.concatenate([kp_ref[sq], kc_ref[sq], kn_ref[sq]], axis=0) for sq in range(seqs)]
    vext = [jnp.concatenate([vp_ref[sq], vc_ref[sq], vn_ref[sq]], axis=0) for sq in range(seqs)]
    kw = ATT_Q_BLOCK + 2 * BAND_HALF
    lane_head = lax.broadcasted_iota(jnp.int32, (ATT_Q_BLOCK, GROUP_WIDTH), 1) // HEAD_DIM
    qi = lax.broadcasted_iota(jnp.int32, (ATT_Q_BLOCK, kw), 0)
    kc = lax.broadcasted_iota(jnp.int32, (ATT_Q_BLOCK, kw), 1)
    diff = kc - BAND_HALF - qi
    band = (diff <= BAND_HALF) & (diff >= -BAND_HALF)
    mine = [lane_head == h for h in range(ATT_SLOTS)]
    all_blocks = [(sq, a) for sq in range(seqs) for a in range(0, q_chunk, ATT_Q_BLOCK)]
    for b0 in range(0, len(all_blocks), ATT_BLOCKS_PER_PASS):
        blocks = all_blocks[b0:b0 + ATT_BLOCKS_PER_PASS]
        pairs = [(blk, h) for blk in blocks for h in range(ATT_SLOTS)]
        valid = {(sq, a): band & (i * q_chunk + (a - BAND_HALF) + kc >= 0) & (i * q_chunk + (a - BAND_HALF) + kc < seq_len)
                 for sq, a in blocks}
        qb = {(sq, a): q_ref[sq, a:a + ATT_Q_BLOCK, :] for sq, a in blocks}
        kb = {(sq, a): kext[sq][a:a + kw] for sq, a in blocks}
        vb = {(sq, a): vext[sq][a:a + kw] for sq, a in blocks}
        s = [lax.dot_general(jnp.where(mine[h], qb[a], jnp.zeros_like(qb[a])), kb[a], (((1,), (1,)), ((), ())),
                             preferred_element_type=jnp.float32) for a, h in pairs]
        s = [jnp.where(valid[a], s_, NEG_INF) for s_, (a, _) in zip(s, pairs)]
        m = [jnp.max(s_, -1, keepdims=True) for s_ in s]
        p = [jnp.exp(s_ - m_) for s_, m_ in zip(s, m)]
        den = [jnp.sum(p_, -1, keepdims=True) for p_ in p]
        pv = [jnp.dot(p_.astype(jnp.bfloat16), vb[a], preferred_element_type=jnp.float32) for p_, (a, _) in zip(p, pairs)]
        for j, (sq, a) in enumerate(blocks):
            acc = jnp.zeros((ATT_Q_BLOCK, GROUP_WIDTH), jnp.float32)
            lse = jnp.zeros((ATT_Q_BLOCK, GROUP_WIDTH), jnp.float32)
            for h in range(ATT_SLOTS):
                n = j * ATT_SLOTS + h
                acc = jnp.where(mine[h], pv[n] * (1.0 / den[n]), acc)
                lse = jnp.where(mine[h], m[n] + jnp.log(den[n]), lse)
            o_ref[sq, a:a + ATT_Q_BLOCK, :] = acc
            lse_ref[sq, a:a + ATT_Q_BLOCK, :] = lse


def _band_attn(q, k, v):
    b, dil, seq_len, _ = q.shape
    ns = b * dil
    q, k, v = (z.reshape(ns, seq_len, GROUP_WIDTH) for z in (q, k, v))
    q_chunk = min(seq_len, ATT_Q_CHUNK)
    seqs = max(1, min(ns, ATT_Q_CHUNK // seq_len))
    halo_per_chunk = q_chunk // BAND_HALF
    last_halo = seq_len // BAND_HALF - 1
    cur = pl.BlockSpec((seqs, q_chunk, GROUP_WIDTH), lambda s, i: (s, i, 0))
    prev = pl.BlockSpec((seqs, BAND_HALF, GROUP_WIDTH), lambda s, i: (s, jnp.maximum(i * halo_per_chunk - 1, 0), 0))
    nxt = pl.BlockSpec((seqs, BAND_HALF, GROUP_WIDTH),
                       lambda s, i: (s, jnp.minimum((i + 1) * halo_per_chunk, last_halo), 0))
    o, lse = pl.pallas_call(
        functools.partial(_band_attn_kernel, seq_len=seq_len, q_chunk=q_chunk),
        grid=(ns // seqs, seq_len // q_chunk),
        in_specs=[cur, prev, cur, nxt, prev, cur, nxt],
        out_specs=[cur, cur],
        out_shape=[jax.ShapeDtypeStruct((ns, seq_len, GROUP_WIDTH), jnp.float32)] * 2,
        compiler_params=_cparams(("parallel", "parallel")),
        name="band_attn",
    )(q, k, k, k, v, v, v)
    shape4 = (b, dil, seq_len, GROUP_WIDTH)
    return o.reshape(shape4), lse.reshape(shape4)


def _sigmoid(x):
    return 1.0 / (1.0 + jnp.exp(-x))


def _rwkv_token_terms(sp_ref, sc_ref, sn_ref, mup_ref, mun_ref, w0_ref, w2_ref, a0_ref, a2_ref, kk_ref, ka_ref, bd):
    i = pl.program_id(1)
    last = pl.num_programs(1) - 1
    p = sc_ref[...]
    rows = p.shape[0]
    row = lax.broadcasted_iota(jnp.int32, p.shape, 0)
    halo_prev = sp_ref[7:8, :] * (i > 0).astype(jnp.float32)
    halo_next = sn_ref[0:1, :] * (i < last).astype(jnp.float32)
    p_prev = jnp.where(row == 0, halo_prev, pltpu.roll(p, 1, 0))
    p_next = jnp.where(row == rows - 1, halo_next, pltpu.roll(p, rows - 1, 0))
    u = p + mup_ref[...] * (p_prev - p) + mun_ref[...] * (p_next - p)

    r = u[:, 0:RW_WIDTH]
    k = u[:, RW_WIDTH:2 * RW_WIDTH]
    v = u[:, 2 * RW_WIDTH:3 * RW_WIDTH]
    c0 = 3 * RW_WIDTH
    lw = jnp.tanh(u[:, c0:c0 + 128]).astype(jnp.bfloat16)
    la = u[:, c0 + 128:c0 + 256].astype(jnp.bfloat16)
    lg = _sigmoid(u[:, c0 + 256:c0 + 384]).astype(jnp.bfloat16)
    kk = k * kk_ref[...]
    kk = kk * lax.rsqrt(_head_sum(kk * kk, bd) + 1e-12)
    per_dir = []
    for d in range(N_DIR):
        z = w0_ref[d:d + 1, :] + jnp.dot(lw, w2_ref[d], preferred_element_type=jnp.float32)
        e = _sigmoid(z) * float(np.exp(-0.5))
        a = _sigmoid(a0_ref[d:d + 1, :] + jnp.dot(la, a2_ref[d], preferred_element_type=jnp.float32))
        kd = k * (1.0 + (a - 1.0) * ka_ref[...])
        per_dir.append((e, a, kd))
    return r, v, kk, lg, per_dir


def _rwkv_prep_kernel(sp_ref, sc_ref, sn_ref, mup_ref, mun_ref, w0_ref, w2_ref, a0_ref, a2_ref, g2_ref,
                      kk_ref, ka_ref, rk_ref, bd_ref,
                      lhs_ref, add_ref, bonus_ref, gate_ref, risk_ref,
                      r_s, v_s, nkk_s, e_s, kd_s, beta_s):
    bd = bd_ref[...]
    r, v, kk, lg, per_dir = _rwkv_token_terms(sp_ref, sc_ref, sn_ref, mup_ref, mun_ref, w0_ref, w2_ref, a0_ref,
                                              a2_ref, kk_ref, ka_ref, bd)
    rows = r.shape[0]
    r_s[...] = r
    v_s[...] = v
    nkk_s[...] = -kk
    gate_ref[...] = jnp.dot(lg, g2_ref[...], preferred_element_type=jnp.float32)
    bonus = jnp.zeros_like(v)
    for d, (e, a, kd) in enumerate(per_dir):
        e_s[d] = e
        kd_s[d] = kd
        beta_s[d] = kk * a
        bonus = bonus + _head_sum(r * kd * rk_ref[...], bd) * v
    bonus_ref[...] = bonus

    idx_t = lax.broadcasted_iota(jnp.int32, (CHUNK, CHUNK), 0)
    idx_s = lax.broadcasted_iota(jnp.int32, (CHUNK, CHUNK), 1)
    eye = (idx_t == idx_s).astype(jnp.float32)
    bf = lambda z: z.astype(jnp.bfloat16)
    mm = lambda a_, b_: jnp.dot(bf(a_), bf(b_), preferred_element_type=jnp.float32)
    mm_nt = lambda a_, b_: lax.dot_general(bf(a_), bf(b_), (((1,), (1,)), ((), ())), preferred_element_type=jnp.float32)
    mm_tn = lambda a_, b_: lax.dot_general(bf(a_), bf(b_), (((0,), (0,)), ((), ())), preferred_element_type=jnp.float32)

    def chunk_body(cp, carry):
        units = []
        for c, d in [(cp * CHUNKS_PER_ITER + j, d_) for j in range(CHUNKS_PER_ITER) for d_ in range(N_DIR)]:
            rows_c = pl.ds(pl.multiple_of(c * CHUNK, CHUNK), CHUNK)
            r_c, v_c, nkk_c = r_s[rows_c, :], v_s[rows_c, :], nkk_s[rows_c, :]
            before = (idx_s < idx_t) if d == 0 else (idx_s > idx_t)
            before_eq = (idx_s <= idx_t) if d == 0 else (idx_s >= idx_t)
            last = CHUNK - 1 if d == 0 else 0
            e_c = e_s[d, rows_c, :]
            cl = jnp.dot(before_eq.astype(jnp.float32), e_c, precision=HIGHEST, preferred_element_type=jnp.float32)
            w_inc = jnp.exp(-cl)
            w_exc = jnp.exp(e_c - cl)
            w_inv = jnp.exp(cl)
            w_tot = w_inc[last:last + 1, :]
            a_t = nkk_c * w_exc
            r_t = r_c * w_inc
            b_h = beta_s[d, rows_c, :] * w_inv
            k_h = kd_s[d, rows_c, :] * w_inv
            b_t = b_h * w_tot
            k_t = k_h * w_tot
            for h in range(RW_HEADS):
                hs = slice(h * HEAD_DIM, (h + 1) * HEAD_DIM)
                units.append(dict(chunk=c, dir=d, before=before, before_eq=before_eq, a_t=a_t[:, hs], r_t=r_t[:, hs], b_h=b_h[:, hs],
                                  k_h=k_h[:, hs], b_t=b_t[:, hs], k_t=k_t[:, hs], w_tot=w_tot[:, hs], v=v_c[:, hs]))
        each = lambda fn: [fn(un) for un in units]
        g = each(lambda un: mm_nt(jnp.concatenate([un['a_t'], un['r_t']], axis=0),
                                  jnp.concatenate([un['b_h'], un['k_h']], axis=0)))
        for un, g_u in zip(units, g):
            un['m_ab'] = jnp.where(un['before'], g_u[:CHUNK, :CHUNK], 0.0)
            un['m_ak'] = jnp.where(un['before'], g_u[:CHUNK, CHUNK:], 0.0)
            un['n_rb'] = jnp.where(un['before_eq'], g_u[CHUNK:, :CHUNK], 0.0)
            un['n_rk'] = jnp.where(un['before_eq'], g_u[CHUNK:, CHUNK:], 0.0)
        t_inv = each(lambda un: eye + un['m_ab'])
        pw = each(lambda un: un['m_ab'])
        risk = carry
        for _ in range(CHUNK.bit_length() - 2):
            pw = [mm(p_, p_) for p_ in pw]
            t_inv = [t_ + mm(t_, p_) for t_, p_ in zip(t_inv, pw)]
            for p_ in pw:
                risk = jnp.maximum(risk, jnp.abs(p_))
        for t_ in t_inv:
            risk = jnp.maximum(risk, jnp.abs(t_) * (1.0 / INVERSE_SCALE))
        mv = each(lambda un: mm(un['m_ak'], un['v']))
        tx = [mm(t_, jnp.concatenate([un['a_t'], mv_], axis=1)) for t_, un, mv_ in zip(t_inv, units, mv)]
        zero = jnp.zeros((CHUNK, HEAD_DIM), jnp.float32)
        z = [jnp.concatenate([tx_, jnp.concatenate([zero, un['v']], axis=1)], axis=0)
             for un, tx_ in zip(units, tx)]
        nx = [mm(jnp.concatenate([un['n_rb'], un['n_rk']], axis=1), z_) for un, z_ in zip(units, z)]
        bx = [mm_tn(jnp.concatenate([un['b_t'], un['k_t']], axis=0), z_) for un, z_ in zip(units, z)]
        lhs_parts, add_parts = [], []
        for un, nx_, bx_ in zip(units, nx, bx):
            ra = un['r_t'] + nx_[:, :HEAD_DIM]
            y0 = nx_[:, HEAD_DIM:]
            phi = eye * un['w_tot'] + bx_[:, :HEAD_DIM]
            psi = bx_[:, HEAD_DIM:]
            lhs_parts.append(jnp.concatenate([ra, phi], axis=0))
            add_parts.append(jnp.concatenate([y0, psi], axis=0))
        for g0 in range(0, len(units), RW_HEADS):
            c, d = units[g0]['chunk'], units[g0]['dir']
            lhs_ref[d, c] = jnp.concatenate(lhs_parts[g0:g0 + RW_HEADS], axis=1).astype(lhs_ref.dtype)
            add_ref[d, c] = jnp.concatenate(add_parts[g0:g0 + RW_HEADS], axis=1)
        return risk

    risk = lax.fori_loop(0, rows // (CHUNK * CHUNKS_PER_ITER), chunk_body, jnp.zeros((CHUNK, CHUNK), jnp.float32))
    risk_ref[...] = jnp.broadcast_to(jnp.max(jnp.max(risk, axis=0, keepdims=True), axis=1, keepdims=True), risk_ref.shape)


def _rwkv_prep(s, mu_prev, mu_next, w0, w2p, a0, a2p, g2, k_k, k_a, r_k, bd):
    b, t, _ = s.shape
    nt = t // ROW_TILE
    halo_per_tile = ROW_TILE // 8
    last_halo = t // 8 - 1
    row2 = lambda bi, i: (0, 0)
    row3 = lambda bi, i: (0, 0, 0)
    vec = lambda n: pl.BlockSpec((1, n), row2)
    chunks_per_tile = ROW_TILE // CHUNK
    cspec = pl.BlockSpec((N_DIR, None, chunks_per_tile, 2 * CHUNK, RW_WIDTH), lambda bi, i: (0, bi, i, 0, 0))
    nspec = pl.BlockSpec((None, ROW_TILE, RW_WIDTH), lambda bi, i: (bi, i, 0))
    cshape = jax.ShapeDtypeStruct((N_DIR, b, t // CHUNK, 2 * CHUNK, RW_WIDTH), jnp.float32)
    nshape = jax.ShapeDtypeStruct((b, t, RW_WIDTH), jnp.float32)
    tile_f32 = pltpu.VMEM((ROW_TILE, RW_WIDTH), jnp.float32)
    tile2_f32 = pltpu.VMEM((N_DIR, ROW_TILE, RW_WIDTH), jnp.float32)
    return pl.pallas_call(
        _rwkv_prep_kernel,
        grid=(b, nt),
        in_specs=[
            pl.BlockSpec((None, 8, SHIFT_COLS), lambda bi, i: (bi, jnp.maximum(i * halo_per_tile - 1, 0), 0)),
            pl.BlockSpec((None, ROW_TILE, SHIFT_COLS), lambda bi, i: (bi, i, 0)),
            pl.BlockSpec((None, 8, SHIFT_COLS), lambda bi, i: (bi, jnp.minimum((i + 1) * halo_per_tile, last_halo), 0)),
            vec(SHIFT_COLS), vec(SHIFT_COLS),
            pl.BlockSpec((N_DIR, RW_WIDTH), row2),
            pl.BlockSpec((N_DIR, 128, RW_WIDTH), row3),
            pl.BlockSpec((N_DIR, RW_WIDTH), row2),
            pl.BlockSpec((N_DIR, 128, RW_WIDTH), row3),
            pl.BlockSpec((GATE_LORA, RW_WIDTH), row2),
            vec(RW_WIDTH), vec(RW_WIDTH), vec(RW_WIDTH),
            pl.BlockSpec((GROUP_WIDTH, GROUP_WIDTH), row2),
        ],
        out_specs=[cspec, cspec, nspec, nspec, pl.BlockSpec((None, None, 8, LANES), lambda bi, i: (bi, i, 0, 0))],
        out_shape=[jax.ShapeDtypeStruct(cshape.shape, jnp.bfloat16), cshape, nshape, nshape,
                   jax.ShapeDtypeStruct((b, nt, 8, LANES), jnp.float32)],
        scratch_shapes=[tile_f32, tile_f32, tile_f32, tile2_f32, tile2_f32, tile2_f32],
        compiler_params=_cparams(("parallel", "parallel")),
        name="rwkv_prep",
    )(s, s, s, mu_prev, mu_next, w0, w2p, a0, a2p, g2, k_k, k_a, r_k, bd)


def _split_bf16(x):
    hi = x.astype(jnp.bfloat16)
    return hi, (x - hi.astype(jnp.float32)).astype(jnp.bfloat16)


def _rwkv_scan_kernel(lhs_f, add_f, lhs_b, add_b, y_f, y_b, state_ref):
    n_chunks = lhs_f.shape[0]

    @pl.when(pl.program_id(1) == 0)
    def _():
        state_ref[...] = jnp.zeros_like(state_ref)

    dot = lambda a_, b_: jnp.dot(a_, b_, preferred_element_type=jnp.float32)

    def chunk_body(cc, carry):
        chains = []
        for d, (lhs_ref, add_ref, c) in enumerate(((lhs_f, add_f, cc), (lhs_b, add_b, n_chunks - 1 - cc))):
            for h in range(RW_HEADS):
                hs = slice(h * HEAD_DIM, (h + 1) * HEAD_DIM)
                chains.append((d, h, c, hs, lhs_ref, add_ref))
        lhs = [lhs_ref[c, :, hs] for (_, _, c, hs, lhs_ref, _) in chains]
        ssplit = [_split_bf16(state_ref[d, h]) for (d, h, *_) in chains]
        res = [dot(l_, s_hi) + dot(l_, s_lo) for l_, (s_hi, s_lo) in zip(lhs, ssplit)]
        res = [r_ + add_ref[c, :, hs] for r_, (_, _, c, hs, _, add_ref) in zip(res, chains)]
        for r_, (d, h, *_) in zip(res, chains):
            state_ref[d, h] = r_[CHUNK:]
        for d, (y_ref, c) in enumerate(((y_f, cc), (y_b, n_chunks - 1 - cc))):
            rows = pl.ds(pl.multiple_of(c * CHUNK, CHUNK), CHUNK)
            y_ref[rows, :] = jnp.concatenate([r_[:CHUNK] for r_ in res[d * RW_HEADS:(d + 1) * RW_HEADS]], axis=1)
        return carry

    lax.fori_loop(0, n_chunks, chunk_body, 0)


def _rwkv_scan(lhs, add):
    _, b, nchunk, _, _ = lhs.shape
    per_step = ROW_TILE // CHUNK
    nc = nchunk // per_step
    fwd = pl.BlockSpec((None, None, per_step, 2 * CHUNK, RW_WIDTH), lambda bi, c: (0, bi, c, 0, 0))
    bwd = pl.BlockSpec((None, None, per_step, 2 * CHUNK, RW_WIDTH), lambda bi, c: (1, bi, nc - 1 - c, 0, 0))
    y_shape = jax.ShapeDtypeStruct((b, nchunk * CHUNK, RW_WIDTH), jnp.float32)
    return pl.pallas_call(
        _rwkv_scan_kernel,
        grid=(b, nc),
        in_specs=[fwd, fwd, bwd, bwd],
        out_specs=[pl.BlockSpec((None, ROW_TILE, RW_WIDTH), lambda bi, c: (bi, c, 0)),
                   pl.BlockSpec((None, ROW_TILE, RW_WIDTH), lambda bi, c: (bi, nc - 1 - c, 0))],
        out_shape=[y_shape, y_shape],
        scratch_shapes=[pltpu.VMEM((N_DIR, RW_HEADS, HEAD_DIM, HEAD_DIM), jnp.float32)],
        compiler_params=_cparams(("parallel", "arbitrary")),
        name="rwkv_scan",
    )(lhs, add, lhs, add)


def _rwkv_steps_prep_kernel(sp_ref, sc_ref, sn_ref, mup_ref, mun_ref, w0_ref, w2_ref, a0_ref, a2_ref,
                            kk_ref, ka_ref, bd_ref, nkkT_ref, rT_ref, wT_ref, bT_ref, kdT_ref, v_ref):
    r, v, kk, _, per_dir = _rwkv_token_terms(sp_ref, sc_ref, sn_ref, mup_ref, mun_ref, w0_ref, w2_ref, a0_ref,
                                             a2_ref, kk_ref, ka_ref, bd_ref[...])
    nkkT_ref[...] = (-kk).T
    rT_ref[...] = r.T
    v_ref[...] = v
    for d, (e, a, kd) in enumerate(per_dir):
        wT_ref[d] = jnp.exp(-e).T
        kdT_ref[d] = kd.T
        bT_ref[d] = (kk * a).T


def _rwkv_steps_kernel(nkkT_ref, rT_ref, wT_ref, bT_ref, kdT_ref, v_ref, y_ref, state_ref):
    d = pl.program_id(0)

    @pl.when(pl.program_id(2) == 0)
    def _():
        state_ref[...] = jnp.zeros_like(state_ref)

    col_refs = (nkkT_ref, wT_ref, bT_ref, kdT_ref, rT_ref)

    def run(reverse):
        def block(blk, states):
            base = (SCAN_CHUNK - SCAN_UNROLL * (blk + 1)) if reverse else SCAN_UNROLL * blk
            shift = (SCAN_CHUNK - base) % SCAN_CHUNK
            tiles = [pltpu.roll(ref[...], shift, 1) for ref in col_refs]
            rows = pl.ds(pl.multiple_of(base, SCAN_UNROLL), SCAN_UNROLL)
            v_rows = v_ref[rows, :]
            states = list(states)
            y_rows = [None] * SCAN_UNROLL
            order = range(SCAN_UNROLL - 1, -1, -1) if reverse else range(SCAN_UNROLL)
            for u in order:
                y_heads = []
                for h in range(RW_HEADS):
                    hs = slice(h * HEAD_DIM, (h + 1) * HEAD_DIM)
                    col = lambda z: jnp.broadcast_to(z[hs, u:u + 1], (HEAD_DIM, HEAD_DIM))
                    nkk, w, bb, kd, r = (col(z) for z in tiles)
                    st = states[h]
                    sa = jnp.sum(st * nkk, axis=0, keepdims=True)
                    st = st * w + bb * sa + kd * v_rows[u:u + 1, hs]
                    y_heads.append(jnp.sum(st * r, axis=0, keepdims=True))
                    states[h] = st
                y_rows[u] = jnp.concatenate(y_heads, axis=1)
            y_ref[rows, :] = jnp.concatenate(y_rows, axis=0)
            return tuple(states)

        init = tuple(state_ref[h] for h in range(RW_HEADS))
        final = lax.fori_loop(0, SCAN_CHUNK // SCAN_UNROLL, block, init)
        for h in range(RW_HEADS):
            state_ref[h] = final[h]

    @pl.when(d == 0)
    def _():
        run(False)

    @pl.when(d == 1)
    def _():
        run(True)


def _rwkv_steps(s, mu_prev, mu_next, w0, w2p, a0, a2p, k_k, k_a, bd):
    b, t, _ = s.shape
    nt = t // ROW_TILE
    halo_per_tile = ROW_TILE // 8
    last_halo = t // 8 - 1
    row2 = lambda bi, i: (0, 0)
    row3 = lambda bi, i: (0, 0, 0)
    vec = lambda n: pl.BlockSpec((1, n), row2)
    tspec = pl.BlockSpec((None, RW_WIDTH, ROW_TILE), lambda bi, i: (bi, 0, i))
    tspec2 = pl.BlockSpec((N_DIR, None, RW_WIDTH, ROW_TILE), lambda bi, i: (0, bi, 0, i))
    tshape = jax.ShapeDtypeStruct((b, RW_WIDTH, t), jnp.float32)
    tshape2 = jax.ShapeDtypeStruct((N_DIR, b, RW_WIDTH, t), jnp.float32)
    nkkT, rT, wT, bT, kdT, v = pl.pallas_call(
        _rwkv_steps_prep_kernel,
        grid=(b, nt),
        in_specs=[
            pl.BlockSpec((None, 8, SHIFT_COLS), lambda bi, i: (bi, jnp.maximum(i * halo_per_tile - 1, 0), 0)),
            pl.BlockSpec((None, ROW_TILE, SHIFT_COLS), lambda bi, i: (bi, i, 0)),
            pl.BlockSpec((None, 8, SHIFT_COLS), lambda bi, i: (bi, jnp.minimum((i + 1) * halo_per_tile, last_halo), 0)),
            vec(SHIFT_COLS), vec(SHIFT_COLS),
            pl.BlockSpec((N_DIR, RW_WIDTH), row2),
            pl.BlockSpec((N_DIR, 128, RW_WIDTH), row3),
            pl.BlockSpec((N_DIR, RW_WIDTH), row2),
            pl.BlockSpec((N_DIR, 128, RW_WIDTH), row3),
            vec(RW_WIDTH), vec(RW_WIDTH),
            pl.BlockSpec((GROUP_WIDTH, GROUP_WIDTH), row2),
        ],
        out_specs=[tspec, tspec, tspec2, tspec2, tspec2,
                   pl.BlockSpec((None, ROW_TILE, RW_WIDTH), lambda bi, i: (bi, i, 0))],
        out_shape=[tshape, tshape, tshape2, tshape2, tshape2, jax.ShapeDtypeStruct((b, t, RW_WIDTH), jnp.float32)],
        compiler_params=_cparams(("parallel", "parallel")),
        name="rwkv_steps_prep",
    )(s, s, s, mu_prev, mu_next, w0, w2p, a0, a2p, k_k, k_a, bd)
    nc = t // SCAN_CHUNK
    chunk = lambda d, c: c + d * (nc - 1 - 2 * c)
    shared = pl.BlockSpec((None, RW_WIDTH, SCAN_CHUNK), lambda d, bi, c: (bi, 0, chunk(d, c)))
    per_dir = pl.BlockSpec((None, None, RW_WIDTH, SCAN_CHUNK), lambda d, bi, c: (d, bi, 0, chunk(d, c)))
    y = pl.pallas_call(
        _rwkv_steps_kernel,
        grid=(N_DIR, b, nc),
        in_specs=[shared, shared, per_dir, per_dir, per_dir,
                  pl.BlockSpec((None, SCAN_CHUNK, RW_WIDTH), lambda d, bi, c: (bi, chunk(d, c), 0))],
        out_specs=pl.BlockSpec((None, None, SCAN_CHUNK, RW_WIDTH), lambda d, bi, c: (d, bi, chunk(d, c), 0)),
        out_shape=jax.ShapeDtypeStruct((N_DIR, b, t, RW_WIDTH), jnp.float32),
        scratch_shapes=[pltpu.VMEM((RW_HEADS, HEAD_DIM, HEAD_DIM), jnp.float32)],
        compiler_params=_cparams(("parallel", "parallel", "arbitrary")),
        name="rwkv_steps",
    )(nkkT, rT, wT, bT, kdT, v)
    return y[0], y[1]


def _out_proj_kernel(x_ref, o0_ref, l0_ref, o1_ref, l1_ref, o2_ref, l2_ref, yf_ref, yb_ref, bonus_ref, gate_ref,
                     lnxg_ref, lnxb_ref, wout_ref, ln2_ref, router_ref, bd_ref,
                     xn_ref, h2_ref, aff_ref, so1_ref, sl1_ref, so2_ref, sl2_ref):
    rows = x_ref.shape[0]

    def interleave(src_ref, dst_ref, dil):
        halves = range(GROUP_WIDTH // LANES)
        for r in range(dil):
            for hf in halves:
                dst_ref[hf, pl.ds(r, rows // dil, stride=dil), :] = src_ref[r, :, hf * LANES:(hf + 1) * LANES]
        return jnp.concatenate([dst_ref[hf] for hf in halves], axis=-1)

    o1 = interleave(o1_ref, so1_ref, DILATIONS[1])
    l1 = interleave(l1_ref, sl1_ref, DILATIONS[1])
    o2 = interleave(o2_ref, so2_ref, DILATIONS[2])
    l2 = interleave(l2_ref, sl2_ref, DILATIONS[2])
    l0 = l0_ref[0]
    m = jnp.maximum(jnp.maximum(l0, l1), l2)
    e0, e1, e2 = jnp.exp(l0 - m), jnp.exp(l1 - m), jnp.exp(l2 - m)
    att = (e0 * o0_ref[0] + e1 * o1 + e2 * o2) / (e0 + e1 + e2)

    bd = bd_ref[...]
    y = yf_ref[...] + yb_ref[...]
    mu = _head_sum(y, bd) * (1.0 / HEAD_DIM)
    dlt = y - mu
    var = _head_sum(dlt * dlt, bd) * (1.0 / HEAD_DIM)
    yn = dlt * lax.rsqrt(var + GN_EPS) * lnxg_ref[...] + lnxb_ref[...]
    rw = (yn + bonus_ref[...]) * gate_ref[...]

    mixed = (jnp.dot(att.astype(jnp.bfloat16), wout_ref[0:GROUP_WIDTH, :], preferred_element_type=jnp.float32)
             + jnp.dot(rw.astype(jnp.bfloat16), wout_ref[GROUP_WIDTH:, :], preferred_element_type=jnp.float32))
    xn = x_ref[...] + mixed
    xn_ref[...] = xn
    h2 = xn * lax.rsqrt(jnp.mean(xn * xn, -1, keepdims=True) + NORM_EPS) * ln2_ref[...]
    h2_ref[...] = h2.astype(h2_ref.dtype)
    logits = lax.dot_general(router_ref[...], h2, (((1,), (1,)), ((), ())),
                             precision=HIGHEST, preferred_element_type=jnp.float32)
    ex = jnp.exp(logits - jnp.max(logits, 0, keepdims=True))
    aff_ref[...] = ex / jnp.sum(ex, 0, keepdims=True)


def _out_proj(x, att_parts, y, bonus, gate, lnx_g, lnx_b, w_out_bf16, ln2_g, router_t, bd):
    b, t, _ = x.shape
    nt = t // ROW_TILE
    row2 = lambda bi, i: (0, 0)
    tok = lambda w: pl.BlockSpec((None, ROW_TILE, w), lambda bi, i: (bi, i, 0))
    grp = lambda dil: pl.BlockSpec((None, dil, ROW_TILE // dil, GROUP_WIDTH), lambda bi, i: (bi, 0, i, 0))
    vec = lambda n: pl.BlockSpec((1, n), row2)
    grp_specs, grp_args = [], []
    for (o, lse), dil in zip(att_parts, DILATIONS):
        grp_specs += [grp(dil), grp(dil)]
        grp_args += [o, lse]
    return pl.pallas_call(
        _out_proj_kernel,
        grid=(b, nt),
        in_specs=[tok(D_MODEL)] + grp_specs + [
            tok(RW_WIDTH), tok(RW_WIDTH), tok(RW_WIDTH), tok(RW_WIDTH), vec(RW_WIDTH), vec(RW_WIDTH),
            pl.BlockSpec((GROUP_WIDTH + RW_WIDTH, D_MODEL), row2),
            vec(D_MODEL),
            pl.BlockSpec((N_EXPERTS, D_MODEL), row2),
            pl.BlockSpec((GROUP_WIDTH, GROUP_WIDTH), row2),
        ],
        out_specs=[tok(D_MODEL), tok(D_MODEL),
                   pl.BlockSpec((N_EXPERTS, ROW_TILE), lambda bi, i: (0, bi * nt + i))],
        out_shape=[jax.ShapeDtypeStruct((b, t, D_MODEL), jnp.float32),
                   jax.ShapeDtypeStruct((b, t, D_MODEL), jnp.bfloat16),
                   jax.ShapeDtypeStruct((N_EXPERTS, b * t), jnp.float32)],
        scratch_shapes=[pltpu.VMEM((GROUP_WIDTH // LANES, ROW_TILE, LANES), jnp.float32)] * 4,
        compiler_params=_cparams(("parallel", "parallel")),
        name="out_proj",
    )(x, *grp_args, *y, bonus, gate, lnx_g, lnx_b, w_out_bf16, ln2_g, router_t, bd)


FF_CHUNK = 512


def _expert_ffn_kernel(x_ref, gate_ref, wg_ref, wu_ref, wd_ref, o_ref):
    x = x_ref[...]
    acc = jnp.zeros(o_ref.shape, jnp.float32)
    for f0 in range(0, EXPERT_FF, FF_CHUNK):
        hg = jnp.dot(x, wg_ref[:, f0:f0 + FF_CHUNK], preferred_element_type=jnp.float32)
        hu = jnp.dot(x, wu_ref[:, f0:f0 + FF_CHUNK], preferred_element_type=jnp.float32)
        hid = (hg * _sigmoid(hg) * hu).astype(jnp.bfloat16)
        acc = acc + jnp.dot(hid, wd_ref[f0:f0 + FF_CHUNK, :], preferred_element_type=jnp.float32)
    o_ref[...] = (acc * gate_ref[...]).astype(o_ref.dtype)


def _expert_ffn(xs, gate, wg, wu, wd):
    e, cap, _ = xs.shape
    tile = min(FFN_ROW_TILE, cap)
    tok = lambda w: pl.BlockSpec((None, tile, w), lambda ei, i: (ei, i, 0))
    wspec = lambda r, c: pl.BlockSpec((None, r, c), lambda ei, i: (ei, 0, 0))
    return pl.pallas_call(
        _expert_ffn_kernel,
        grid=(e, cap // tile),
        in_specs=[tok(D_MODEL), tok(1), wspec(D_MODEL, EXPERT_FF), wspec(D_MODEL, EXPERT_FF), wspec(EXPERT_FF, D_MODEL)],
        out_specs=tok(D_MODEL),
        out_shape=jax.ShapeDtypeStruct((e, cap, D_MODEL), jnp.bfloat16),
        compiler_params=_cparams(("parallel", "arbitrary")),
        name="expert_ffn",
    )(xs, gate, wg, wu, wd)


ONE_BITS_PLUS_1 = 0x3F800001
LIST_ROWS = 16


def _prefix_counts(mask_bf16, tri_ones, strict_lower):
    t = jnp.dot(mask_bf16, tri_ones, preferred_element_type=jnp.float32)
    offs = jnp.dot(strict_lower, t[:, LANES:].astype(jnp.bfloat16), preferred_element_type=jnp.float32)
    return t[:, :LANES] + offs, offs, t[:, LANES:]


def _route_kernel(aff_ref, pos_ref, offs_ref, idx_ref, gate_ref, *, cap):
    x = pltpu.bitcast(aff_ref[...], jnp.int32)
    nb = x.shape[0]

    def count_ge(thr):
        hit = (x >= thr).astype(jnp.int32)
        return jnp.sum(jnp.sum(hit, axis=0, keepdims=True), axis=1, keepdims=True)

    def bisect(_, bounds):
        lo, hi = bounds
        mid = lo + ((hi - lo) >> 1)
        ok = count_ge(mid) >= cap
        return jnp.where(ok, mid, lo), jnp.where(ok, hi, mid)

    lo0 = jnp.zeros((1, 1), jnp.int32)
    hi0 = jnp.full((1, 1), ONE_BITS_PLUS_1, jnp.int32)
    thr, _ = lax.fori_loop(0, 31, bisect, (lo0, hi0))
    above = x > thr
    tie = x == thr
    need = cap - count_ge(thr + 1)

    r_i = lax.broadcasted_iota(jnp.int32, (LANES, 2 * LANES), 0)
    c_i = lax.broadcasted_iota(jnp.int32, (LANES, 2 * LANES), 1)
    tri_ones = ((c_i >= LANES) | (r_i <= c_i)).astype(jnp.bfloat16)
    rr = lax.broadcasted_iota(jnp.int32, (nb, nb), 0)
    cc = lax.broadcasted_iota(jnp.int32, (nb, nb), 1)
    strict_lower = (cc < rr).astype(jnp.bfloat16)

    tie_f = tie.astype(jnp.float32)
    tie_incl, _, _ = _prefix_counts(tie_f.astype(jnp.bfloat16), tri_ones, strict_lower)
    sel = above | (tie & ((tie_incl - tie_f) < need.astype(jnp.float32)))
    sel_incl, offs, row_total = _prefix_counts(sel.astype(jnp.bfloat16), tri_ones, strict_lower)
    in_row = sel_incl - offs
    row_end = offs + row_total
    pos_ref[...] = jnp.where(sel, sel_incl.astype(jnp.int32) - 1, -1)
    offs_ref[...] = offs.astype(jnp.int32)

    aff_t = aff_ref[...].T
    g_hi = aff_t.astype(jnp.bfloat16)
    g_mid = (aff_t - g_hi.astype(jnp.float32)).astype(jnp.bfloat16)
    g_lo = (aff_t - g_hi.astype(jnp.float32) - g_mid.astype(jnp.float32)).astype(jnp.bfloat16)
    row_id = lax.broadcasted_iota(jnp.int32, (1, nb), 1)
    offs_t = offs.T[0:1, :].astype(jnp.int32)
    facts = jnp.concatenate([row_id >> 4, row_id & 15, offs_t >> 7, offs_t & (LANES - 1)], axis=0)
    facts = jnp.concatenate([facts.astype(jnp.float32), jnp.zeros((LIST_ROWS - 4, nb), jnp.float32)], axis=0)
    stack = jnp.concatenate([in_row.T.astype(jnp.bfloat16), facts.astype(jnp.bfloat16), g_hi, g_mid, g_lo], axis=0)
    sub = lax.broadcasted_iota(jnp.int32, (LANES, LANES), 0).astype(jnp.float32)

    def window_pair(wp, carry):
        ws = [wp * 2, wp * 2 + 1]
        slots = [(w * LANES + lax.broadcasted_iota(jnp.int32, (1, LANES), 1)).astype(jnp.float32) for w in ws]
        onehots = [((offs <= s_) & (s_ < row_end)).astype(jnp.bfloat16) for s_ in slots]
        gots = [jnp.dot(stack, oh, preferred_element_type=jnp.float32) for oh in onehots]
        for w, slot, got in zip(ws, slots, gots):
            emit(w, slot, got)
        return carry

    def emit(w, slot, got):
        counts = got[:LANES]
        fact = got[LANES:LANES + LIST_ROWS]
        gates = got[LANES + LIST_ROWS:LANES + LIST_ROWS + LANES] + got[2 * LANES + LIST_ROWS:3 * LANES + LIST_ROWS] \
            + got[3 * LANES + LIST_ROWS:]
        row = fact[0:1] * 16.0 + fact[1:2]
        local = slot - (fact[2:3] * float(LANES) + fact[3:4])
        lane = jnp.sum((counts <= local).astype(jnp.float32), axis=0, keepdims=True)
        idx = (row * float(LANES) + lane).astype(jnp.int32)
        gate = jnp.sum(jnp.where(sub == lane, gates, 0.0), axis=0, keepdims=True)
        idx_ref[w] = jnp.broadcast_to(idx, (8, LANES))
        gate_ref[w] = jnp.broadcast_to(gate, (8, LANES))

    lax.fori_loop(0, cap // (2 * LANES), window_pair, 0)


def _route(aff3, cap):
    e, nb, _ = aff3.shape
    n_win = cap // LANES
    spec = pl.BlockSpec((None, nb, LANES), lambda ei: (ei, 0, 0))
    lspec = pl.BlockSpec((None, n_win, 8, LANES), lambda ei: (ei, 0, 0, 0))
    shape = jax.ShapeDtypeStruct((e, nb, LANES), jnp.int32)
    pos, offs, idx, gate = pl.pallas_call(
        functools.partial(_route_kernel, cap=cap),
        grid=(e,),
        in_specs=[spec],
        out_specs=[spec, spec, lspec, lspec],
        out_shape=[shape, shape, jax.ShapeDtypeStruct((e, n_win, 8, LANES), jnp.int32),
                   jax.ShapeDtypeStruct((e, n_win, 8, LANES), jnp.float32)],
        compiler_params=_cparams(("parallel",)),
        name="route",
    )(aff3)
    return pos, offs, idx[:, :, 0, :].reshape(e, cap), gate[:, :, 0, :].reshape(e, cap)


COMBINE_TILE = 512
WINDOW = 64
GROUP = 16
MAX_WINDOWS = N_EXPERTS * (COMBINE_TILE // WINDOW + 1)
NO_SLOT = -(1 << 20)


def _combine_kernel(first_ref, count_ref, x_ref, pos_ref, ye_hbm, o_ref, stage, sems, win_e, win_s, n_win):
    i = pl.program_id(0)
    half = i % 2

    def fetch(h, j, e, win):
        return pltpu.make_async_copy(ye_hbm.at[e, pl.ds(win * WINDOW, WINDOW), :], stage.at[h, j], sems.at[h, j])

    def request_tile(tile, h):
        n_req = jnp.int32(0)
        for e in range(N_EXPERTS):
            first = first_ref[e, tile]

            def request_one(k, j, e=e, first=first):
                win_e[h, j] = e
                win_s[h, j] = (first + k) * WINDOW
                fetch(h, j, e, first + k).start()
                return j + 1

            n_req = lax.fori_loop(0, count_ref[e, tile], request_one, n_req)
        n_win[h] = n_req

    @pl.when(i == 0)
    def _():
        stage[...] = jnp.zeros_like(stage)
        request_tile(0, 0)

    @pl.when(i + 1 < pl.num_programs(0))
    def _():
        request_tile(i + 1, 1 - half)

    n_here = n_win[half]
    slot_iota = lax.broadcasted_iota(jnp.int32, (WINDOW, COMBINE_TILE), 0)

    def group_body(g, acc):
        onehots = []
        for u in range(GROUP):
            j = g * GROUP + u
            live = j < n_here
            pos_row = pos_ref[jnp.where(live, win_e[half, j], 0)]
            base = jnp.where(live, win_s[half, j], NO_SLOT)
            onehots.append((pos_row == base + slot_iota).astype(jnp.bfloat16))
        onehot = jnp.concatenate(onehots, axis=0)
        for u in range(GROUP):
            j = g * GROUP + u

            @pl.when(j < n_here)
            def _(j=j):
                fetch(half, j, 0, 0).wait()

        rows = stage[half, pl.ds(g * GROUP, GROUP)].reshape(GROUP * WINDOW, D_MODEL)
        return acc + lax.dot_general(onehot, rows, (((0,), (0,)), ((), ())), preferred_element_type=jnp.float32)

    o_ref[...] = lax.fori_loop(0, (n_here + GROUP - 1) // GROUP, group_body, x_ref[...])


def _combine(xn, pos, offs, ye, cap):
    n = xn.shape[0]
    nt = n // COMBINE_TILE
    rows_per_tile = COMBINE_TILE // LANES
    start = offs[:, ::rows_per_tile, 0]
    stop = jnp.concatenate([start[:, 1:], jnp.full((N_EXPERTS, 1), cap, jnp.int32)], axis=1)
    first = start // WINDOW
    count = jnp.where(stop > start, (stop - 1) // WINDOW - first + 1, 0)
    pos_t = jnp.transpose(pos.reshape(N_EXPERTS, nt, COMBINE_TILE), (1, 0, 2)).reshape(nt, N_EXPERTS, 1, COMBINE_TILE)
    tok = pl.BlockSpec((COMBINE_TILE, D_MODEL), lambda i, f, c: (i, 0))
    return pl.pallas_call(
        _combine_kernel,
        grid_spec=pltpu.PrefetchScalarGridSpec(
            num_scalar_prefetch=2,
            grid=(nt,),
            in_specs=[tok,
                      pl.BlockSpec((None, N_EXPERTS, 1, COMBINE_TILE), lambda i, f, c: (i, 0, 0, 0)),
                      pl.BlockSpec(memory_space=pl.ANY)],
            out_specs=tok,
            scratch_shapes=[pltpu.VMEM((2, MAX_WINDOWS, WINDOW, D_MODEL), jnp.bfloat16),
                            pltpu.SemaphoreType.DMA((2, MAX_WINDOWS)),
                            pltpu.SMEM((2, MAX_WINDOWS), jnp.int32),
                            pltpu.SMEM((2, MAX_WINDOWS), jnp.int32),
                            pltpu.SMEM((2,), jnp.int32)],
        ),
        out_shape=jax.ShapeDtypeStruct((n, D_MODEL), jnp.float32),
        compiler_params=_cparams(("arbitrary",)),
        name="combine",
    )(first, count, xn, pos_t, ye)


def _pad_lora(w):
    z = jnp.zeros((N_DIR, N_DIR * w.shape[1], w.shape[2]), w.dtype)
    for d in range(N_DIR):
        z = z.at[d, d * w.shape[1]:(d + 1) * w.shape[1]].set(w[d])
    return z.astype(jnp.bfloat16)


def _layer_params(l, ln1_g, ln2_g, w_in, mu_prev, mu_next, q_norm_g, k_norm_g, w0, w2, a0, a2, g2, k_k, k_a,
                  r_k, lnx_g, lnx_b, w_out, router, w_gate, w_up, w_down):
    bf = lambda z: z.astype(jnp.bfloat16)
    row = lambda z: z.reshape(1, -1)
    return dict(
        ln1_g=ln1_g[l], ln2_g=row(ln2_g[l]), w_in=bf(w_in[l]), mu_prev=row(mu_prev[l]), mu_next=row(mu_next[l]),
        q_gain=row(jnp.tile(q_norm_g[l], ATT_SLOTS)), k_gain=row(jnp.tile(k_norm_g[l], ATT_SLOTS)),
        w0=w0[l], w2=_pad_lora(w2[l]), a0=a0[l], a2=_pad_lora(a2[l]), g2=bf(g2[l]),
        k_k=row(k_k[l]), k_a=row(k_a[l]), r_k=row(r_k[l]), lnx_g=row(lnx_g[l]), lnx_b=row(lnx_b[l]),
        w_out=bf(w_out[l]), router_t=router[l].T, w_gate=bf(w_gate[l]), w_up=bf(w_up[l]), w_down=bf(w_down[l]),
    )


def _encoder_layer(x, p, rope, bd):
    b, t, _ = x.shape
    n = b * t
    q, k, v, s = _in_proj(x, p['ln1_g'], p['w_in'], p['q_gain'], p['k_gain'], rope, bd)
    att_parts = [_band_attn(q[g], k[g], v[g]) for g in range(N_DIL)]
    lhs, add, bonus, gate, risk = _rwkv_prep(
        s, p['mu_prev'], p['mu_next'], p['w0'], p['w2'], p['a0'], p['a2'], p['g2'], p['k_k'], p['k_a'], p['r_k'], bd)
    chunked_ok = jnp.max(risk) <= RISK_LIMIT
    y = lax.cond(chunked_ok,
                 lambda: tuple(_rwkv_scan(lhs, add)),
                 lambda: _rwkv_steps(s, p['mu_prev'], p['mu_next'], p['w0'], p['w2'], p['a0'], p['a2'], p['k_k'],
                                     p['k_a'], bd))
    xn, h2, aff_t = _out_proj(x, att_parts, y, bonus, gate, p['lnx_g'], p['lnx_b'], p['w_out'], p['ln2_g'],
                              p['router_t'], bd)
    cap = max(1, (EC_CAPACITY_FACTOR * n) // N_EXPERTS)
    pos, offs, idx, gate_e = _route(aff_t.reshape(N_EXPERTS, n // LANES, LANES), cap)
    xs = h2.reshape(n, D_MODEL)[idx]
    ye = _expert_ffn(xs, gate_e[..., None], p['w_gate'], p['w_up'], p['w_down'])
    return _combine(xn.reshape(n, D_MODEL), pos, offs, ye, cap).reshape(b, t, D_MODEL)


def kernel(x_prompt, x_sample, ln1_g, ln2_g, w_in, mu_prev, mu_next, q_norm_g, k_norm_g, w0, w2, a0, a2, g2,
           k_k, k_a, r_k, lnx_g, lnx_b, w_out, router, w_gate, w_up, w_down):
    depth = w_in.shape[0]
    bd = _head_sum_matrix()
    rope_p = _rope_tables(x_prompt.shape[1])
    rope_s = _rope_tables(x_sample.shape[1])
    y_prompt, y_sample = x_prompt, x_sample
    for l in range(depth):
        p = _layer_params(l, ln1_g, ln2_g, w_in, mu_prev, mu_next, q_norm_g, k_norm_g, w0, w2, a0, a2, g2, k_k,
                          k_a, r_k, lnx_g, lnx_b, w_out, router, w_gate, w_up, w_down)
        y_prompt = _encoder_layer(y_prompt, p, rope_p, bd)
        y_sample = _encoder_layer(y_sample, p, rope_s, bd)
    return (y_prompt, y_sample)
```

```python
import functools

import jax
import jax.numpy as jnp
import numpy as np
from jax import lax
from jax.experimental import pallas as pl
from jax.experimental.pallas import tpu as pltpu

D_MODEL = 1024
RW_HEADS = 4
HEAD_DIM = 64
RW_WIDTH = RW_HEADS * HEAD_DIM
DECAY_LORA = 64
AAA_LORA = 64
GATE_LORA = 128
N_DIR = 2
GN_EPS = 64e-5
ATT_SLOTS = 4
DILATIONS = (1, 4, 16)
BAND_HALF = 64
N_DIL = len(DILATIONS)
GROUP_WIDTH = ATT_SLOTS * HEAD_DIM
ATT_WIDTH = N_DIL * GROUP_WIDTH
ATT_COLS = 3 * ATT_WIDTH
SHIFT_COLS = 3 * RW_WIDTH + N_DIR * DECAY_LORA + N_DIR * AAA_LORA + GATE_LORA
IN_COLS = ATT_COLS + SHIFT_COLS
ROPE_THETA = 500000.0
ROPE_DIMS = HEAD_DIM // 4
ROPE_HALF = ROPE_DIMS // 2
N_EXPERTS = 16
EC_CAPACITY_FACTOR = 2
EXPERT_FF = 2048
NORM_EPS = 1e-6
NEG_INF = -1e30
LANES = 128

ROW_TILE = 512
ATT_Q_CHUNK = 512
ATT_Q_BLOCK = 128
ATT_BLOCKS_PER_PASS = 2
CHUNK = 64
CHUNKS_PER_ITER = 4
INVERSE_SCALE = 16.0
RISK_LIMIT = 4.0
SCAN_CHUNK = 128
SCAN_UNROLL = 8
FFN_ROW_TILE = 512
VMEM_LIMIT = 56 * 1024 * 1024

HIGHEST = lax.Precision.HIGHEST


def _cparams(sem):
    return pltpu.CompilerParams(dimension_semantics=sem, vmem_limit_bytes=VMEM_LIMIT)


def _head_sum_matrix():
    h = np.arange(GROUP_WIDTH) // HEAD_DIM
    return jnp.asarray((h[:, None] == h[None, :]).astype(np.float32), dtype=jnp.bfloat16)


def _head_sum(x, bd):
    hi = x.astype(jnp.bfloat16)
    lo = (x - hi.astype(jnp.float32)).astype(jnp.bfloat16)
    return (jnp.dot(hi, bd, preferred_element_type=jnp.float32)
            + jnp.dot(lo, bd, preferred_element_type=jnp.float32))


def _rope_tables(t):
    inv = jnp.power(jnp.float32(ROPE_THETA), -jnp.arange(ROPE_HALF, dtype=jnp.float32) * 2.0 / ROPE_DIMS)
    ang = jnp.arange(t, dtype=jnp.float32)[:, None] * inv[None, :]
    cos, sin = jnp.cos(ang), jnp.sin(ang)
    zeros = jnp.zeros((t, HEAD_DIM - ROPE_DIMS), jnp.float32)
    zero8 = jnp.zeros((t, ROPE_HALF), jnp.float32)
    c = jnp.concatenate([cos, cos, zeros + 1.0], -1)
    s1 = jnp.concatenate([zero8, sin, zeros], -1)
    s2 = jnp.concatenate([-sin, zero8, zeros], -1)
    tile = lambda z: jnp.tile(z, (1, ATT_SLOTS))
    return tile(c), tile(s1), tile(s2)


def _in_proj_kernel(x_ref, g_ref, w_ref, qg_ref, kg_ref, c_ref, s1_ref, s2_ref, bd_ref,
                    q0_ref, q1_ref, q2_ref, k0_ref, k1_ref, k2_ref, v0_ref, v1_ref, v2_ref, s_ref,
                    tmp_ref):
    x = x_ref[...]
    h = x * lax.rsqrt(jnp.mean(x * x, -1, keepdims=True) + NORM_EPS) * g_ref[...]
    hb = h.astype(jnp.bfloat16)
    bd = bd_ref[...]
    c, s1, s2 = c_ref[...], s1_ref[...], s2_ref[...]
    rows = x.shape[0]

    def deinterleave(val, out_ref, dil):
        if dil == 1:
            out_ref[0] = val.astype(out_ref.dtype)
            return
        for hf in range(GROUP_WIDTH // LANES):
            tmp_ref[hf] = val[:, hf * LANES:(hf + 1) * LANES]
        for r in range(dil):
            for hf in range(GROUP_WIDTH // LANES):
                out_ref[r, :, hf * LANES:(hf + 1) * LANES] = (
                    tmp_ref[hf, pl.ds(r, rows // dil, stride=dil), :].astype(out_ref.dtype))

    proj = jnp.dot(hb, w_ref[:, :ATT_COLS], preferred_element_type=jnp.float32)

    def normed_rope(col0, gain, scale):
        p = proj[:, col0:col0 + GROUP_WIDTH]
        ms = _head_sum(p * p, bd) * (1.0 / HEAD_DIM)
        n = p * lax.rsqrt(ms + NORM_EPS) * gain
        n = n * c + pltpu.roll(n, ROPE_HALF, 1) * s1 + pltpu.roll(n, GROUP_WIDTH - ROPE_HALF, 1) * s2
        return n * scale if scale != 1.0 else n

    q_refs = (q0_ref, q1_ref, q2_ref)
    k_refs = (k0_ref, k1_ref, k2_ref)
    v_refs = (v0_ref, v1_ref, v2_ref)
    for g, dil in enumerate(DILATIONS):
        deinterleave(normed_rope(g * GROUP_WIDTH, qg_ref[...], HEAD_DIM ** -0.5), q_refs[g], dil)
        deinterleave(normed_rope(ATT_WIDTH + g * GROUP_WIDTH, kg_ref[...], 1.0), k_refs[g], dil)
        col0 = 2 * ATT_WIDTH + g * GROUP_WIDTH
        deinterleave(proj[:, col0:col0 + GROUP_WIDTH], v_refs[g], dil)
    s_ref[...] = jnp.dot(hb, w_ref[:, ATT_COLS:], preferred_element_type=jnp.float32)


def _in_proj(x, ln1_g, w_in_bf16, q_gain, k_gain, rope, bd):
    b, t, _ = x.shape
    nt = t // ROW_TILE
    row2 = lambda bi, i: (0, 0)
    grp_shapes, grp_specs = [], []
    for _ in range(3):
        for dil in DILATIONS:
            grp_shapes.append(jax.ShapeDtypeStruct((b, dil, t // dil, GROUP_WIDTH), jnp.bfloat16))
            grp_specs.append(pl.BlockSpec((None, dil, ROW_TILE // dil, GROUP_WIDTH), lambda bi, i: (bi, 0, i, 0)))
    tab_spec = pl.BlockSpec((ROW_TILE, GROUP_WIDTH), lambda bi, i: (i, 0))
    outs = pl.pallas_call(
        _in_proj_kernel,
        grid=(b, nt),
        in_specs=[
            pl.BlockSpec((None, ROW_TILE, D_MODEL), lambda bi, i: (bi, i, 0)),
            pl.BlockSpec((1, D_MODEL), row2),
            pl.BlockSpec((D_MODEL, IN_COLS), row2),
            pl.BlockSpec((1, GROUP_WIDTH), row2),
            pl.BlockSpec((1, GROUP_WIDTH), row2),
            tab_spec, tab_spec, tab_spec,
            pl.BlockSpec((GROUP_WIDTH, GROUP_WIDTH), row2),
        ],
        out_specs=grp_specs + [pl.BlockSpec((None, ROW_TILE, SHIFT_COLS), lambda bi, i: (bi, i, 0))],
        out_shape=grp_shapes + [jax.ShapeDtypeStruct((b, t, SHIFT_COLS), jnp.float32)],
        scratch_shapes=[pltpu.VMEM((GROUP_WIDTH // LANES, ROW_TILE, LANES), jnp.float32)],
        compiler_params=_cparams(("parallel", "parallel")),
        name="in_proj",
    )(x, ln1_g.reshape(1, D_MODEL), w_in_bf16, q_gain, k_gain, *rope, bd)
    q = outs[0:3]
    k = outs[3:6]
    v = outs[6:9]
    return q, k, v, outs[9]


def _band_attn_kernel(q_ref, kp_ref, kc_ref, kn_ref, vp_ref, vc_ref, vn_ref, o_ref, lse_ref, *, seq_len, q_chunk):
    i = pl.program_id(1)
    seqs = q_ref.shape[0]
    kext = [jnp.concatenate([kp_ref[sq], kc_ref[sq], kn_ref[sq]], axis=0) for sq in range(seqs)]
    vext = [jnp.concatenate([vp_ref[sq], vc_ref[sq], vn_ref[sq]], axis=0) for sq in range(seqs)]
    kw = ATT_Q_BLOCK + 2 * BAND_HALF
    lane_head = lax.broadcasted_iota(jnp.int32, (ATT_Q_BLOCK, GROUP_WIDTH), 1) // HEAD_DIM
    qi = lax.broadcasted_iota(jnp.int32, (ATT_Q_BLOCK, kw), 0)
    kc = lax.broadcasted_iota(jnp.int32, (ATT_Q_BLOCK, kw), 1)
    diff = kc - BAND_HALF - qi
    band = (diff <= BAND_HALF) & (diff >= -BAND_HALF)
    mine = [lane_head == h for h in range(ATT_SLOTS)]
    all_blocks = [(sq, a) for sq in range(seqs) for a in range(0, q_chunk, ATT_Q_BLOCK)]
    for b0 in range(0, len(all_blocks), ATT_BLOCKS_PER_PASS):
        blocks = all_blocks[b0:b0 + ATT_BLOCKS_PER_PASS]
        pairs = [(blk, h) for blk in blocks for h in range(ATT_SLOTS)]
        valid = {(sq, a): band & (i * q_chunk + (a - BAND_HALF) + kc >= 0) & (i * q_chunk + (a - BAND_HALF) + kc < seq_len)
                 for sq, a in blocks}
        qb = {(sq, a): q_ref[sq, a:a + ATT_Q_BLOCK, :] for sq, a in blocks}
        kb = {(sq, a): kext[sq][a:a + kw] for sq, a in blocks}
        vb = {(sq, a): vext[sq][a:a + kw] for sq, a in blocks}
        s = [lax.dot_general(jnp.where(mine[h], qb[a], jnp.zeros_like(qb[a])), kb[a], (((1,), (1,)), ((), ())),
                             preferred_element_type=jnp.float32) for a, h in pairs]
        s = [jnp.where(valid[a], s_, NEG_INF) for s_, (a, _) in zip(s, pairs)]
        m = [jnp.max(s_, -1, keepdims=True) for s_ in s]
        p = [jnp.exp(s_ - m_) for s_, m_ in zip(s, m)]
        den = [jnp.sum(p_, -1, keepdims=True) for p_ in p]
        pv = [jnp.dot(p_.astype(jnp.bfloat16), vb[a], preferred_element_type=jnp.float32) for p_, (a, _) in zip(p, pairs)]
        for j, (sq, a) in enumerate(blocks):
            acc = jnp.zeros((ATT_Q_BLOCK, GROUP_WIDTH), jnp.float32)
            lse = jnp.zeros((ATT_Q_BLOCK, GROUP_WIDTH), jnp.float32)
            for h in range(ATT_SLOTS):
                n = j * ATT_SLOTS + h
                acc = jnp.where(mine[h], pv[n] * (1.0 / den[n]), acc)
                lse = jnp.where(mine[h], m[n] + jnp.log(den[n]), lse)
            o_ref[sq, a:a + ATT_Q_BLOCK, :] = acc
            lse_ref[sq, a:a + ATT_Q_BLOCK, :] = lse


def _band_attn(q, k, v):
    b, dil, seq_len, _ = q.shape
    ns = b * dil
    q, k, v = (z.reshape(ns, seq_len, GROUP_WIDTH) for z in (q, k, v))
    q_chunk = min(seq_len, ATT_Q_CHUNK)
    seqs = max(1, min(ns, ATT_Q_CHUNK // seq_len))
    halo_per_chunk = q_chunk // BAND_HALF
    last_halo = seq_len // BAND_HALF - 1
    cur = pl.BlockSpec((seqs, q_chunk, GROUP_WIDTH), lambda s, i: (s, i, 0))
    prev = pl.BlockSpec((seqs, BAND_HALF, GROUP_WIDTH), lambda s, i: (s, jnp.maximum(i * halo_per_chunk - 1, 0), 0))
    nxt = pl.BlockSpec((seqs, BAND_HALF, GROUP_WIDTH),
                       lambda s, i: (s, jnp.minimum((i + 1) * halo_per_chunk, last_halo), 0))
    o, lse = pl.pallas_call(
        functools.partial(_band_attn_kernel, seq_len=seq_len, q_chunk=q_chunk),
        grid=(ns // seqs, seq_len // q_chunk),
        in_specs=[cur, prev, cur, nxt, prev, cur, nxt],
        out_specs=[cur, cur],
        out_shape=[jax.ShapeDtypeStruct((ns, seq_len, GROUP_WIDTH), jnp.float32)] * 2,
        compiler_params=_cparams(("parallel", "parallel")),
        name="band_attn",
    )(q, k, k, k, v, v, v)
    shape4 = (b, dil, seq_len, GROUP_WIDTH)
    return o.reshape(shape4), lse.reshape(shape4)


def _sigmoid(x):
    return 1.0 / (1.0 + jnp.exp(-x))


def _rwkv_token_terms(sp_ref, sc_ref, sn_ref, mup_ref, mun_ref, w0_ref, w2_ref, a0_ref, a2_ref, kk_ref, ka_ref, bd):
    i = pl.program_id(1)
    last = pl.num_programs(1) - 1
    p = sc_ref[...]
    rows = p.shape[0]
    row = lax.broadcasted_iota(jnp.int32, p.shape, 0)
    halo_prev = sp_ref[7:8, :] * (i > 0).astype(jnp.float32)
    halo_next = sn_ref[0:1, :] * (i < last).astype(jnp.float32)
    p_prev = jnp.where(row == 0, halo_prev, pltpu.roll(p, 1, 0))
    p_next = jnp.where(row == rows - 1, halo_next, pltpu.roll(p, rows - 1, 0))
    u = p + mup_ref[...] * (p_prev - p) + mun_ref[...] * (p_next - p)

    r = u[:, 0:RW_WIDTH]
    k = u[:, RW_WIDTH:2 * RW_WIDTH]
    v = u[:, 2 * RW_WIDTH:3 * RW_WIDTH]
    c0 = 3 * RW_WIDTH
    lw = jnp.tanh(u[:, c0:c0 + 128]).astype(jnp.bfloat16)
    la = u[:, c0 + 128:c0 + 256].astype(jnp.bfloat16)
    lg = _sigmoid(u[:, c0 + 256:c0 + 384]).astype(jnp.bfloat16)
    kk = k * kk_ref[...]
    kk = kk * lax.rsqrt(_head_sum(kk * kk, bd) + 1e-12)
    per_dir = []
    for d in range(N_DIR):
        z = w0_ref[d:d + 1, :] + jnp.dot(lw, w2_ref[d], preferred_element_type=jnp.float32)
        e = _sigmoid(z) * float(np.exp(-0.5))
        a = _sigmoid(a0_ref[d:d + 1, :] + jnp.dot(la, a2_ref[d], preferred_element_type=jnp.float32))
        kd = k * (1.0 + (a - 1.0) * ka_ref[...])
        per_dir.append((e, a, kd))
    return r, v, kk, lg, per_dir


def _rwkv_prep_kernel(sp_ref, sc_ref, sn_ref, mup_ref, mun_ref, w0_ref, w2_ref, a0_ref, a2_ref, g2_ref,
                      kk_ref, ka_ref, rk_ref, bd_ref,
                      lhs_ref, add_ref, bonus_ref, gate_ref, risk_ref,
                      r_s, v_s, nkk_s, e_s, kd_s, beta_s):
    bd = bd_ref[...]
    r, v, kk, lg, per_dir = _rwkv_token_terms(sp_ref, sc_ref, sn_ref, mup_ref, mun_ref, w0_ref, w2_ref, a0_ref,
                                              a2_ref, kk_ref, ka_ref, bd)
    rows = r.shape[0]
    r_s[...] = r
    v_s[...] = v
    nkk_s[...] = -kk
    gate_ref[...] = jnp.dot(lg, g2_ref[...], preferred_element_type=jnp.float32)
    bonus = jnp.zeros_like(v)
    for d, (e, a, kd) in enumerate(per_dir):
        e_s[d] = e
        kd_s[d] = kd
        beta_s[d] = kk * a
        bonus = bonus + _head_sum(r * kd * rk_ref[...], bd) * v
    bonus_ref[...] = bonus

    idx_t = lax.broadcasted_iota(jnp.int32, (CHUNK, CHUNK), 0)
    idx_s = lax.broadcasted_iota(jnp.int32, (CHUNK, CHUNK), 1)
    eye = (idx_t == idx_s).astype(jnp.float32)
    bf = lambda z: z.astype(jnp.bfloat16)
    mm = lambda a_, b_: jnp.dot(bf(a_), bf(b_), preferred_element_type=jnp.float32)
    mm_nt = lambda a_, b_: lax.dot_general(bf(a_), bf(b_), (((1,), (1,)), ((), ())), preferred_element_type=jnp.float32)
    mm_tn = lambda a_, b_: lax.dot_general(bf(a_), bf(b_), (((0,), (0,)), ((), ())), preferred_element_type=jnp.float32)

    def chunk_body(cp, carry):
        units = []
        for c, d in [(cp * CHUNKS_PER_ITER + j, d_) for j in range(CHUNKS_PER_ITER) for d_ in range(N_DIR)]:
            rows_c = pl.ds(pl.multiple_of(c * CHUNK, CHUNK), CHUNK)
            r_c, v_c, nkk_c = r_s[rows_c, :], v_s[rows_c, :], nkk_s[rows_c, :]
            before = (idx_s < idx_t) if d == 0 else (idx_s > idx_t)
            before_eq = (idx_s <= idx_t) if d == 0 else (idx_s >= idx_t)
            last = CHUNK - 1 if d == 0 else 0
            e_c = e_s[d, rows_c, :]
            cl = jnp.dot(before_eq.astype(jnp.float32), e_c, precision=HIGHEST, preferred_element_type=jnp.float32)
            w_inc = jnp.exp(-cl)
            w_exc = jnp.exp(e_c - cl)
            w_inv = jnp.exp(cl)
            w_tot = w_inc[last:last + 1, :]
            a_t = nkk_c * w_exc
            r_t = r_c * w_inc
            b_h = beta_s[d, rows_c, :] * w_inv
            k_h = kd_s[d, rows_c, :] * w_inv
            b_t = b_h * w_tot
            k_t = k_h * w_tot
            for h in range(RW_HEADS):
                hs = slice(h * HEAD_DIM, (h + 1) * HEAD_DIM)
                units.append(dict(chunk=c, dir=d, before=before, before_eq=before_eq, a_t=a_t[:, hs], r_t=r_t[:, hs], b_h=b_h[:, hs],
                                  k_h=k_h[:, hs], b_t=b_t[:, hs], k_t=k_t[:, hs], w_tot=w_tot[:, hs], v=v_c[:, hs]))
        each = lambda fn: [fn(un) for un in units]
        g = each(lambda un: mm_nt(jnp.concatenate([un['a_t'], un['r_t']], axis=0),
                                  jnp.concatenate([un['b_h'], un['k_h']], axis=0)))
        for un, g_u in zip(units, g):
            un['m_ab'] = jnp.where(un['before'], g_u[:CHUNK, :CHUNK], 0.0)
            un['m_ak'] = jnp.where(un['before'], g_u[:CHUNK, CHUNK:], 0.0)
            un['n_rb'] = jnp.where(un['before_eq'], g_u[CHUNK:, :CHUNK], 0.0)
            un['n_rk'] = jnp.where(un['before_eq'], g_u[CHUNK:, CHUNK:], 0.0)
        t_inv = each(lambda un: eye + un['m_ab'])
        pw = each(lambda un: un['m_ab'])
        risk = carry
        for _ in range(CHUNK.bit_length() - 2):
            pw = [mm(p_, p_) for p_ in pw]
            t_inv = [t_ + mm(t_, p_) for t_, p_ in zip(t_inv, pw)]
            for p_ in pw:
                risk = jnp.maximum(risk, jnp.abs(p_))
        for t_ in t_inv:
            risk = jnp.maximum(risk, jnp.abs(t_) * (1.0 / INVERSE_SCALE))
        mv = each(lambda un: mm(un['m_ak'], un['v']))
        tx = [mm(t_, jnp.concatenate([un['a_t'], mv_], axis=1)) for t_, un, mv_ in zip(t_inv, units, mv)]
        zero = jnp.zeros((CHUNK, HEAD_DIM), jnp.float32)
        z = [jnp.concatenate([tx_, jnp.concatenate([zero, un['v']], axis=1)], axis=0)
             for un, tx_ in zip(units, tx)]
        nx = [mm(jnp.concatenate([un['n_rb'], un['n_rk']], axis=1), z_) for un, z_ in zip(units, z)]
        bx = [mm_tn(jnp.concatenate([un['b_t'], un['k_t']], axis=0), z_) for un, z_ in zip(units, z)]
        lhs_parts, add_parts = [], []
        for un, nx_, bx_ in zip(units, nx, bx):
            ra = un['r_t'] + nx_[:, :HEAD_DIM]
            y0 = nx_[:, HEAD_DIM:]
            phi = eye * un['w_tot'] + bx_[:, :HEAD_DIM]
            psi = bx_[:, HEAD_DIM:]
            lhs_parts.append(jnp.concatenate([ra, phi], axis=0))
            add_parts.append(jnp.concatenate([y0, psi], axis=0))
        for g0 in range(0, len(units), RW_HEADS):
            c, d = units[g0]['chunk'], units[g0]['dir']
            lhs_ref[d, c] = jnp.concatenate(lhs_parts[g0:g0 + RW_HEADS], axis=1).astype(lhs_ref.dtype)
            add_ref[d, c] = jnp.concatenate(add_parts[g0:g0 + RW_HEADS], axis=1).astype(add_ref.dtype)
        return risk

    risk = lax.fori_loop(0, rows // (CHUNK * CHUNKS_PER_ITER), chunk_body, jnp.zeros((CHUNK, CHUNK), jnp.float32))
    risk_ref[...] = jnp.broadcast_to(jnp.max(jnp.max(risk, axis=0, keepdims=True), axis=1, keepdims=True), risk_ref.shape)


def _rwkv_prep(s, mu_prev, mu_next, w0, w2p, a0, a2p, g2, k_k, k_a, r_k, bd):
    b, t, _ = s.shape
    nt = t // ROW_TILE
    halo_per_tile = ROW_TILE // 8
    last_halo = t // 8 - 1
    row2 = lambda bi, i: (0, 0)
    row3 = lambda bi, i: (0, 0, 0)
    vec = lambda n: pl.BlockSpec((1, n), row2)
    chunks_per_tile = ROW_TILE // CHUNK
    cspec = pl.BlockSpec((N_DIR, None, chunks_per_tile, 2 * CHUNK, RW_WIDTH), lambda bi, i: (0, bi, i, 0, 0))
    nspec = pl.BlockSpec((None, ROW_TILE, RW_WIDTH), lambda bi, i: (bi, i, 0))
    cshape = jax.ShapeDtypeStruct((N_DIR, b, t // CHUNK, 2 * CHUNK, RW_WIDTH), jnp.float32)
    nshape = jax.ShapeDtypeStruct((b, t, RW_WIDTH), jnp.float32)
    tile_f32 = pltpu.VMEM((ROW_TILE, RW_WIDTH), jnp.float32)
    tile2_f32 = pltpu.VMEM((N_DIR, ROW_TILE, RW_WIDTH), jnp.float32)
    return pl.pallas_call(
        _rwkv_prep_kernel,
        grid=(b, nt),
        in_specs=[
            pl.BlockSpec((None, 8, SHIFT_COLS), lambda bi, i: (bi, jnp.maximum(i * halo_per_tile - 1, 0), 0)),
            pl.BlockSpec((None, ROW_TILE, SHIFT_COLS), lambda bi, i: (bi, i, 0)),
            pl.BlockSpec((None, 8, SHIFT_COLS), lambda bi, i: (bi, jnp.minimum((i + 1) * halo_per_tile, last_halo), 0)),
            vec(SHIFT_COLS), vec(SHIFT_COLS),
            pl.BlockSpec((N_DIR, RW_WIDTH), row2),
            pl.BlockSpec((N_DIR, 128, RW_WIDTH), row3),
            pl.BlockSpec((N_DIR, RW_WIDTH), row2),
            pl.BlockSpec((N_DIR, 128, RW_WIDTH), row3),
            pl.BlockSpec((GATE_LORA, RW_WIDTH), row2),
            vec(RW_WIDTH), vec(RW_WIDTH), vec(RW_WIDTH),
            pl.BlockSpec((GROUP_WIDTH, GROUP_WIDTH), row2),
        ],
        out_specs=[cspec, cspec, nspec, nspec, pl.BlockSpec((None, None, 8, LANES), lambda bi, i: (bi, i, 0, 0))],
        out_shape=[jax.ShapeDtypeStruct(cshape.shape, jnp.bfloat16), jax.ShapeDtypeStruct(cshape.shape, jnp.bfloat16),
                   nshape, nshape,
                   jax.ShapeDtypeStruct((b, nt, 8, LANES), jnp.float32)],
        scratch_shapes=[tile_f32, tile_f32, tile_f32, tile2_f32, tile2_f32, tile2_f32],
        compiler_params=_cparams(("parallel", "parallel")),
        name="rwkv_prep",
    )(s, s, s, mu_prev, mu_next, w0, w2p, a0, a2p, g2, k_k, k_a, r_k, bd)


def _split_bf16(x):
    hi = x.astype(jnp.bfloat16)
    return hi, (x - hi.astype(jnp.float32)).astype(jnp.bfloat16)


def _rwkv_scan_kernel(lhs_f, add_f, lhs_b, add_b, y_f, y_b, state_ref):
    n_chunks = lhs_f.shape[0]

    @pl.when(pl.program_id(1) == 0)
    def _():
        state_ref[...] = jnp.zeros_like(state_ref)

    dot = lambda a_, b_: jnp.dot(a_, b_, preferred_element_type=jnp.float32)

    def chunk_body(cc, carry):
        chains = []
        for d, (lhs_ref, add_ref, c) in enumerate(((lhs_f, add_f, cc), (lhs_b, add_b, n_chunks - 1 - cc))):
            for h in range(RW_HEADS):
                hs = slice(h * HEAD_DIM, (h + 1) * HEAD_DIM)
                chains.append((d, h, c, hs, lhs_ref, add_ref))
        lhs = [lhs_ref[c, :, hs] for (_, _, c, hs, lhs_ref, _) in chains]
        ssplit = [_split_bf16(state_ref[d, h]) for (d, h, *_) in chains]
        res = [dot(l_, s_hi) + dot(l_, s_lo) for l_, (s_hi, s_lo) in zip(lhs, ssplit)]
        res = [r_ + add_ref[c, :, hs] for r_, (_, _, c, hs, _, add_ref) in zip(res, chains)]
        for r_, (d, h, *_) in zip(res, chains):
            state_ref[d, h] = r_[CHUNK:]
        for d, (y_ref, c) in enumerate(((y_f, cc), (y_b, n_chunks - 1 - cc))):
            rows = pl.ds(pl.multiple_of(c * CHUNK, CHUNK), CHUNK)
            y_ref[rows, :] = jnp.concatenate([r_[:CHUNK] for r_ in res[d * RW_HEADS:(d + 1) * RW_HEADS]], axis=1)
        return carry

    lax.fori_loop(0, n_chunks, chunk_body, 0)


def _rwkv_scan(lhs, add):
    _, b, nchunk, _, _ = lhs.shape
    per_step = ROW_TILE // CHUNK
    nc = nchunk // per_step
    fwd = pl.BlockSpec((None, None, per_step, 2 * CHUNK, RW_WIDTH), lambda bi, c: (0, bi, c, 0, 0))
    bwd = pl.BlockSpec((None, None, per_step, 2 * CHUNK, RW_WIDTH), lambda bi, c: (1, bi, nc - 1 - c, 0, 0))
    y_shape = jax.ShapeDtypeStruct((b, nchunk * CHUNK, RW_WIDTH), jnp.float32)
    return pl.pallas_call(
        _rwkv_scan_kernel,
        grid=(b, nc),
        in_specs=[fwd, fwd, bwd, bwd],
        out_specs=[pl.BlockSpec((None, ROW_TILE, RW_WIDTH), lambda bi, c: (bi, c, 0)),
                   pl.BlockSpec((None, ROW_TILE, RW_WIDTH), lambda bi, c: (bi, nc - 1 - c, 0))],
        out_shape=[y_shape, y_shape],
        scratch_shapes=[pltpu.VMEM((N_DIR, RW_HEADS, HEAD_DIM, HEAD_DIM), jnp.float32)],
        compiler_params=_cparams(("parallel", "arbitrary")),
        name="rwkv_scan",
    )(lhs, add, lhs, add)


def _rwkv_steps_prep_kernel(sp_ref, sc_ref, sn_ref, mup_ref, mun_ref, w0_ref, w2_ref, a0_ref, a2_ref,
                            kk_ref, ka_ref, bd_ref, nkkT_ref, rT_ref, wT_ref, bT_ref, kdT_ref, v_ref):
    r, v, kk, _, per_dir = _rwkv_token_terms(sp_ref, sc_ref, sn_ref, mup_ref, mun_ref, w0_ref, w2_ref, a0_ref,
                                             a2_ref, kk_ref, ka_ref, bd_ref[...])
    nkkT_ref[...] = (-kk).T
    rT_ref[...] = r.T
    v_ref[...] = v
    for d, (e, a, kd) in enumerate(per_dir):
        wT_ref[d] = jnp.exp(-e).T
        kdT_ref[d] = kd.T
        bT_ref[d] = (kk * a).T


def _rwkv_steps_kernel(nkkT_ref, rT_ref, wT_ref, bT_ref, kdT_ref, v_ref, y_ref, state_ref):
    d = pl.program_id(0)

    @pl.when(pl.program_id(2) == 0)
    def _():
        state_ref[...] = jnp.zeros_like(state_ref)

    col_refs = (nkkT_ref, wT_ref, bT_ref, kdT_ref, rT_ref)

    def run(reverse):
        def block(blk, states):
            base = (SCAN_CHUNK - SCAN_UNROLL * (blk + 1)) if reverse else SCAN_UNROLL * blk
            shift = (SCAN_CHUNK - base) % SCAN_CHUNK
            tiles = [pltpu.roll(ref[...], shift, 1) for ref in col_refs]
            rows = pl.ds(pl.multiple_of(base, SCAN_UNROLL), SCAN_UNROLL)
            v_rows = v_ref[rows, :]
            states = list(states)
            y_rows = [None] * SCAN_UNROLL
            order = range(SCAN_UNROLL - 1, -1, -1) if reverse else range(SCAN_UNROLL)
            for u in order:
                y_heads = []
                for h in range(RW_HEADS):
                    hs = slice(h * HEAD_DIM, (h + 1) * HEAD_DIM)
                    col = lambda z: jnp.broadcast_to(z[hs, u:u + 1], (HEAD_DIM, HEAD_DIM))
                    nkk, w, bb, kd, r = (col(z) for z in tiles)
                    st = states[h]
                    sa = jnp.sum(st * nkk, axis=0, keepdims=True)
                    st = st * w + bb * sa + kd * v_rows[u:u + 1, hs]
                    y_heads.append(jnp.sum(st * r, axis=0, keepdims=True))
                    states[h] = st
                y_rows[u] = jnp.concatenate(y_heads, axis=1)
            y_ref[rows, :] = jnp.concatenate(y_rows, axis=0)
            return tuple(states)

        init = tuple(state_ref[h] for h in range(RW_HEADS))
        final = lax.fori_loop(0, SCAN_CHUNK // SCAN_UNROLL, block, init)
        for h in range(RW_HEADS):
            state_ref[h] = final[h]

    @pl.when(d == 0)
    def _():
        run(False)

    @pl.when(d == 1)
    def _():
        run(True)


def _rwkv_steps(s, mu_prev, mu_next, w0, w2p, a0, a2p, k_k, k_a, bd):
    b, t, _ = s.shape
    nt = t // ROW_TILE
    halo_per_tile = ROW_TILE // 8
    last_halo = t // 8 - 1
    row2 = lambda bi, i: (0, 0)
    row3 = lambda bi, i: (0, 0, 0)
    vec = lambda n: pl.BlockSpec((1, n), row2)
    tspec = pl.BlockSpec((None, RW_WIDTH, ROW_TILE), lambda bi, i: (bi, 0, i))
    tspec2 = pl.BlockSpec((N_DIR, None, RW_WIDTH, ROW_TILE), lambda bi, i: (0, bi, 0, i))
    tshape = jax.ShapeDtypeStruct((b, RW_WIDTH, t), jnp.float32)
    tshape2 = jax.ShapeDtypeStruct((N_DIR, b, RW_WIDTH, t), jnp.float32)
    nkkT, rT, wT, bT, kdT, v = pl.pallas_call(
        _rwkv_steps_prep_kernel,
        grid=(b, nt),
        in_specs=[
            pl.BlockSpec((None, 8, SHIFT_COLS), lambda bi, i: (bi, jnp.maximum(i * halo_per_tile - 1, 0), 0)),
            pl.BlockSpec((None, ROW_TILE, SHIFT_COLS), lambda bi, i: (bi, i, 0)),
            pl.BlockSpec((None, 8, SHIFT_COLS), lambda bi, i: (bi, jnp.minimum((i + 1) * halo_per_tile, last_halo), 0)),
            vec(SHIFT_COLS), vec(SHIFT_COLS),
            pl.BlockSpec((N_DIR, RW_WIDTH), row2),
            pl.BlockSpec((N_DIR, 128, RW_WIDTH), row3),
            pl.BlockSpec((N_DIR, RW_WIDTH), row2),
            pl.BlockSpec((N_DIR, 128, RW_WIDTH), row3),
            vec(RW_WIDTH), vec(RW_WIDTH),
            pl.BlockSpec((GROUP_WIDTH, GROUP_WIDTH), row2),
        ],
        out_specs=[tspec, tspec, tspec2, tspec2, tspec2,
                   pl.BlockSpec((None, ROW_TILE, RW_WIDTH), lambda bi, i: (bi, i, 0))],
        out_shape=[tshape, tshape, tshape2, tshape2, tshape2, jax.ShapeDtypeStruct((b, t, RW_WIDTH), jnp.float32)],
        compiler_params=_cparams(("parallel", "parallel")),
        name="rwkv_steps_prep",
    )(s, s, s, mu_prev, mu_next, w0, w2p, a0, a2p, k_k, k_a, bd)
    nc = t // SCAN_CHUNK
    chunk = lambda d, c: c + d * (nc - 1 - 2 * c)
    shared = pl.BlockSpec((None, RW_WIDTH, SCAN_CHUNK), lambda d, bi, c: (bi, 0, chunk(d, c)))
    per_dir = pl.BlockSpec((None, None, RW_WIDTH, SCAN_CHUNK), lambda d, bi, c: (d, bi, 0, chunk(d, c)))
    y = pl.pallas_call(
        _rwkv_steps_kernel,
        grid=(N_DIR, b, nc),
        in_specs=[shared, shared, per_dir, per_dir, per_dir,
                  pl.BlockSpec((None, SCAN_CHUNK, RW_WIDTH), lambda d, bi, c: (bi, chunk(d, c), 0))],
        out_specs=pl.BlockSpec((None, None, SCAN_CHUNK, RW_WIDTH), lambda d, bi, c: (d, bi, chunk(d, c), 0)),
        out_shape=jax.ShapeDtypeStruct((N_DIR, b, t, RW_WIDTH), jnp.float32),
        scratch_shapes=[pltpu.VMEM((RW_HEADS, HEAD_DIM, HEAD_DIM), jnp.float32)],
        compiler_params=_cparams(("parallel", "parallel", "arbitrary")),
        name="rwkv_steps",
    )(nkkT, rT, wT, bT, kdT, v)
    return y[0], y[1]


def _out_proj_kernel(x_ref, o0_ref, l0_ref, o1_ref, l1_ref, o2_ref, l2_ref, yf_ref, yb_ref, bonus_ref, gate_ref,
                     lnxg_ref, lnxb_ref, wout_ref, ln2_ref, router_ref, bd_ref,
                     xn_ref, h2_ref, aff_ref, so1_ref, sl1_ref, so2_ref, sl2_ref):
    rows = x_ref.shape[0]

    def interleave(src_ref, dst_ref, dil):
        halves = range(GROUP_WIDTH // LANES)
        for r in range(dil):
            for hf in halves:
                dst_ref[hf, pl.ds(r, rows // dil, stride=dil), :] = src_ref[r, :, hf * LANES:(hf + 1) * LANES]
        return jnp.concatenate([dst_ref[hf] for hf in halves], axis=-1)

    o1 = interleave(o1_ref, so1_ref, DILATIONS[1])
    l1 = interleave(l1_ref, sl1_ref, DILATIONS[1])
    o2 = interleave(o2_ref, so2_ref, DILATIONS[2])
    l2 = interleave(l2_ref, sl2_ref, DILATIONS[2])
    l0 = l0_ref[0]
    m = jnp.maximum(jnp.maximum(l0, l1), l2)
    e0, e1, e2 = jnp.exp(l0 - m), jnp.exp(l1 - m), jnp.exp(l2 - m)
    att = (e0 * o0_ref[0] + e1 * o1 + e2 * o2) / (e0 + e1 + e2)

    bd = bd_ref[...]
    y = yf_ref[...] + yb_ref[...]
    mu = _head_sum(y, bd) * (1.0 / HEAD_DIM)
    dlt = y - mu
    var = _head_sum(dlt * dlt, bd) * (1.0 / HEAD_DIM)
    yn = dlt * lax.rsqrt(var + GN_EPS) * lnxg_ref[...] + lnxb_ref[...]
    rw = (yn + bonus_ref[...]) * gate_ref[...]

    mixed = (jnp.dot(att.astype(jnp.bfloat16), wout_ref[0:GROUP_WIDTH, :], preferred_element_type=jnp.float32)
             + jnp.dot(rw.astype(jnp.bfloat16), wout_ref[GROUP_WIDTH:, :], preferred_element_type=jnp.float32))
    xn = x_ref[...] + mixed
    xn_ref[...] = xn
    h2 = xn * lax.rsqrt(jnp.mean(xn * xn, -1, keepdims=True) + NORM_EPS) * ln2_ref[...]
    h2_ref[...] = h2.astype(h2_ref.dtype)
    logits = lax.dot_general(router_ref[...], h2, (((1,), (1,)), ((), ())),
                             precision=HIGHEST, preferred_element_type=jnp.float32)
    ex = jnp.exp(logits - jnp.max(logits, 0, keepdims=True))
    aff_ref[...] = ex / jnp.sum(ex, 0, keepdims=True)


def _out_proj(x, att_parts, y, bonus, gate, lnx_g, lnx_b, w_out_bf16, ln2_g, router_t, bd):
    b, t, _ = x.shape
    nt = t // ROW_TILE
    row2 = lambda bi, i: (0, 0)
    tok = lambda w: pl.BlockSpec((None, ROW_TILE, w), lambda bi, i: (bi, i, 0))
    grp = lambda dil: pl.BlockSpec((None, dil, ROW_TILE // dil, GROUP_WIDTH), lambda bi, i: (bi, 0, i, 0))
    vec = lambda n: pl.BlockSpec((1, n), row2)
    grp_specs, grp_args = [], []
    for (o, lse), dil in zip(att_parts, DILATIONS):
        grp_specs += [grp(dil), grp(dil)]
        grp_args += [o, lse]
    return pl.pallas_call(
        _out_proj_kernel,
        grid=(b, nt),
        in_specs=[tok(D_MODEL)] + grp_specs + [
            tok(RW_WIDTH), tok(RW_WIDTH), tok(RW_WIDTH), tok(RW_WIDTH), vec(RW_WIDTH), vec(RW_WIDTH),
            pl.BlockSpec((GROUP_WIDTH + RW_WIDTH, D_MODEL), row2),
            vec(D_MODEL),
            pl.BlockSpec((N_EXPERTS, D_MODEL), row2),
            pl.BlockSpec((GROUP_WIDTH, GROUP_WIDTH), row2),
        ],
        out_specs=[tok(D_MODEL), tok(D_MODEL),
                   pl.BlockSpec((N_EXPERTS, ROW_TILE), lambda bi, i: (0, bi * nt + i))],
        out_shape=[jax.ShapeDtypeStruct((b, t, D_MODEL), jnp.float32),
                   jax.ShapeDtypeStruct((b, t, D_MODEL), jnp.bfloat16),
                   jax.ShapeDtypeStruct((N_EXPERTS, b * t), jnp.float32)],
        scratch_shapes=[pltpu.VMEM((GROUP_WIDTH // LANES, ROW_TILE, LANES), jnp.float32)] * 4,
        compiler_params=_cparams(("parallel", "parallel")),
        name="out_proj",
    )(x, *grp_args, *y, bonus, gate, lnx_g, lnx_b, w_out_bf16, ln2_g, router_t, bd)


FF_CHUNK = 512


def _expert_ffn_kernel(x_ref, gate_ref, wg_ref, wu_ref, wd_ref, o_ref):
    x = x_ref[...]
    acc = jnp.zeros(o_ref.shape, jnp.float32)
    for f0 in range(0, EXPERT_FF, FF_CHUNK):
        hg = jnp.dot(x, wg_ref[:, f0:f0 + FF_CHUNK], preferred_element_type=jnp.float32)
        hu = jnp.dot(x, wu_ref[:, f0:f0 + FF_CHUNK], preferred_element_type=jnp.float32)
        hid = (hg * _sigmoid(hg) * hu).astype(jnp.bfloat16)
        acc = acc + jnp.dot(hid, wd_ref[f0:f0 + FF_CHUNK, :], preferred_element_type=jnp.float32)
    o_ref[...] = (acc * gate_ref[...]).astype(o_ref.dtype)


def _expert_ffn(xs, gate, wg, wu, wd):
    e, cap, _ = xs.shape
    tile = min(FFN_ROW_TILE, cap)
    tok = lambda w: pl.BlockSpec((None, tile, w), lambda ei, i: (ei, i, 0))
    wspec = lambda r, c: pl.BlockSpec((None, r, c), lambda ei, i: (ei, 0, 0))
    return pl.pallas_call(
        _expert_ffn_kernel,
        grid=(e, cap // tile),
        in_specs=[tok(D_MODEL), tok(1), wspec(D_MODEL, EXPERT_FF), wspec(D_MODEL, EXPERT_FF), wspec(EXPERT_FF, D_MODEL)],
        out_specs=tok(D_MODEL),
        out_shape=jax.ShapeDtypeStruct((e, cap, D_MODEL), jnp.bfloat16),
        compiler_params=_cparams(("parallel", "arbitrary")),
        name="expert_ffn",
    )(xs, gate, wg, wu, wd)


ONE_BITS_PLUS_1 = 0x3F800001
LIST_ROWS = 16


def _prefix_counts(mask_bf16, tri_ones, strict_lower):
    t = jnp.dot(mask_bf16, tri_ones, preferred_element_type=jnp.float32)
    offs = jnp.dot(strict_lower, t[:, LANES:].astype(jnp.bfloat16), preferred_element_type=jnp.float32)
    return t[:, :LANES] + offs, offs, t[:, LANES:]


def _route_kernel(aff_ref, pos_ref, offs_ref, idx_ref, gate_ref, *, cap):
    x = pltpu.bitcast(aff_ref[...], jnp.int32)
    nb = x.shape[0]

    def count_ge(thr):
        hit = (x >= thr).astype(jnp.int32)
        return jnp.sum(jnp.sum(hit, axis=0, keepdims=True), axis=1, keepdims=True)

    def bisect(_, bounds):
        lo, hi = bounds
        mid = lo + ((hi - lo) >> 1)
        ok = count_ge(mid) >= cap
        return jnp.where(ok, mid, lo), jnp.where(ok, hi, mid)

    lo0 = jnp.zeros((1, 1), jnp.int32)
    hi0 = jnp.full((1, 1), ONE_BITS_PLUS_1, jnp.int32)
    thr, _ = lax.fori_loop(0, 31, bisect, (lo0, hi0))
    above = x > thr
    tie = x == thr
    need = cap - count_ge(thr + 1)

    r_i = lax.broadcasted_iota(jnp.int32, (LANES, 2 * LANES), 0)
    c_i = lax.broadcasted_iota(jnp.int32, (LANES, 2 * LANES), 1)
    tri_ones = ((c_i >= LANES) | (r_i <= c_i)).astype(jnp.bfloat16)
    rr = lax.broadcasted_iota(jnp.int32, (nb, nb), 0)
    cc = lax.broadcasted_iota(jnp.int32, (nb, nb), 1)
    strict_lower = (cc < rr).astype(jnp.bfloat16)

    tie_f = tie.astype(jnp.float32)
    tie_incl, _, _ = _prefix_counts(tie_f.astype(jnp.bfloat16), tri_ones, strict_lower)
    sel = above | (tie & ((tie_incl - tie_f) < need.astype(jnp.float32)))
    sel_incl, offs, row_total = _prefix_counts(sel.astype(jnp.bfloat16), tri_ones, strict_lower)
    in_row = sel_incl - offs
    row_end = offs + row_total
    pos_ref[...] = jnp.where(sel, sel_incl.astype(jnp.int32) - 1, -1)
    offs_ref[...] = offs.astype(jnp.int32)

    aff_t = aff_ref[...].T
    g_hi = aff_t.astype(jnp.bfloat16)
    g_mid = (aff_t - g_hi.astype(jnp.float32)).astype(jnp.bfloat16)
    g_lo = (aff_t - g_hi.astype(jnp.float32) - g_mid.astype(jnp.float32)).astype(jnp.bfloat16)
    row_id = lax.broadcasted_iota(jnp.int32, (1, nb), 1)
    offs_t = offs.T[0:1, :].astype(jnp.int32)
    facts = jnp.concatenate([row_id >> 4, row_id & 15, offs_t >> 7, offs_t & (LANES - 1)], axis=0)
    facts = jnp.concatenate([facts.astype(jnp.float32), jnp.zeros((LIST_ROWS - 4, nb), jnp.float32)], axis=0)
    stack = jnp.concatenate([in_row.T.astype(jnp.bfloat16), facts.astype(jnp.bfloat16), g_hi, g_mid, g_lo], axis=0)
    sub = lax.broadcasted_iota(jnp.int32, (LANES, LANES), 0).astype(jnp.float32)

    def window_pair(wp, carry):
        ws = [wp * 2, wp * 2 + 1]
        slots = [(w * LANES + lax.broadcasted_iota(jnp.int32, (1, LANES), 1)).astype(jnp.float32) for w in ws]
        onehots = [((offs <= s_) & (s_ < row_end)).astype(jnp.bfloat16) for s_ in slots]
        gots = [jnp.dot(stack, oh, preferred_element_type=jnp.float32) for oh in onehots]
        for w, slot, got in zip(ws, slots, gots):
            emit(w, slot, got)
        return carry

    def emit(w, slot, got):
        counts = got[:LANES]
        fact = got[LANES:LANES + LIST_ROWS]
        gates = got[LANES + LIST_ROWS:LANES + LIST_ROWS + LANES] + got[2 * LANES + LIST_ROWS:3 * LANES + LIST_ROWS] \
            + got[3 * LANES + LIST_ROWS:]
        row = fact[0:1] * 16.0 + fact[1:2]
        local = slot - (fact[2:3] * float(LANES) + fact[3:4])
        lane = jnp.sum((counts <= local).astype(jnp.float32), axis=0, keepdims=True)
        idx = (row * float(LANES) + lane).astype(jnp.int32)
        gate = jnp.sum(jnp.where(sub == lane, gates, 0.0), axis=0, keepdims=True)
        idx_ref[w] = jnp.broadcast_to(idx, (8, LANES))
        gate_ref[w] = jnp.broadcast_to(gate, (8, LANES))

    lax.fori_loop(0, cap // (2 * LANES), window_pair, 0)


def _route(aff3, cap):
    e, nb, _ = aff3.shape
    n_win = cap // LANES
    spec = pl.BlockSpec((None, nb, LANES), lambda ei: (ei, 0, 0))
    lspec = pl.BlockSpec((None, n_win, 8, LANES), lambda ei: (ei, 0, 0, 0))
    shape = jax.ShapeDtypeStruct((e, nb, LANES), jnp.int32)
    pos, offs, idx, gate = pl.pallas_call(
        functools.partial(_route_kernel, cap=cap),
        grid=(e,),
        in_specs=[spec],
        out_specs=[spec, spec, lspec, lspec],
        out_shape=[shape, shape, jax.ShapeDtypeStruct((e, n_win, 8, LANES), jnp.int32),
                   jax.ShapeDtypeStruct((e, n_win, 8, LANES), jnp.float32)],
        compiler_params=_cparams(("parallel",)),
        name="route",
    )(aff3)
    return pos, offs, idx[:, :, 0, :].reshape(e, cap), gate[:, :, 0, :].reshape(e, cap)


COMBINE_TILE = 512
WINDOW = 64
GROUP = 16
MAX_WINDOWS = N_EXPERTS * (COMBINE_TILE // WINDOW + 1)
NO_SLOT = -(1 << 20)


def _combine_kernel(first_ref, count_ref, x_ref, pos_ref, ye_hbm, o_ref, stage, sems, win_e, win_s, n_win):
    i = pl.program_id(0)
    half = i % 2

    def fetch(h, j, e, win):
        return pltpu.make_async_copy(ye_hbm.at[e, pl.ds(win * WINDOW, WINDOW), :], stage.at[h, j], sems.at[h, j])

    def request_tile(tile, h):
        n_req = jnp.int32(0)
        for e in range(N_EXPERTS):
            first = first_ref[e, tile]

            def request_one(k, j, e=e, first=first):
                win_e[h, j] = e
                win_s[h, j] = (first + k) * WINDOW
                fetch(h, j, e, first + k).start()
                return j + 1

            n_req = lax.fori_loop(0, count_ref[e, tile], request_one, n_req)
        n_win[h] = n_req

    @pl.when(i == 0)
    def _():
        stage[...] = jnp.zeros_like(stage)
        request_tile(0, 0)

    @pl.when(i + 1 < pl.num_programs(0))
    def _():
        request_tile(i + 1, 1 - half)

    n_here = n_win[half]
    slot_iota = lax.broadcasted_iota(jnp.int32, (WINDOW, COMBINE_TILE), 0)

    def group_body(g, acc):
        onehots = []
        for u in range(GROUP):
            j = g * GROUP + u
            live = j < n_here
            pos_row = pos_ref[jnp.where(live, win_e[half, j], 0)]
            base = jnp.where(live, win_s[half, j], NO_SLOT)
            onehots.append((pos_row == base + slot_iota).astype(jnp.bfloat16))
        onehot = jnp.concatenate(onehots, axis=0)
        for u in range(GROUP):
            j = g * GROUP + u

            @pl.when(j < n_here)
            def _(j=j):
                fetch(half, j, 0, 0).wait()

        rows = stage[half, pl.ds(g * GROUP, GROUP)].reshape(GROUP * WINDOW, D_MODEL)
        return acc + lax.dot_general(onehot, rows, (((0,), (0,)), ((), ())), preferred_element_type=jnp.float32)

    o_ref[...] = lax.fori_loop(0, (n_here + GROUP - 1) // GROUP, group_body, x_ref[...])


def _combine(xn, pos, offs, ye, cap):
    n = xn.shape[0]
    nt = n // COMBINE_TILE
    rows_per_tile = COMBINE_TILE // LANES
    start = offs[:, ::rows_per_tile, 0]
    stop = jnp.concatenate([start[:, 1:], jnp.full((N_EXPERTS, 1), cap, jnp.int32)], axis=1)
    first = start // WINDOW
    count = jnp.where(stop > start, (stop - 1) // WINDOW - first + 1, 0)
    pos_t = jnp.transpose(pos.reshape(N_EXPERTS, nt, COMBINE_TILE), (1, 0, 2)).reshape(nt, N_EXPERTS, 1, COMBINE_TILE)
    tok = pl.BlockSpec((COMBINE_TILE, D_MODEL), lambda i, f, c: (i, 0))
    return pl.pallas_call(
        _combine_kernel,
        grid_spec=pltpu.PrefetchScalarGridSpec(
            num_scalar_prefetch=2,
            grid=(nt,),
            in_specs=[tok,
                      pl.BlockSpec((None, N_EXPERTS, 1, COMBINE_TILE), lambda i, f, c: (i, 0, 0, 0)),
                      pl.BlockSpec(memory_space=pl.ANY)],
            out_specs=tok,
            scratch_shapes=[pltpu.VMEM((2, MAX_WINDOWS, WINDOW, D_MODEL), jnp.bfloat16),
                            pltpu.SemaphoreType.DMA((2, MAX_WINDOWS)),
                            pltpu.SMEM((2, MAX_WINDOWS), jnp.int32),
                            pltpu.SMEM((2, MAX_WINDOWS), jnp.int32),
                            pltpu.SMEM((2,), jnp.int32)],
        ),
        out_shape=jax.ShapeDtypeStruct((n, D_MODEL), jnp.float32),
        compiler_params=_cparams(("arbitrary",)),
        name="combine",
    )(first, count, xn, pos_t, ye)


def _pad_lora(w):
    z = jnp.zeros((N_DIR, N_DIR * w.shape[1], w.shape[2]), w.dtype)
    for d in range(N_DIR):
        z = z.at[d, d * w.shape[1]:(d + 1) * w.shape[1]].set(w[d])
    return z.astype(jnp.bfloat16)


def _layer_params(l, ln1_g, ln2_g, w_in, mu_prev, mu_next, q_norm_g, k_norm_g, w0, w2, a0, a2, g2, k_k, k_a,
                  r_k, lnx_g, lnx_b, w_out, router, w_gate, w_up, w_down):
    bf = lambda z: z.astype(jnp.bfloat16)
    row = lambda z: z.reshape(1, -1)
    return dict(
        ln1_g=ln1_g[l], ln2_g=row(ln2_g[l]), w_in=bf(w_in[l]), mu_prev=row(mu_prev[l]), mu_next=row(mu_next[l]),
        q_gain=row(jnp.tile(q_norm_g[l], ATT_SLOTS)), k_gain=row(jnp.tile(k_norm_g[l], ATT_SLOTS)),
        w0=w0[l], w2=_pad_lora(w2[l]), a0=a0[l], a2=_pad_lora(a2[l]), g2=bf(g2[l]),
        k_k=row(k_k[l]), k_a=row(k_a[l]), r_k=row(r_k[l]), lnx_g=row(lnx_g[l]), lnx_b=row(lnx_b[l]),
        w_out=bf(w_out[l]), router_t=router[l].T, w_gate=bf(w_gate[l]), w_up=bf(w_up[l]), w_down=bf(w_down[l]),
    )


def _encoder_layer(x, p, rope, bd):
    b, t, _ = x.shape
    n = b * t
    q, k, v, s = _in_proj(x, p['ln1_g'], p['w_in'], p['q_gain'], p['k_gain'], rope, bd)
    att_parts = [_band_attn(q[g], k[g], v[g]) for g in range(N_DIL)]
    lhs, add, bonus, gate, risk = _rwkv_prep(
        s, p['mu_prev'], p['mu_next'], p['w0'], p['w2'], p['a0'], p['a2'], p['g2'], p['k_k'], p['k_a'], p['r_k'], bd)
    chunked_ok = jnp.max(risk) <= RISK_LIMIT
    y = lax.cond(chunked_ok,
                 lambda: tuple(_rwkv_scan(lhs, add)),
                 lambda: _rwkv_steps(s, p['mu_prev'], p['mu_next'], p['w0'], p['w2'], p['a0'], p['a2'], p['k_k'],
                                     p['k_a'], bd))
    xn, h2, aff_t = _out_proj(x, att_parts, y, bonus, gate, p['lnx_g'], p['lnx_b'], p['w_out'], p['ln2_g'],
                              p['router_t'], bd)
    cap = max(1, (EC_CAPACITY_FACTOR * n) // N_EXPERTS)
    pos, offs, idx, gate_e = _route(aff_t.reshape(N_EXPERTS, n // LANES, LANES), cap)
    xs = h2.reshape(n, D_MODEL)[idx]
    ye = _expert_ffn(xs, gate_e[..., None], p['w_gate'], p['w_up'], p['w_down'])
    return _combine(xn.reshape(n, D_MODEL), pos, offs, ye, cap).reshape(b, t, D_MODEL)


def kernel(x_prompt, x_sample, ln1_g, ln2_g, w_in, mu_prev, mu_next, q_norm_g, k_norm_g, w0, w2, a0, a2, g2,
           k_k, k_a, r_k, lnx_g, lnx_b, w_out, router, w_gate, w_up, w_down):
    depth = w_in.shape[0]
    bd = _head_sum_matrix()
    rope_p = _rope_tables(x_prompt.shape[1])
    rope_s = _rope_tables(x_sample.shape[1])
    y_prompt, y_sample = x_prompt, x_sample
    for l in range(depth):
        p = _layer_params(l, ln1_g, ln2_g, w_in, mu_prev, mu_next, q_norm_g, k_norm_g, w0, w2, a0, a2, g2, k_k,
                          k_a, r_k, lnx_g, lnx_b, w_out, router, w_gate, w_up, w_down)
        y_prompt = _encoder_layer(y_prompt, p, rope_p, bd)
        y_sample = _encoder_layer(y_sample, p, rope_s, bd)
    return (y_prompt, y_sample)
```

```python
import functools

import jax
import jax.numpy as jnp
import numpy as np
from jax import lax
from jax.experimental import pallas as pl
from jax.experimental.pallas import tpu as pltpu

D_MODEL = 1024
RW_HEADS = 4
HEAD_DIM = 64
RW_WIDTH = RW_HEADS * HEAD_DIM
DECAY_LORA = 64
AAA_LORA = 64
GATE_LORA = 128
N_DIR = 2
GN_EPS = 64e-5
ATT_SLOTS = 4
DILATIONS = (1, 4, 16)
BAND_HALF = 64
N_DIL = len(DILATIONS)
GROUP_WIDTH = ATT_SLOTS * HEAD_DIM
ATT_WIDTH = N_DIL * GROUP_WIDTH
ATT_COLS = 3 * ATT_WIDTH
SHIFT_COLS = 3 * RW_WIDTH + N_DIR * DECAY_LORA + N_DIR * AAA_LORA + GATE_LORA
IN_COLS = ATT_COLS + SHIFT_COLS
ROPE_THETA = 500000.0
ROPE_DIMS = HEAD_DIM // 4
ROPE_HALF = ROPE_DIMS // 2
N_EXPERTS = 16
EC_CAPACITY_FACTOR = 2
EXPERT_FF = 2048
NORM_EPS = 1e-6
NEG_INF = -1e30
LANES = 128

ROW_TILE = 512
ATT_Q_CHUNK = 512
ATT_Q_BLOCK = 128
ATT_BLOCKS_PER_PASS = 2
CHUNK = 64
CHUNKS_PER_ITER = 4
SCAN_BATCH = 4
INVERSE_SCALE = 16.0
RISK_LIMIT = 4.0
SCAN_CHUNK = 128
SCAN_UNROLL = 8
FFN_ROW_TILE = 512
VMEM_LIMIT = 56 * 1024 * 1024

HIGHEST = lax.Precision.HIGHEST


def _cparams(sem):
    return pltpu.CompilerParams(dimension_semantics=sem, vmem_limit_bytes=VMEM_LIMIT)


def _head_sum_matrix():
    h = np.arange(GROUP_WIDTH) // HEAD_DIM
    return jnp.asarray((h[:, None] == h[None, :]).astype(np.float32), dtype=jnp.bfloat16)


def _head_sum(x, bd):
    hi = x.astype(jnp.bfloat16)
    lo = (x - hi.astype(jnp.float32)).astype(jnp.bfloat16)
    return (jnp.dot(hi, bd, preferred_element_type=jnp.float32)
            + jnp.dot(lo, bd, preferred_element_type=jnp.float32))


def _rope_tables(t):
    inv = jnp.power(jnp.float32(ROPE_THETA), -jnp.arange(ROPE_HALF, dtype=jnp.float32) * 2.0 / ROPE_DIMS)
    ang = jnp.arange(t, dtype=jnp.float32)[:, None] * inv[None, :]
    cos, sin = jnp.cos(ang), jnp.sin(ang)
    zeros = jnp.zeros((t, HEAD_DIM - ROPE_DIMS), jnp.float32)
    zero8 = jnp.zeros((t, ROPE_HALF), jnp.float32)
    c = jnp.concatenate([cos, cos, zeros + 1.0], -1)
    s1 = jnp.concatenate([zero8, sin, zeros], -1)
    s2 = jnp.concatenate([-sin, zero8, zeros], -1)
    tile = lambda z: jnp.tile(z, (1, ATT_SLOTS))
    return tile(c), tile(s1), tile(s2)


def _in_proj_kernel(x_ref, g_ref, w_ref, qg_ref, kg_ref, c_ref, s1_ref, s2_ref, bd_ref,
                    q0_ref, q1_ref, q2_ref, k0_ref, k1_ref, k2_ref, v0_ref, v1_ref, v2_ref, s_ref,
                    tmp_ref):
    x = x_ref[...]
    h = x * lax.rsqrt(jnp.mean(x * x, -1, keepdims=True) + NORM_EPS) * g_ref[...]
    hb = h.astype(jnp.bfloat16)
    bd = bd_ref[...]
    c, s1, s2 = c_ref[...], s1_ref[...], s2_ref[...]
    rows = x.shape[0]

    def deinterleave(val, out_ref, dil):
        if dil == 1:
            out_ref[0] = val.astype(out_ref.dtype)
            return
        for hf in range(GROUP_WIDTH // LANES):
            tmp_ref[hf] = val[:, hf * LANES:(hf + 1) * LANES]
        for r in range(dil):
            for hf in range(GROUP_WIDTH // LANES):
                out_ref[r, :, hf * LANES:(hf + 1) * LANES] = (
                    tmp_ref[hf, pl.ds(r, rows // dil, stride=dil), :].astype(out_ref.dtype))

    proj = jnp.dot(hb, w_ref[:, :ATT_COLS], preferred_element_type=jnp.float32)

    def normed_rope(col0, gain, scale):
        p = proj[:, col0:col0 + GROUP_WIDTH]
        ms = _head_sum(p * p, bd) * (1.0 / HEAD_DIM)
        n = p * lax.rsqrt(ms + NORM_EPS) * gain
        n = n * c + pltpu.roll(n, ROPE_HALF, 1) * s1 + pltpu.roll(n, GROUP_WIDTH - ROPE_HALF, 1) * s2
        return n * scale if scale != 1.0 else n

    q_refs = (q0_ref, q1_ref, q2_ref)
    k_refs = (k0_ref, k1_ref, k2_ref)
    v_refs = (v0_ref, v1_ref, v2_ref)
    for g, dil in enumerate(DILATIONS):
        deinterleave(normed_rope(g * GROUP_WIDTH, qg_ref[...], HEAD_DIM ** -0.5), q_refs[g], dil)
        deinterleave(normed_rope(ATT_WIDTH + g * GROUP_WIDTH, kg_ref[...], 1.0), k_refs[g], dil)
        col0 = 2 * ATT_WIDTH + g * GROUP_WIDTH
        deinterleave(proj[:, col0:col0 + GROUP_WIDTH], v_refs[g], dil)
    s_ref[...] = jnp.dot(hb, w_ref[:, ATT_COLS:], preferred_element_type=jnp.float32)


def _in_proj(x, ln1_g, w_in_bf16, q_gain, k_gain, rope, bd):
    b, t, _ = x.shape
    nt = t // ROW_TILE
    row2 = lambda bi, i: (0, 0)
    grp_shapes, grp_specs = [], []
    for _ in range(3):
        for dil in DILATIONS:
            grp_shapes.append(jax.ShapeDtypeStruct((b, dil, t // dil, GROUP_WIDTH), jnp.bfloat16))
            grp_specs.append(pl.BlockSpec((None, dil, ROW_TILE // dil, GROUP_WIDTH), lambda bi, i: (bi, 0, i, 0)))
    tab_spec = pl.BlockSpec((ROW_TILE, GROUP_WIDTH), lambda bi, i: (i, 0))
    outs = pl.pallas_call(
        _in_proj_kernel,
        grid=(b, nt),
        in_specs=[
            pl.BlockSpec((None, ROW_TILE, D_MODEL), lambda bi, i: (bi, i, 0)),
            pl.BlockSpec((1, D_MODEL), row2),
            pl.BlockSpec((D_MODEL, IN_COLS), row2),
            pl.BlockSpec((1, GROUP_WIDTH), row2),
            pl.BlockSpec((1, GROUP_WIDTH), row2),
            tab_spec, tab_spec, tab_spec,
            pl.BlockSpec((GROUP_WIDTH, GROUP_WIDTH), row2),
        ],
        out_specs=grp_specs + [pl.BlockSpec((None, ROW_TILE, SHIFT_COLS), lambda bi, i: (bi, i, 0))],
        out_shape=grp_shapes + [jax.ShapeDtypeStruct((b, t, SHIFT_COLS), jnp.float32)],
        scratch_shapes=[pltpu.VMEM((GROUP_WIDTH // LANES, ROW_TILE, LANES), jnp.float32)],
        compiler_params=_cparams(("parallel", "parallel")),
        name="in_proj",
    )(x, ln1_g.reshape(1, D_MODEL), w_in_bf16, q_gain, k_gain, *rope, bd)
    q = outs[0:3]
    k = outs[3:6]
    v = outs[6:9]
    return q, k, v, outs[9]


def _band_attn_kernel(q_ref, kp_ref, kc_ref, kn_ref, vp_ref, vc_ref, vn_ref, o_ref, lse_ref, *, seq_len, q_chunk):
    i = pl.program_id(1)
    seqs = q_ref.shape[0]
    kext = [jnp.concatenate([kp_ref[sq], kc_ref[sq], kn_ref[sq]], axis=0) for sq in range(seqs)]
    vext = [jnp.concatenate([vp_ref[sq], vc_ref[sq], vn_ref[sq]], axis=0) for sq in range(seqs)]
    kw = ATT_Q_BLOCK + 2 * BAND_HALF
    lane_head = lax.broadcasted_iota(jnp.int32, (ATT_Q_BLOCK, GROUP_WIDTH), 1) // HEAD_DIM
    qi = lax.broadcasted_iota(jnp.int32, (ATT_Q_BLOCK, kw), 0)
    kc = lax.broadcasted_iota(jnp.int32, (ATT_Q_BLOCK, kw), 1)
    diff = kc - BAND_HALF - qi
    band = (diff <= BAND_HALF) & (diff >= -BAND_HALF)
    mine = [lane_head == h for h in range(ATT_SLOTS)]
    all_blocks = [(sq, a) for sq in range(seqs) for a in range(0, q_chunk, ATT_Q_BLOCK)]
    for b0 in range(0, len(all_blocks), ATT_BLOCKS_PER_PASS):
        blocks = all_blocks[b0:b0 + ATT_BLOCKS_PER_PASS]
        pairs = [(blk, h) for blk in blocks for h in range(ATT_SLOTS)]
        valid = {(sq, a): band & (i * q_chunk + (a - BAND_HALF) + kc >= 0) & (i * q_chunk + (a - BAND_HALF) + kc < seq_len)
                 for sq, a in blocks}
        qb = {(sq, a): q_ref[sq, a:a + ATT_Q_BLOCK, :] for sq, a in blocks}
        kb = {(sq, a): kext[sq][a:a + kw] for sq, a in blocks}
        vb = {(sq, a): vext[sq][a:a + kw] for sq, a in blocks}
        s = [lax.dot_general(jnp.where(mine[h], qb[a], jnp.zeros_like(qb[a])), kb[a], (((1,), (1,)), ((), ())),
                             preferred_element_type=jnp.float32) for a, h in pairs]
        s = [jnp.where(valid[a], s_, NEG_INF) for s_, (a, _) in zip(s, pairs)]
        m = [jnp.max(s_, -1, keepdims=True) for s_ in s]
        p = [jnp.exp(s_ - m_) for s_, m_ in zip(s, m)]
        den = [jnp.sum(p_, -1, keepdims=True) for p_ in p]
        pv = [jnp.dot(p_.astype(jnp.bfloat16), vb[a], preferred_element_type=jnp.float32) for p_, (a, _) in zip(p, pairs)]
        for j, (sq, a) in enumerate(blocks):
            acc = jnp.zeros((ATT_Q_BLOCK, GROUP_WIDTH), jnp.float32)
            lse = jnp.zeros((ATT_Q_BLOCK, GROUP_WIDTH), jnp.float32)
            for h in range(ATT_SLOTS):
                n = j * ATT_SLOTS + h
                acc = jnp.where(mine[h], pv[n] * (1.0 / den[n]), acc)
                lse = jnp.where(mine[h], m[n] + jnp.log(den[n]), lse)
            o_ref[sq, a:a + ATT_Q_BLOCK, :] = acc
            lse_ref[sq, a:a + ATT_Q_BLOCK, :] = lse


def _band_attn(q, k, v):
    b, dil, seq_len, _ = q.shape
    ns = b * dil
    q, k, v = (z.reshape(ns, seq_len, GROUP_WIDTH) for z in (q, k, v))
    q_chunk = min(seq_len, ATT_Q_CHUNK)
    seqs = max(1, min(ns, ATT_Q_CHUNK // seq_len))
    halo_per_chunk = q_chunk // BAND_HALF
    last_halo = seq_len // BAND_HALF - 1
    cur = pl.BlockSpec((seqs, q_chunk, GROUP_WIDTH), lambda s, i: (s, i, 0))
    prev = pl.BlockSpec((seqs, BAND_HALF, GROUP_WIDTH), lambda s, i: (s, jnp.maximum(i * halo_per_chunk - 1, 0), 0))
    nxt = pl.BlockSpec((seqs, BAND_HALF, GROUP_WIDTH),
                       lambda s, i: (s, jnp.minimum((i + 1) * halo_per_chunk, last_halo), 0))
    o, lse = pl.pallas_call(
        functools.partial(_band_attn_kernel, seq_len=seq_len, q_chunk=q_chunk),
        grid=(ns // seqs, seq_len // q_chunk),
        in_specs=[cur, prev, cur, nxt, prev, cur, nxt],
        out_specs=[cur, cur],
        out_shape=[jax.ShapeDtypeStruct((ns, seq_len, GROUP_WIDTH), jnp.float32)] * 2,
        compiler_params=_cparams(("parallel", "parallel")),
        name="band_attn",
    )(q, k, k, k, v, v, v)
    shape4 = (b, dil, seq_len, GROUP_WIDTH)
    return o.reshape(shape4), lse.reshape(shape4)


def _sigmoid(x):
    return 1.0 / (1.0 + jnp.exp(-x))


def _rwkv_token_terms(sp_ref, sc_ref, sn_ref, mup_ref, mun_ref, w0_ref, w2_ref, a0_ref, a2_ref, kk_ref, ka_ref, bd):
    i = pl.program_id(1)
    last = pl.num_programs(1) - 1
    p = sc_ref[...]
    rows = p.shape[0]
    row = lax.broadcasted_iota(jnp.int32, p.shape, 0)
    halo_prev = sp_ref[7:8, :] * (i > 0).astype(jnp.float32)
    halo_next = sn_ref[0:1, :] * (i < last).astype(jnp.float32)
    p_prev = jnp.where(row == 0, halo_prev, pltpu.roll(p, 1, 0))
    p_next = jnp.where(row == rows - 1, halo_next, pltpu.roll(p, rows - 1, 0))
    u = p + mup_ref[...] * (p_prev - p) + mun_ref[...] * (p_next - p)

    r = u[:, 0:RW_WIDTH]
    k = u[:, RW_WIDTH:2 * RW_WIDTH]
    v = u[:, 2 * RW_WIDTH:3 * RW_WIDTH]
    c0 = 3 * RW_WIDTH
    lw = jnp.tanh(u[:, c0:c0 + 128]).astype(jnp.bfloat16)
    la = u[:, c0 + 128:c0 + 256].astype(jnp.bfloat16)
    lg = _sigmoid(u[:, c0 + 256:c0 + 384]).astype(jnp.bfloat16)
    kk = k * kk_ref[...]
    kk = kk * lax.rsqrt(_head_sum(kk * kk, bd) + 1e-12)
    per_dir = []
    for d in range(N_DIR):
        z = w0_ref[d:d + 1, :] + jnp.dot(lw, w2_ref[d], preferred_element_type=jnp.float32)
        e = _sigmoid(z) * float(np.exp(-0.5))
        a = _sigmoid(a0_ref[d:d + 1, :] + jnp.dot(la, a2_ref[d], preferred_element_type=jnp.float32))
        kd = k * (1.0 + (a - 1.0) * ka_ref[...])
        per_dir.append((e, a, kd))
    return r, v, kk, lg, per_dir


def _rwkv_prep_kernel(sp_ref, sc_ref, sn_ref, mup_ref, mun_ref, w0_ref, w2_ref, a0_ref, a2_ref, g2_ref,
                      kk_ref, ka_ref, rk_ref, bd_ref,
                      lhs_ref, add_ref, bonus_ref, gate_ref, risk_ref,
                      r_s, v_s, nkk_s, e_s, kd_s, beta_s):
    bd = bd_ref[...]
    r, v, kk, lg, per_dir = _rwkv_token_terms(sp_ref, sc_ref, sn_ref, mup_ref, mun_ref, w0_ref, w2_ref, a0_ref,
                                              a2_ref, kk_ref, ka_ref, bd)
    rows = r.shape[0]
    r_s[...] = r
    v_s[...] = v
    nkk_s[...] = -kk
    gate_ref[...] = jnp.dot(lg, g2_ref[...], preferred_element_type=jnp.float32)
    bonus = jnp.zeros_like(v)
    for d, (e, a, kd) in enumerate(per_dir):
        e_s[d] = e
        kd_s[d] = kd
        beta_s[d] = kk * a
        bonus = bonus + _head_sum(r * kd * rk_ref[...], bd) * v
    bonus_ref[...] = bonus

    idx_t = lax.broadcasted_iota(jnp.int32, (CHUNK, CHUNK), 0)
    idx_s = lax.broadcasted_iota(jnp.int32, (CHUNK, CHUNK), 1)
    eye = (idx_t == idx_s).astype(jnp.float32)
    bf = lambda z: z.astype(jnp.bfloat16)
    mm = lambda a_, b_: jnp.dot(bf(a_), bf(b_), preferred_element_type=jnp.float32)
    mm_nt = lambda a_, b_: lax.dot_general(bf(a_), bf(b_), (((1,), (1,)), ((), ())), preferred_element_type=jnp.float32)
    mm_tn = lambda a_, b_: lax.dot_general(bf(a_), bf(b_), (((0,), (0,)), ((), ())), preferred_element_type=jnp.float32)

    def chunk_body(cp, carry):
        units = []
        for c, d in [(cp * CHUNKS_PER_ITER + j, d_) for j in range(CHUNKS_PER_ITER) for d_ in range(N_DIR)]:
            rows_c = pl.ds(pl.multiple_of(c * CHUNK, CHUNK), CHUNK)
            r_c, v_c, nkk_c = r_s[rows_c, :], v_s[rows_c, :], nkk_s[rows_c, :]
            before = (idx_s < idx_t) if d == 0 else (idx_s > idx_t)
            before_eq = (idx_s <= idx_t) if d == 0 else (idx_s >= idx_t)
            last = CHUNK - 1 if d == 0 else 0
            e_c = e_s[d, rows_c, :]
            cl = jnp.dot(before_eq.astype(jnp.float32), e_c, precision=HIGHEST, preferred_element_type=jnp.float32)
            w_inc = jnp.exp(-cl)
            w_exc = jnp.exp(e_c - cl)
            w_inv = jnp.exp(cl)
            w_tot = w_inc[last:last + 1, :]
            a_t = nkk_c * w_exc
            r_t = r_c * w_inc
            b_h = beta_s[d, rows_c, :] * w_inv
            k_h = kd_s[d, rows_c, :] * w_inv
            b_t = b_h * w_tot
            k_t = k_h * w_tot
            for h in range(RW_HEADS):
                hs = slice(h * HEAD_DIM, (h + 1) * HEAD_DIM)
                units.append(dict(chunk=c, dir=d, before=before, before_eq=before_eq, a_t=a_t[:, hs], r_t=r_t[:, hs], b_h=b_h[:, hs],
                                  k_h=k_h[:, hs], b_t=b_t[:, hs], k_t=k_t[:, hs], w_tot=w_tot[:, hs], v=v_c[:, hs]))
        each = lambda fn: [fn(un) for un in units]
        g = each(lambda un: mm_nt(jnp.concatenate([un['a_t'], un['r_t']], axis=0),
                                  jnp.concatenate([un['b_h'], un['k_h']], axis=0)))
        for un, g_u in zip(units, g):
            un['m_ab'] = jnp.where(un['before'], g_u[:CHUNK, :CHUNK], 0.0)
            un['m_ak'] = jnp.where(un['before'], g_u[:CHUNK, CHUNK:], 0.0)
            un['n_rb'] = jnp.where(un['before_eq'], g_u[CHUNK:, :CHUNK], 0.0)
            un['n_rk'] = jnp.where(un['before_eq'], g_u[CHUNK:, CHUNK:], 0.0)
        t_inv = each(lambda un: eye + un['m_ab'])
        pw = each(lambda un: un['m_ab'])
        risk = carry
        for _ in range(CHUNK.bit_length() - 2):
            pw = [mm(p_, p_) for p_ in pw]
            t_inv = [t_ + mm(t_, p_) for t_, p_ in zip(t_inv, pw)]
            for p_ in pw:
                risk = jnp.maximum(risk, jnp.abs(p_))
        for t_ in t_inv:
            risk = jnp.maximum(risk, jnp.abs(t_) * (1.0 / INVERSE_SCALE))
        mv = each(lambda un: mm(un['m_ak'], un['v']))
        tx = [mm(t_, jnp.concatenate([un['a_t'], mv_], axis=1)) for t_, un, mv_ in zip(t_inv, units, mv)]
        zero = jnp.zeros((CHUNK, HEAD_DIM), jnp.float32)
        z = [jnp.concatenate([tx_, jnp.concatenate([zero, un['v']], axis=1)], axis=0)
             for un, tx_ in zip(units, tx)]
        nx = [mm(jnp.concatenate([un['n_rb'], un['n_rk']], axis=1), z_) for un, z_ in zip(units, z)]
        bx = [mm_tn(jnp.concatenate([un['b_t'], un['k_t']], axis=0), z_) for un, z_ in zip(units, z)]
        lhs_parts, add_parts = [], []
        for un, nx_, bx_ in zip(units, nx, bx):
            ra = un['r_t'] + nx_[:, :HEAD_DIM]
            y0 = nx_[:, HEAD_DIM:]
            phi = eye * un['w_tot'] + bx_[:, :HEAD_DIM]
            psi = bx_[:, HEAD_DIM:]
            lhs_parts.append(jnp.concatenate([ra, phi], axis=0))
            add_parts.append(jnp.concatenate([y0, psi], axis=0))
        for g0 in range(0, len(units), RW_HEADS):
            c, d = units[g0]['chunk'], units[g0]['dir']
            lhs_ref[d, c] = jnp.concatenate(lhs_parts[g0:g0 + RW_HEADS], axis=1).astype(lhs_ref.dtype)
            add_ref[d, c] = jnp.concatenate(add_parts[g0:g0 + RW_HEADS], axis=1)
        return risk

    risk = lax.fori_loop(0, rows // (CHUNK * CHUNKS_PER_ITER), chunk_body, jnp.zeros((CHUNK, CHUNK), jnp.float32))
    risk_ref[...] = jnp.broadcast_to(jnp.max(jnp.max(risk, axis=0, keepdims=True), axis=1, keepdims=True), risk_ref.shape)


def _rwkv_prep(s, mu_prev, mu_next, w0, w2p, a0, a2p, g2, k_k, k_a, r_k, bd):
    b, t, _ = s.shape
    nt = t // ROW_TILE
    halo_per_tile = ROW_TILE // 8
    last_halo = t // 8 - 1
    row2 = lambda bi, i: (0, 0)
    row3 = lambda bi, i: (0, 0, 0)
    vec = lambda n: pl.BlockSpec((1, n), row2)
    chunks_per_tile = ROW_TILE // CHUNK
    cspec = pl.BlockSpec((N_DIR, None, chunks_per_tile, 2 * CHUNK, RW_WIDTH), lambda bi, i: (0, bi, i, 0, 0))
    nspec = pl.BlockSpec((None, ROW_TILE, RW_WIDTH), lambda bi, i: (bi, i, 0))
    cshape = jax.ShapeDtypeStruct((N_DIR, b, t // CHUNK, 2 * CHUNK, RW_WIDTH), jnp.float32)
    nshape = jax.ShapeDtypeStruct((b, t, RW_WIDTH), jnp.float32)
    tile_f32 = pltpu.VMEM((ROW_TILE, RW_WIDTH), jnp.float32)
    tile2_f32 = pltpu.VMEM((N_DIR, ROW_TILE, RW_WIDTH), jnp.float32)
    return pl.pallas_call(
        _rwkv_prep_kernel,
        grid=(b, nt),
        in_specs=[
            pl.BlockSpec((None, 8, SHIFT_COLS), lambda bi, i: (bi, jnp.maximum(i * halo_per_tile - 1, 0), 0)),
            pl.BlockSpec((None, ROW_TILE, SHIFT_COLS), lambda bi, i: (bi, i, 0)),
            pl.BlockSpec((None, 8, SHIFT_COLS), lambda bi, i: (bi, jnp.minimum((i + 1) * halo_per_tile, last_halo), 0)),
            vec(SHIFT_COLS), vec(SHIFT_COLS),
            pl.BlockSpec((N_DIR, RW_WIDTH), row2),
            pl.BlockSpec((N_DIR, 128, RW_WIDTH), row3),
            pl.BlockSpec((N_DIR, RW_WIDTH), row2),
            pl.BlockSpec((N_DIR, 128, RW_WIDTH), row3),
            pl.BlockSpec((GATE_LORA, RW_WIDTH), row2),
            vec(RW_WIDTH), vec(RW_WIDTH), vec(RW_WIDTH),
            pl.BlockSpec((GROUP_WIDTH, GROUP_WIDTH), row2),
        ],
        out_specs=[cspec, cspec, nspec, nspec, pl.BlockSpec((None, None, 8, LANES), lambda bi, i: (bi, i, 0, 0))],
        out_shape=[jax.ShapeDtypeStruct(cshape.shape, jnp.bfloat16), cshape, nshape, nshape,
                   jax.ShapeDtypeStruct((b, nt, 8, LANES), jnp.float32)],
        scratch_shapes=[tile_f32, tile_f32, tile_f32, tile2_f32, tile2_f32, tile2_f32],
        compiler_params=_cparams(("parallel", "parallel")),
        name="rwkv_prep",
    )(s, s, s, mu_prev, mu_next, w0, w2p, a0, a2p, g2, k_k, k_a, r_k, bd)


def _split_bf16(x):
    hi = x.astype(jnp.bfloat16)
    return hi, (x - hi.astype(jnp.float32)).astype(jnp.bfloat16)


def _rwkv_scan_kernel(lhs_f, add_f, lhs_b, add_b, y_f, y_b, state_ref):
    n_chunks = lhs_f.shape[1]

    @pl.when(pl.program_id(1) == 0)
    def _():
        state_ref[...] = jnp.zeros_like(state_ref)

    dot = lambda a_, b_: jnp.dot(a_, b_, preferred_element_type=jnp.float32)

    def chunk_body(cc, carry):
        groups = [(bb, d, lhs_ref, add_ref, y_ref, c)
                  for bb in range(lhs_f.shape[0])
                  for d, (lhs_ref, add_ref, y_ref, c) in enumerate(((lhs_f, add_f, y_f, cc),
                                                                    (lhs_b, add_b, y_b, n_chunks - 1 - cc)))]
        chains = [(bb, d, h, c, slice(h * HEAD_DIM, (h + 1) * HEAD_DIM), lhs_ref, add_ref)
                  for (bb, d, lhs_ref, add_ref, _, c) in groups for h in range(RW_HEADS)]
        lhs = [lhs_ref[bb, c, :, hs] for (bb, _, _, c, hs, lhs_ref, _) in chains]
        ssplit = [_split_bf16(state_ref[bb, d, h]) for (bb, d, h, *_) in chains]
        res = [dot(l_, s_hi) + dot(l_, s_lo) for l_, (s_hi, s_lo) in zip(lhs, ssplit)]
        res = [r_ + add_ref[bb, c, :, hs] for r_, (bb, _, _, c, hs, _, add_ref) in zip(res, chains)]
        for r_, (bb, d, h, *_) in zip(res, chains):
            state_ref[bb, d, h] = r_[CHUNK:]
        for g, (bb, _, _, _, y_ref, c) in enumerate(groups):
            rows = pl.ds(pl.multiple_of(c * CHUNK, CHUNK), CHUNK)
            y_ref[bb, rows, :] = jnp.concatenate([r_[:CHUNK] for r_ in res[g * RW_HEADS:(g + 1) * RW_HEADS]], axis=1)
        return carry

    lax.fori_loop(0, n_chunks, chunk_body, 0)


def _rwkv_scan(lhs, add):
    _, b, nchunk, _, _ = lhs.shape
    per_step = ROW_TILE // CHUNK
    nc = nchunk // per_step
    nb = min(b, SCAN_BATCH)
    fwd = pl.BlockSpec((None, nb, per_step, 2 * CHUNK, RW_WIDTH), lambda bi, c: (0, bi, c, 0, 0))
    bwd = pl.BlockSpec((None, nb, per_step, 2 * CHUNK, RW_WIDTH), lambda bi, c: (1, bi, nc - 1 - c, 0, 0))
    y_shape = jax.ShapeDtypeStruct((b, nchunk * CHUNK, RW_WIDTH), jnp.float32)
    return pl.pallas_call(
        _rwkv_scan_kernel,
        grid=(b // nb, nc),
        in_specs=[fwd, fwd, bwd, bwd],
        out_specs=[pl.BlockSpec((nb, ROW_TILE, RW_WIDTH), lambda bi, c: (bi, c, 0)),
                   pl.BlockSpec((nb, ROW_TILE, RW_WIDTH), lambda bi, c: (bi, nc - 1 - c, 0))],
        out_shape=[y_shape, y_shape],
        scratch_shapes=[pltpu.VMEM((nb, N_DIR, RW_HEADS, HEAD_DIM, HEAD_DIM), jnp.float32)],
        compiler_params=_cparams(("parallel", "arbitrary")),
        name="rwkv_scan",
    )(lhs, add, lhs, add)


def _rwkv_steps_prep_kernel(sp_ref, sc_ref, sn_ref, mup_ref, mun_ref, w0_ref, w2_ref, a0_ref, a2_ref,
                            kk_ref, ka_ref, bd_ref, nkkT_ref, rT_ref, wT_ref, bT_ref, kdT_ref, v_ref):
    r, v, kk, _, per_dir = _rwkv_token_terms(sp_ref, sc_ref, sn_ref, mup_ref, mun_ref, w0_ref, w2_ref, a0_ref,
                                             a2_ref, kk_ref, ka_ref, bd_ref[...])
    nkkT_ref[...] = (-kk).T
    rT_ref[...] = r.T
    v_ref[...] = v
    for d, (e, a, kd) in enumerate(per_dir):
        wT_ref[d] = jnp.exp(-e).T
        kdT_ref[d] = kd.T
        bT_ref[d] = (kk * a).T


def _rwkv_steps_kernel(nkkT_ref, rT_ref, wT_ref, bT_ref, kdT_ref, v_ref, y_ref, state_ref):
    d = pl.program_id(0)

    @pl.when(pl.program_id(2) == 0)
    def _():
        state_ref[...] = jnp.zeros_like(state_ref)

    col_refs = (nkkT_ref, wT_ref, bT_ref, kdT_ref, rT_ref)

    def run(reverse):
        def block(blk, states):
            base = (SCAN_CHUNK - SCAN_UNROLL * (blk + 1)) if reverse else SCAN_UNROLL * blk
            shift = (SCAN_CHUNK - base) % SCAN_CHUNK
            tiles = [pltpu.roll(ref[...], shift, 1) for ref in col_refs]
            rows = pl.ds(pl.multiple_of(base, SCAN_UNROLL), SCAN_UNROLL)
            v_rows = v_ref[rows, :]
            states = list(states)
            y_rows = [None] * SCAN_UNROLL
            order = range(SCAN_UNROLL - 1, -1, -1) if reverse else range(SCAN_UNROLL)
            for u in order:
                y_heads = []
                for h in range(RW_HEADS):
                    hs = slice(h * HEAD_DIM, (h + 1) * HEAD_DIM)
                    col = lambda z: jnp.broadcast_to(z[hs, u:u + 1], (HEAD_DIM, HEAD_DIM))
                    nkk, w, bb, kd, r = (col(z) for z in tiles)
                    st = states[h]
                    sa = jnp.sum(st * nkk, axis=0, keepdims=True)
                    st = st * w + bb * sa + kd * v_rows[u:u + 1, hs]
                    y_heads.append(jnp.sum(st * r, axis=0, keepdims=True))
                    states[h] = st
                y_rows[u] = jnp.concatenate(y_heads, axis=1)
            y_ref[rows, :] = jnp.concatenate(y_rows, axis=0)
            return tuple(states)

        init = tuple(state_ref[h] for h in range(RW_HEADS))
        final = lax.fori_loop(0, SCAN_CHUNK // SCAN_UNROLL, block, init)
        for h in range(RW_HEADS):
            state_ref[h] = final[h]

    @pl.when(d == 0)
    def _():
        run(False)

    @pl.when(d == 1)
    def _():
        run(True)


def _rwkv_steps(s, mu_prev, mu_next, w0, w2p, a0, a2p, k_k, k_a, bd):
    b, t, _ = s.shape
    nt = t // ROW_TILE
    halo_per_tile = ROW_TILE // 8
    last_halo = t // 8 - 1
    row2 = lambda bi, i: (0, 0)
    row3 = lambda bi, i: (0, 0, 0)
    vec = lambda n: pl.BlockSpec((1, n), row2)
    tspec = pl.BlockSpec((None, RW_WIDTH, ROW_TILE), lambda bi, i: (bi, 0, i))
    tspec2 = pl.BlockSpec((N_DIR, None, RW_WIDTH, ROW_TILE), lambda bi, i: (0, bi, 0, i))
    tshape = jax.ShapeDtypeStruct((b, RW_WIDTH, t), jnp.float32)
    tshape2 = jax.ShapeDtypeStruct((N_DIR, b, RW_WIDTH, t), jnp.float32)
    nkkT, rT, wT, bT, kdT, v = pl.pallas_call(
        _rwkv_steps_prep_kernel,
        grid=(b, nt),
        in_specs=[
            pl.BlockSpec((None, 8, SHIFT_COLS), lambda bi, i: (bi, jnp.maximum(i * halo_per_tile - 1, 0), 0)),
            pl.BlockSpec((None, ROW_TILE, SHIFT_COLS), lambda bi, i: (bi, i, 0)),
            pl.BlockSpec((None, 8, SHIFT_COLS), lambda bi, i: (bi, jnp.minimum((i + 1) * halo_per_tile, last_halo), 0)),
            vec(SHIFT_COLS), vec(SHIFT_COLS),
            pl.BlockSpec((N_DIR, RW_WIDTH), row2),
            pl.BlockSpec((N_DIR, 128, RW_WIDTH), row3),
            pl.BlockSpec((N_DIR, RW_WIDTH), row2),
            pl.BlockSpec((N_DIR, 128, RW_WIDTH), row3),
            vec(RW_WIDTH), vec(RW_WIDTH),
            pl.BlockSpec((GROUP_WIDTH, GROUP_WIDTH), row2),
        ],
        out_specs=[tspec, tspec, tspec2, tspec2, tspec2,
                   pl.BlockSpec((None, ROW_TILE, RW_WIDTH), lambda bi, i: (bi, i, 0))],
        out_shape=[tshape, tshape, tshape2, tshape2, tshape2, jax.ShapeDtypeStruct((b, t, RW_WIDTH), jnp.float32)],
        compiler_params=_cparams(("parallel", "parallel")),
        name="rwkv_steps_prep",
    )(s, s, s, mu_prev, mu_next, w0, w2p, a0, a2p, k_k, k_a, bd)
    nc = t // SCAN_CHUNK
    chunk = lambda d, c: c + d * (nc - 1 - 2 * c)
    shared = pl.BlockSpec((None, RW_WIDTH, SCAN_CHUNK), lambda d, bi, c: (bi, 0, chunk(d, c)))
    per_dir = pl.BlockSpec((None, None, RW_WIDTH, SCAN_CHUNK), lambda d, bi, c: (d, bi, 0, chunk(d, c)))
    y = pl.pallas_call(
        _rwkv_steps_kernel,
        grid=(N_DIR, b, nc),
        in_specs=[shared, shared, per_dir, per_dir, per_dir,
                  pl.BlockSpec((None, SCAN_CHUNK, RW_WIDTH), lambda d, bi, c: (bi, chunk(d, c), 0))],
        out_specs=pl.BlockSpec((None, None, SCAN_CHUNK, RW_WIDTH), lambda d, bi, c: (d, bi, chunk(d, c), 0)),
        out_shape=jax.ShapeDtypeStruct((N_DIR, b, t, RW_WIDTH), jnp.float32),
        scratch_shapes=[pltpu.VMEM((RW_HEADS, HEAD_DIM, HEAD_DIM), jnp.float32)],
        compiler_params=_cparams(("parallel", "parallel", "arbitrary")),
        name="rwkv_steps",
    )(nkkT, rT, wT, bT, kdT, v)
    return y[0], y[1]


def _out_proj_kernel(x_ref, o0_ref, l0_ref, o1_ref, l1_ref, o2_ref, l2_ref, yf_ref, yb_ref, bonus_ref, gate_ref,
                     lnxg_ref, lnxb_ref, wout_ref, ln2_ref, router_ref, bd_ref,
                     xn_ref, h2_ref, aff_ref, so1_ref, sl1_ref, so2_ref, sl2_ref):
    rows = x_ref.shape[0]

    def interleave(src_ref, dst_ref, dil):
        halves = range(GROUP_WIDTH // LANES)
        for r in range(dil):
            for hf in halves:
                dst_ref[hf, pl.ds(r, rows // dil, stride=dil), :] = src_ref[r, :, hf * LANES:(hf + 1) * LANES]
        return jnp.concatenate([dst_ref[hf] for hf in halves], axis=-1)

    o1 = interleave(o1_ref, so1_ref, DILATIONS[1])
    l1 = interleave(l1_ref, sl1_ref, DILATIONS[1])
    o2 = interleave(o2_ref, so2_ref, DILATIONS[2])
    l2 = interleave(l2_ref, sl2_ref, DILATIONS[2])
    l0 = l0_ref[0]
    m = jnp.maximum(jnp.maximum(l0, l1), l2)
    e0, e1, e2 = jnp.exp(l0 - m), jnp.exp(l1 - m), jnp.exp(l2 - m)
    att = (e0 * o0_ref[0] + e1 * o1 + e2 * o2) / (e0 + e1 + e2)

    bd = bd_ref[...]
    y = yf_ref[...] + yb_ref[...]
    mu = _head_sum(y, bd) * (1.0 / HEAD_DIM)
    dlt = y - mu
    var = _head_sum(dlt * dlt, bd) * (1.0 / HEAD_DIM)
    yn = dlt * lax.rsqrt(var + GN_EPS) * lnxg_ref[...] + lnxb_ref[...]
    rw = (yn + bonus_ref[...]) * gate_ref[...]

    mixed = (jnp.dot(att.astype(jnp.bfloat16), wout_ref[0:GROUP_WIDTH, :], preferred_element_type=jnp.float32)
             + jnp.dot(rw.astype(jnp.bfloat16), wout_ref[GROUP_WIDTH:, :], preferred_element_type=jnp.float32))
    xn = x_ref[...] + mixed
    xn_ref[...] = xn
    h2 = xn * lax.rsqrt(jnp.mean(xn * xn, -1, keepdims=True) + NORM_EPS) * ln2_ref[...]
    h2_ref[...] = h2.astype(h2_ref.dtype)
    logits = lax.dot_general(router_ref[...], h2, (((1,), (1,)), ((), ())),
                             precision=HIGHEST, preferred_element_type=jnp.float32)
    ex = jnp.exp(logits - jnp.max(logits, 0, keepdims=True))
    aff_ref[...] = ex / jnp.sum(ex, 0, keepdims=True)


def _out_proj(x, att_parts, y, bonus, gate, lnx_g, lnx_b, w_out_bf16, ln2_g, router_t, bd):
    b, t, _ = x.shape
    nt = t // ROW_TILE
    row2 = lambda bi, i: (0, 0)
    tok = lambda w: pl.BlockSpec((None, ROW_TILE, w), lambda bi, i: (bi, i, 0))
    grp = lambda dil: pl.BlockSpec((None, dil, ROW_TILE // dil, GROUP_WIDTH), lambda bi, i: (bi, 0, i, 0))
    vec = lambda n: pl.BlockSpec((1, n), row2)
    grp_specs, grp_args = [], []
    for (o, lse), dil in zip(att_parts, DILATIONS):
        grp_specs += [grp(dil), grp(dil)]
        grp_args += [o, lse]
    return pl.pallas_call(
        _out_proj_kernel,
        grid=(b, nt),
        in_specs=[tok(D_MODEL)] + grp_specs + [
            tok(RW_WIDTH), tok(RW_WIDTH), tok(RW_WIDTH), tok(RW_WIDTH), vec(RW_WIDTH), vec(RW_WIDTH),
            pl.BlockSpec((GROUP_WIDTH + RW_WIDTH, D_MODEL), row2),
            vec(D_MODEL),
            pl.BlockSpec((N_EXPERTS, D_MODEL), row2),
            pl.BlockSpec((GROUP_WIDTH, GROUP_WIDTH), row2),
        ],
        out_specs=[tok(D_MODEL), tok(D_MODEL),
                   pl.BlockSpec((N_EXPERTS, ROW_TILE), lambda bi, i: (0, bi * nt + i))],
        out_shape=[jax.ShapeDtypeStruct((b, t, D_MODEL), jnp.float32),
                   jax.ShapeDtypeStruct((b, t, D_MODEL), jnp.bfloat16),
                   jax.ShapeDtypeStruct((N_EXPERTS, b * t), jnp.float32)],
        scratch_shapes=[pltpu.VMEM((GROUP_WIDTH // LANES, ROW_TILE, LANES), jnp.float32)] * 4,
        compiler_params=_cparams(("parallel", "parallel")),
        name="out_proj",
    )(x, *grp_args, *y, bonus, gate, lnx_g, lnx_b, w_out_bf16, ln2_g, router_t, bd)


FF_CHUNK = 512


def _expert_ffn_kernel(x_ref, gate_ref, wg_ref, wu_ref, wd_ref, o_ref):
    x = x_ref[...]
    acc = jnp.zeros(o_ref.shape, jnp.float32)
    for f0 in range(0, EXPERT_FF, FF_CHUNK):
        hg = jnp.dot(x, wg_ref[:, f0:f0 + FF_CHUNK], preferred_element_type=jnp.float32)
        hu = jnp.dot(x, wu_ref[:, f0:f0 + FF_CHUNK], preferred_element_type=jnp.float32)
        hid = (hg * _sigmoid(hg) * hu).astype(jnp.bfloat16)
        acc = acc + jnp.dot(hid, wd_ref[f0:f0 + FF_CHUNK, :], preferred_element_type=jnp.float32)
    o_ref[...] = (acc * gate_ref[...]).astype(o_ref.dtype)


def _expert_ffn(xs, gate, wg, wu, wd):
    e, cap, _ = xs.shape
    tile = min(FFN_ROW_TILE, cap)
    tok = lambda w: pl.BlockSpec((None, tile, w), lambda ei, i: (ei, i, 0))
    wspec = lambda r, c: pl.BlockSpec((None, r, c), lambda ei, i: (ei, 0, 0))
    return pl.pallas_call(
        _expert_ffn_kernel,
        grid=(e, cap // tile),
        in_specs=[tok(D_MODEL), tok(1), wspec(D_MODEL, EXPERT_FF), wspec(D_MODEL, EXPERT_FF), wspec(EXPERT_FF, D_MODEL)],
        out_specs=tok(D_MODEL),
        out_shape=jax.ShapeDtypeStruct((e, cap, D_MODEL), jnp.bfloat16),
        compiler_params=_cparams(("parallel", "arbitrary")),
        name="expert_ffn",
    )(xs, gate, wg, wu, wd)


ONE_BITS_PLUS_1 = 0x3F800001
LIST_ROWS = 16


def _prefix_counts(mask_bf16, tri_ones, strict_lower):
    t = jnp.dot(mask_bf16, tri_ones, preferred_element_type=jnp.float32)
    offs = jnp.dot(strict_lower, t[:, LANES:].astype(jnp.bfloat16), preferred_element_type=jnp.float32)
    return t[:, :LANES] + offs, offs, t[:, LANES:]


def _route_kernel(aff_ref, pos_ref, offs_ref, idx_ref, gate_ref, *, cap):
    x = pltpu.bitcast(aff_ref[...], jnp.int32)
    nb = x.shape[0]

    def count_ge(thr):
        hit = (x >= thr).astype(jnp.int32)
        return jnp.sum(jnp.sum(hit, axis=0, keepdims=True), axis=1, keepdims=True)

    def bisect(_, bounds):
        lo, hi = bounds
        mid = lo + ((hi - lo) >> 1)
        ok = count_ge(mid) >= cap
        return jnp.where(ok, mid, lo), jnp.where(ok, hi, mid)

    lo0 = jnp.zeros((1, 1), jnp.int32)
    hi0 = jnp.full((1, 1), ONE_BITS_PLUS_1, jnp.int32)
    thr, _ = lax.fori_loop(0, 31, bisect, (lo0, hi0))
    above = x > thr
    tie = x == thr
    need = cap - count_ge(thr + 1)

    r_i = lax.broadcasted_iota(jnp.int32, (LANES, 2 * LANES), 0)
    c_i = lax.broadcasted_iota(jnp.int32, (LANES, 2 * LANES), 1)
    tri_ones = ((c_i >= LANES) | (r_i <= c_i)).astype(jnp.bfloat16)
    rr = lax.broadcasted_iota(jnp.int32, (nb, nb), 0)
    cc = lax.broadcasted_iota(jnp.int32, (nb, nb), 1)
    strict_lower = (cc < rr).astype(jnp.bfloat16)

    tie_f = tie.astype(jnp.float32)
    tie_incl, _, _ = _prefix_counts(tie_f.astype(jnp.bfloat16), tri_ones, strict_lower)
    sel = above | (tie & ((tie_incl - tie_f) < need.astype(jnp.float32)))
    sel_incl, offs, row_total = _prefix_counts(sel.astype(jnp.bfloat16), tri_ones, strict_lower)
    in_row = sel_incl - offs
    row_end = offs + row_total
    pos_ref[...] = jnp.where(sel, sel_incl.astype(jnp.int32) - 1, -1)
    offs_ref[...] = offs.astype(jnp.int32)

    aff_t = aff_ref[...].T
    g_hi = aff_t.astype(jnp.bfloat16)
    g_mid = (aff_t - g_hi.astype(jnp.float32)).astype(jnp.bfloat16)
    g_lo = (aff_t - g_hi.astype(jnp.float32) - g_mid.astype(jnp.float32)).astype(jnp.bfloat16)
    row_id = lax.broadcasted_iota(jnp.int32, (1, nb), 1)
    offs_t = offs.T[0:1, :].astype(jnp.int32)
    facts = jnp.concatenate([row_id >> 4, row_id & 15, offs_t >> 7, offs_t & (LANES - 1)], axis=0)
    facts = jnp.concatenate([facts.astype(jnp.float32), jnp.zeros((LIST_ROWS - 4, nb), jnp.float32)], axis=0)
    stack = jnp.concatenate([in_row.T.astype(jnp.bfloat16), facts.astype(jnp.bfloat16), g_hi, g_mid, g_lo], axis=0)
    sub = lax.broadcasted_iota(jnp.int32, (LANES, LANES), 0).astype(jnp.float32)

    def window_pair(wp, carry):
        ws = [wp * 2, wp * 2 + 1]
        slots = [(w * LANES + lax.broadcasted_iota(jnp.int32, (1, LANES), 1)).astype(jnp.float32) for w in ws]
        onehots = [((offs <= s_) & (s_ < row_end)).astype(jnp.bfloat16) for s_ in slots]
        gots = [jnp.dot(stack, oh, preferred_element_type=jnp.float32) for oh in onehots]
        for w, slot, got in zip(ws, slots, gots):
            emit(w, slot, got)
        return carry

    def emit(w, slot, got):
        counts = got[:LANES]
        fact = got[LANES:LANES + LIST_ROWS]
        gates = got[LANES + LIST_ROWS:LANES + LIST_ROWS + LANES] + got[2 * LANES + LIST_ROWS:3 * LANES + LIST_ROWS] \
            + got[3 * LANES + LIST_ROWS:]
        row = fact[0:1] * 16.0 + fact[1:2]
        local = slot - (fact[2:3] * float(LANES) + fact[3:4])
        lane = jnp.sum((counts <= local).astype(jnp.float32), axis=0, keepdims=True)
        idx = (row * float(LANES) + lane).astype(jnp.int32)
        gate = jnp.sum(jnp.where(sub == lane, gates, 0.0), axis=0, keepdims=True)
        idx_ref[w] = jnp.broadcast_to(idx, (8, LANES))
        gate_ref[w] = jnp.broadcast_to(gate, (8, LANES))

    lax.fori_loop(0, cap // (2 * LANES), window_pair, 0)


def _route(aff3, cap):
    e, nb, _ = aff3.shape
    n_win = cap // LANES
    spec = pl.BlockSpec((None, nb, LANES), lambda ei: (ei, 0, 0))
    lspec = pl.BlockSpec((None, n_win, 8, LANES), lambda ei: (ei, 0, 0, 0))
    shape = jax.ShapeDtypeStruct((e, nb, LANES), jnp.int32)
    pos, offs, idx, gate = pl.pallas_call(
        functools.partial(_route_kernel, cap=cap),
        grid=(e,),
        in_specs=[spec],
        out_specs=[spec, spec, lspec, lspec],
        out_shape=[shape, shape, jax.ShapeDtypeStruct((e, n_win, 8, LANES), jnp.int32),
                   jax.ShapeDtypeStruct((e, n_win, 8, LANES), jnp.float32)],
        compiler_params=_cparams(("parallel",)),
        name="route",
    )(aff3)
    return pos, offs, idx[:, :, 0, :].reshape(e, cap), gate[:, :, 0, :].reshape(e, cap)


COMBINE_TILE = 512
WINDOW = 64
GROUP = 16
MAX_WINDOWS = N_EXPERTS * (COMBINE_TILE // WINDOW + 1)
NO_SLOT = -(1 << 20)


def _combine_kernel(first_ref, count_ref, x_ref, pos_ref, ye_hbm, o_ref, stage, sems, win_e, win_s, n_win):
    i = pl.program_id(0)
    half = i % 2

    def fetch(h, j, e, win):
        return pltpu.make_async_copy(ye_hbm.at[e, pl.ds(win * WINDOW, WINDOW), :], stage.at[h, j], sems.at[h, j])

    def request_tile(tile, h):
        n_req = jnp.int32(0)
        for e in range(N_EXPERTS):
            first = first_ref[e, tile]

            def request_one(k, j, e=e, first=first):
                win_e[h, j] = e
                win_s[h, j] = (first + k) * WINDOW
                fetch(h, j, e, first + k).start()
                return j + 1

            n_req = lax.fori_loop(0, count_ref[e, tile], request_one, n_req)
        n_win[h] = n_req

    @pl.when(i == 0)
    def _():
        stage[...] = jnp.zeros_like(stage)
        request_tile(0, 0)

    @pl.when(i + 1 < pl.num_programs(0))
    def _():
        request_tile(i + 1, 1 - half)

    n_here = n_win[half]
    slot_iota = lax.broadcasted_iota(jnp.int32, (WINDOW, COMBINE_TILE), 0)

    def group_body(g, acc):
        onehots = []
        for u in range(GROUP):
            j = g * GROUP + u
            live = j < n_here
            pos_row = pos_ref[jnp.where(live, win_e[half, j], 0)]
            base = jnp.where(live, win_s[half, j], NO_SLOT)
            onehots.append((pos_row == base + slot_iota).astype(jnp.bfloat16))
        onehot = jnp.concatenate(onehots, axis=0)
        for u in range(GROUP):
            j = g * GROUP + u

            @pl.when(j < n_here)
            def _(j=j):
                fetch(half, j, 0, 0).wait()

        rows = stage[half, pl.ds(g * GROUP, GROUP)].reshape(GROUP * WINDOW, D_MODEL)
        return acc + lax.dot_general(onehot, rows, (((0,), (0,)), ((), ())), preferred_element_type=jnp.float32)

    o_ref[...] = lax.fori_loop(0, (n_here + GROUP - 1) // GROUP, group_body, x_ref[...])


def _combine(xn, pos, offs, ye, cap):
    n = xn.shape[0]
    nt = n // COMBINE_TILE
    rows_per_tile = COMBINE_TILE // LANES
    start = offs[:, ::rows_per_tile, 0]
    stop = jnp.concatenate([start[:, 1:], jnp.full((N_EXPERTS, 1), cap, jnp.int32)], axis=1)
    first = start // WINDOW
    count = jnp.where(stop > start, (stop - 1) // WINDOW - first + 1, 0)
    pos_t = jnp.transpose(pos.reshape(N_EXPERTS, nt, COMBINE_TILE), (1, 0, 2)).reshape(nt, N_EXPERTS, 1, COMBINE_TILE)
    tok = pl.BlockSpec((COMBINE_TILE, D_MODEL), lambda i, f, c: (i, 0))
    return pl.pallas_call(
        _combine_kernel,
        grid_spec=pltpu.PrefetchScalarGridSpec(
            num_scalar_prefetch=2,
            grid=(nt,),
            in_specs=[tok,
                      pl.BlockSpec((None, N_EXPERTS, 1, COMBINE_TILE), lambda i, f, c: (i, 0, 0, 0)),
                      pl.BlockSpec(memory_space=pl.ANY)],
            out_specs=tok,
            scratch_shapes=[pltpu.VMEM((2, MAX_WINDOWS, WINDOW, D_MODEL), jnp.bfloat16),
                            pltpu.SemaphoreType.DMA((2, MAX_WINDOWS)),
                            pltpu.SMEM((2, MAX_WINDOWS), jnp.int32),
                            pltpu.SMEM((2, MAX_WINDOWS), jnp.int32),
                            pltpu.SMEM((2,), jnp.int32)],
        ),
        out_shape=jax.ShapeDtypeStruct((n, D_MODEL), jnp.float32),
        compiler_params=_cparams(("arbitrary",)),
        name="combine",
    )(first, count, xn, pos_t, ye)


def _pad_lora(w):
    z = jnp.zeros((N_DIR, N_DIR * w.shape[1], w.shape[2]), w.dtype)
    for d in range(N_DIR):
        z = z.at[d, d * w.shape[1]:(d + 1) * w.shape[1]].set(w[d])
    return z.astype(jnp.bfloat16)


def _layer_params(l, ln1_g, ln2_g, w_in, mu_prev, mu_next, q_norm_g, k_norm_g, w0, w2, a0, a2, g2, k_k, k_a,
                  r_k, lnx_g, lnx_b, w_out, router, w_gate, w_up, w_down):
    bf = lambda z: z.astype(jnp.bfloat16)
    row = lambda z: z.reshape(1, -1)
    return dict(
        ln1_g=ln1_g[l], ln2_g=row(ln2_g[l]), w_in=bf(w_in[l]), mu_prev=row(mu_prev[l]), mu_next=row(mu_next[l]),
        q_gain=row(jnp.tile(q_norm_g[l], ATT_SLOTS)), k_gain=row(jnp.tile(k_norm_g[l], ATT_SLOTS)),
        w0=w0[l], w2=_pad_lora(w2[l]), a0=a0[l], a2=_pad_lora(a2[l]), g2=bf(g2[l]),
        k_k=row(k_k[l]), k_a=row(k_a[l]), r_k=row(r_k[l]), lnx_g=row(lnx_g[l]), lnx_b=row(lnx_b[l]),
        w_out=bf(w_out[l]), router_t=router[l].T, w_gate=bf(w_gate[l]), w_up=bf(w_up[l]), w_down=bf(w_down[l]),
    )


def _encoder_layer(x, p, rope, bd):
    b, t, _ = x.shape
    n = b * t
    q, k, v, s = _in_proj(x, p['ln1_g'], p['w_in'], p['q_gain'], p['k_gain'], rope, bd)
    att_parts = [_band_attn(q[g], k[g], v[g]) for g in range(N_DIL)]
    lhs, add, bonus, gate, risk = _rwkv_prep(
        s, p['mu_prev'], p['mu_next'], p['w0'], p['w2'], p['a0'], p['a2'], p['g2'], p['k_k'], p['k_a'], p['r_k'], bd)
    chunked_ok = jnp.max(risk) <= RISK_LIMIT
    y = lax.cond(chunked_ok,
                 lambda: tuple(_rwkv_scan(lhs, add)),
                 lambda: _rwkv_steps(s, p['mu_prev'], p['mu_next'], p['w0'], p['w2'], p['a0'], p['a2'], p['k_k'],
                                     p['k_a'], bd))
    xn, h2, aff_t = _out_proj(x, att_parts, y, bonus, gate, p['lnx_g'], p['lnx_b'], p['w_out'], p['ln2_g'],
                              p['router_t'], bd)
    cap = max(1, (EC_CAPACITY_FACTOR * n) // N_EXPERTS)
    pos, offs, idx, gate_e = _route(aff_t.reshape(N_EXPERTS, n // LANES, LANES), cap)
    xs = h2.reshape(n, D_MODEL)[idx]
    ye = _expert_ffn(xs, gate_e[..., None], p['w_gate'], p['w_up'], p['w_down'])
    return _combine(xn.reshape(n, D_MODEL), pos, offs, ye, cap).reshape(b, t, D_MODEL)


def kernel(x_prompt, x_sample, ln1_g, ln2_g, w_in, mu_prev, mu_next, q_norm_g, k_norm_g, w0, w2, a0, a2, g2,
           k_k, k_a, r_k, lnx_g, lnx_b, w_out, router, w_gate, w_up, w_down):
    depth = w_in.shape[0]
    bd = _head_sum_matrix()
    rope_p = _rope_tables(x_prompt.shape[1])
    rope_s = _rope_tables(x_sample.shape[1])
    y_prompt, y_sample = x_prompt, x_sample
    for l in range(depth):
        p = _layer_params(l, ln1_g, ln2_g, w_in, mu_prev, mu_next, q_norm_g, k_norm_g, w0, w2, a0, a2, g2, k_k,
                          k_a, r_k, lnx_g, lnx_b, w_out, router, w_gate, w_up, w_down)
        y_prompt = _encoder_layer(y_prompt, p, rope_p, bd)
        y_sample = _encoder_layer(y_sample, p, rope_s, bd)
    return (y_prompt, y_sample)
```
